```python
import math
import jax, jax.numpy as jnp
from jax import lax
import numpy as np

D_MODEL = 1024
BATCH = 16
SEQ = 2048
DEPTH = 1
DEC_BATCH = 128
DEC_SEQ = 8
PAST_LEN = 8192
PAGE_SIZE = 128

CONV_CH = 512
CONV_GROUPS = 8
CONV_W = 3
N_HEADS = 8
QK_NOPE = 64
QK_ROPE = 32
V_DIM = 64
Q_LORA = 256
KV_LORA = 128
ATTN_W = N_HEADS * V_DIM
MIX_W = CONV_CH + ATTN_W
IN_COLS = 3 * CONV_CH + Q_LORA + KV_LORA + QK_ROPE
ROPE_BASE = 10000.0
SOFTMAX_SCALE = (QK_NOPE + QK_ROPE) ** -0.5
Q_BLOCK = 128
NEG_INF = -1e30
N_EXPERTS = 256
TOP_K = 8
N_GROUPS = 8
TOPK_GROUPS = 4
D_EXPERT = 256
D_SHARED = 256
ROUTED_SCALE = 2.5
EXPERT_BLOCK = 128
NORM_EPS = 1e-6
LN_EPS = 1e-5
DEEPNORM_ALPHA = (2.0 * DEPTH) ** 0.25
DEEPNORM_BETA = (8.0 * DEPTH) ** -0.25

kernel_name = 'hymba_conv_mla_moe_deepnorm_step'


def rms_norm(x, g):
    xf = x.astype(jnp.float32)
    y = xf * lax.rsqrt(jnp.mean(xf * xf, axis=-1, keepdims=True) + NORM_EPS)
    return (y * g.astype(jnp.float32)).astype(x.dtype)


def layer_norm(x, g, b):
    xf = x.astype(jnp.float32)
    mu = jnp.mean(xf, axis=-1, keepdims=True)
    var = jnp.mean(jnp.square(xf - mu), axis=-1, keepdims=True)
    y = (xf - mu) * lax.rsqrt(var + LN_EPS) * g.astype(jnp.float32) + b.astype(jnp.float32)
    return y.astype(x.dtype)


def rope_tables(pos, dtype):
    half = QK_ROPE // 2
    inv = ROPE_BASE ** (-jnp.arange(half, dtype=jnp.float32) / half)
    ang = pos.astype(jnp.float32)[:, None] * inv[None, :]
    return jnp.cos(ang).astype(dtype), jnp.sin(ang).astype(dtype)


def apply_rope(x, cos, sin):
    x1, x2 = jnp.split(x, 2, axis=-1)
    c, s = cos[:, None, :], sin[:, None, :]
    return jnp.concatenate([x1 * c - x2 * s, x1 * s + x2 * c], axis=-1)


def mixer_inputs(x, lw, pos):
    b, t, _ = x.shape
    h = jnp.einsum('btd,dc->btc', x, lw['w_in'])
    o1, o2, o3 = CONV_CH, 2 * CONV_CH, 3 * CONV_CH
    o4 = o3 + Q_LORA
    o5 = o4 + KV_LORA
    b_gate, c_gate, x_conv = h[..., :o1], h[..., o1:o2], h[..., o2:o3]
    q_a, c_kv, k_pe = h[..., o3:o4], h[..., o4:o5], h[..., o5:]
    u = c_gate * x_conv
    q = jnp.einsum('btr,rf->btf', rms_norm(q_a, lw['q_norm']), lw['w_uq'])
    q = q.reshape(b, t, N_HEADS, QK_NOPE + QK_ROPE)
    cos, sin = rope_tables(pos, x.dtype)
    q_pe = apply_rope(q[..., QK_NOPE:], cos, sin)
    k_pe = apply_rope(k_pe[:, :, None, :], cos, sin)[:, :, 0, :]
    c_kv = rms_norm(c_kv, lw['kv_norm'])
    q_lat = jnp.einsum('bthn,chn->bthc', q[..., :QK_NOPE], lw['w_uk'])
    return b_gate, u, q_lat, q_pe, c_kv, k_pe


def short_conv(u_ext, conv_w, t):
    y = conv_w[0] * u_ext[:, 0:t]
    for j in range(1, CONV_W):
        y = y + conv_w[j] * u_ext[:, j:j + t]
    return y


def latent_scores(q_lat, q_pe, c_kv, k_pe):
    s = jnp.einsum('bqhc,bkc->bhqk', q_lat, c_kv, preferred_element_type=jnp.float32)
    s = s + jnp.einsum('bqhr,bkr->bhqk', q_pe, k_pe, preferred_element_type=jnp.float32)
    return s * SOFTMAX_SCALE


def prompt_attention(q_lat, q_pe, c_kv, k_pe):
    b, s = q_lat.shape[:2]
    nb = s // Q_BLOCK
    kpos = jnp.arange(s)

    def block(i):
        q0 = i * Q_BLOCK
        ql = lax.dynamic_slice_in_dim(q_lat, q0, Q_BLOCK, axis=1)
        qp = lax.dynamic_slice_in_dim(q_pe, q0, Q_BLOCK, axis=1)
        sc = latent_scores(ql, qp, c_kv, k_pe)
        qpos = q0 + jnp.arange(Q_BLOCK)
        sc = jnp.where(kpos[None, :] <= qpos[:, None], sc, NEG_INF)
        p = jax.nn.softmax(sc, axis=-1).astype(c_kv.dtype)
        return jnp.einsum('bhqk,bkc->bqhc', p, c_kv)

    out = lax.map(block, jnp.arange(nb))
    return out.transpose(1, 0, 2, 3, 4).reshape(b, s, N_HEADS, KV_LORA)


def sample_attention(q_lat, q_pe, c_past, kpe_past, c_new, kpe_new):
    t = q_lat.shape[1]
    past_len = c_past.shape[1]
    s_past = latent_scores(q_lat, q_pe, c_past, kpe_past)
    causal = jnp.arange(t)[None, :] <= jnp.arange(t)[:, None]
    s_new = jnp.where(causal, latent_scores(q_lat, q_pe, c_new, kpe_new), NEG_INF)
    p = jax.nn.softmax(jnp.concatenate([s_past, s_new], axis=-1), axis=-1).astype(c_new.dtype)
    return (jnp.einsum('bhqk,bkc->bqhc', p[..., :past_len], c_past)
            + jnp.einsum('bhqk,bkc->bqhc', p[..., past_len:], c_new))


def route(x2, w_router, router_bias):
    n = x2.shape[0]
    s = jax.nn.sigmoid(jnp.einsum('nd,de->ne', x2, w_router, preferred_element_type=jnp.float32))
    sb = s + router_bias.astype(jnp.float32)
    gscore = lax.top_k(sb.reshape(n, N_GROUPS, N_EXPERTS // N_GROUPS), 2)[0].sum(-1)
    _, gidx = lax.top_k(gscore, TOPK_GROUPS)
    gmask = jax.nn.one_hot(gidx, N_GROUPS, dtype=jnp.float32).sum(1) > 0
    emask = jnp.repeat(gmask, N_EXPERTS // N_GROUPS, axis=1)
    _, idx = lax.top_k(jnp.where(emask, sb, -jnp.inf), TOP_K)
    w = jnp.take_along_axis(s, idx, axis=1)
    w = w / jnp.sum(w, axis=-1, keepdims=True) * ROUTED_SCALE
    return idx.astype(jnp.int32), w


def moe_routed(x2, idx, wts, w_gate, w_up, w_down):
    n, d = x2.shape
    a = n * TOP_K
    n_blocks = -(-a // EXPERT_BLOCK) + N_EXPERTS
    flat_e = idx.reshape(a)
    order = jnp.argsort(flat_e)
    e_sorted = flat_e[order]
    counts = jnp.zeros((N_EXPERTS,), jnp.int32).at[flat_e].add(1)
    padded = (counts + EXPERT_BLOCK - 1) // EXPERT_BLOCK * EXPERT_BLOCK
    pad_end = jnp.cumsum(padded)
    pad_start = pad_end - padded
    start = jnp.cumsum(counts) - counts
    dest = pad_start[e_sorted] + jnp.arange(a, dtype=jnp.int32) - start[e_sorted]
    slot_tok = jnp.full((n_blocks * EXPERT_BLOCK,), n, jnp.int32).at[dest].set((order // TOP_K).astype(jnp.int32))
    slot_w = jnp.zeros((n_blocks * EXPERT_BLOCK,), x2.dtype).at[dest].set(wts.reshape(a)[order].astype(x2.dtype))
    block_pos = jnp.arange(n_blocks, dtype=jnp.int32) * EXPERT_BLOCK
    block_e = jnp.minimum(jnp.searchsorted(pad_end, block_pos, side='right'), N_EXPERTS - 1).astype(jnp.int32)
    x_pad = jnp.concatenate([x2, jnp.zeros((1, d), x2.dtype)], axis=0)

    def body(acc, blk):
        tok, w, e = blk
        xb = x_pad[tok]
        h = jax.nn.silu(xb @ w_gate[e]) * (xb @ w_up[e])
        return acc.at[tok].add((h @ w_down[e]) * w[:, None]), None

    acc, _ = lax.scan(body, jnp.zeros((n + 1, d), x2.dtype),
                      (slot_tok.reshape(n_blocks, EXPERT_BLOCK),
                       slot_w.reshape(n_blocks, EXPERT_BLOCK), block_e))
    return acc[:n]


def channel_mixer(x, lw):
    b, t, d = x.shape
    x2 = x.reshape(b * t, d)
    idx, wts = route(x2, lw['w_router'], lw['router_bias'])
    routed = moe_routed(x2, idx, wts, lw['w_gate'], lw['w_up'], lw['w_down'])
    shared = (jax.nn.silu(x2 @ lw['ws_gate']) * (x2 @ lw['ws_up'])) @ lw['ws_down']
    return (routed + shared).reshape(b, t, d)


def finish_layer(x, b_gate, conv_y, attn_lat, lw):
    b, t, _ = x.shape
    y_conv = rms_norm(b_gate * conv_y, lw['g_conv'])
    o = jnp.einsum('bthc,chv->bthv', attn_lat, lw['w_uv']).reshape(b, t, ATTN_W)
    y_attn = rms_norm(o, lw['g_attn'])
    mix = jnp.einsum('btm,md->btd', jnp.concatenate([y_conv, y_attn], axis=-1), lw['w_o'])
    x = layer_norm(DEEPNORM_ALPHA * x + mix, lw['ln1_g'], lw['ln1_b'])
    return layer_norm(DEEPNORM_ALPHA * x + channel_mixer(x, lw), lw['ln2_g'], lw['ln2_b'])


def prompt_layer(x, lw):
    b, s, _ = x.shape
    b_gate, u, q_lat, q_pe, c_kv, k_pe = mixer_inputs(x, lw, jnp.arange(s, dtype=jnp.int32))
    u_ext = jnp.concatenate([jnp.zeros((b, CONV_W - 1, CONV_CH), u.dtype), u], axis=1)
    conv_y = short_conv(u_ext, lw['conv_w'], s)
    attn_lat = prompt_attention(q_lat, q_pe, c_kv, k_pe)
    x = finish_layer(x, b_gate, conv_y, attn_lat, lw)
    return x, c_kv, k_pe, u_ext[:, s:]


def sample_layer(x, c_past, kpe_past, conv_state, lw):
    b, t, _ = x.shape
    past_len = c_past.shape[1]
    pos = past_len + jnp.arange(t, dtype=jnp.int32)
    b_gate, u, q_lat, q_pe, c_kv, k_pe = mixer_inputs(x, lw, pos)
    u_ext = jnp.concatenate([conv_state.astype(u.dtype), u], axis=1)
    conv_y = short_conv(u_ext, lw['conv_w'], t)
    attn_lat = sample_attention(q_lat, q_pe, c_past.astype(c_kv.dtype), kpe_past.astype(k_pe.dtype), c_kv, k_pe)
    x = finish_layer(x, b_gate, conv_y, attn_lat, lw)
    return x, c_kv, k_pe, u_ext[:, t:]


def setup_inputs(seed: int = 0) -> dict:
    key = jax.random.key(seed)
    ks = jax.random.split(key, 32)
    n_pages = PAST_LEN // PAGE_SIZE
    n_used = DEC_BATCH * n_pages
    n_pool = n_used + n_used // 4

    def dense(k, shape, fan_in, scale=1.0):
        return jax.random.normal(k, shape, jnp.float32) * (scale * fan_in ** -0.5)

    def gain(k, shape):
        return 1.0 + 0.02 * jax.random.normal(k, shape, jnp.float32)

    def small(k, shape, s=0.02):
        return s * jax.random.normal(k, shape, jnp.float32)

    perm = jax.random.permutation(ks[5], n_pool).astype(jnp.int32)
    page_table = perm[:n_used].reshape(DEC_BATCH, n_pages)
    L, E = DEPTH, N_EXPERTS
    return {
        'x_prompt': jax.random.normal(ks[0], (BATCH, SEQ, D_MODEL), jnp.float32),
        'x_sample': jax.random.normal(ks[1], (DEC_BATCH, DEC_SEQ, D_MODEL), jnp.float32),
        'cache_kv_latent': jax.random.normal(ks[2], (L, n_pool, PAGE_SIZE, KV_LORA), jnp.float32),
        'cache_k_rope': jax.random.normal(ks[3], (L, n_pool, PAGE_SIZE, QK_ROPE), jnp.float32),
        'state_conv': jax.random.normal(ks[4], (L, DEC_BATCH, CONV_W - 1, CONV_CH), jnp.float32),
        'page_table': page_table,
        'w_in': dense(ks[6], (L, D_MODEL, IN_COLS), D_MODEL),
        'conv_w': dense(ks[7], (L, CONV_W, CONV_CH), CONV_W),
        'q_norm': gain(ks[8], (L, Q_LORA)),
        'w_uq': dense(ks[9], (L, Q_LORA, N_HEADS * (QK_NOPE + QK_ROPE)), Q_LORA),
        'kv_norm': gain(ks[10], (L, KV_LORA)),
        'w_uk': dense(ks[11], (L, KV_LORA, N_HEADS, QK_NOPE), KV_LORA),
        'w_uv': dense(ks[12], (L, KV_LORA, N_HEADS, V_DIM), KV_LORA, DEEPNORM_BETA),
        'g_conv': gain(ks[13], (L, CONV_CH)),
        'g_attn': gain(ks[14], (L, ATTN_W)),
        'w_o': dense(ks[15], (L, MIX_W, D_MODEL), MIX_W, DEEPNORM_BETA),
        'ln1_g': gain(ks[16], (L, D_MODEL)),
        'ln1_b': small(ks[17], (L, D_MODEL)),
        'w_router': dense(ks[18], (L, D_MODEL, E), D_MODEL),
        'router_bias': small(ks[19], (L, E), 0.01),
        'w_gate': dense(ks[20], (L, E, D_MODEL, D_EXPERT), D_MODEL, DEEPNORM_BETA),
        'w_up': dense(ks[21], (L, E, D_MODEL, D_EXPERT), D_MODEL, DEEPNORM_BETA),
        'w_down': dense(ks[22], (L, E, D_EXPERT, D_MODEL), D_EXPERT, DEEPNORM_BETA),
        'ws_gate': dense(ks[23], (L, D_MODEL, D_SHARED), D_MODEL, DEEPNORM_BETA),
        'ws_up': dense(ks[24], (L, D_MODEL, D_SHARED), D_MODEL, DEEPNORM_BETA),
        'ws_down': dense(ks[25], (L, D_SHARED, D_MODEL), D_SHARED, DEEPNORM_BETA),
        'ln2_g': gain(ks[26], (L, D_MODEL)),
        'ln2_b': small(ks[27], (L, D_MODEL)),
    }


def reference(x_prompt, x_sample, cache_kv_latent, cache_k_rope, state_conv, page_table,
              w_in, conv_w, q_norm, w_uq, kv_norm, w_uk, w_uv, g_conv, g_attn, w_o,
              ln1_g, ln1_b, w_router, router_bias, w_gate, w_up, w_down,
              ws_gate, ws_up, ws_down, ln2_g, ln2_b):
    db = x_sample.shape[0]
    xp, xs = x_prompt, x_sample
    kv_p, kpe_p, cs_p, kv_s, kpe_s, cs_s = [], [], [], [], [], []
    for l in range(DEPTH):
        lw = {'w_in': w_in[l], 'conv_w': conv_w[l], 'q_norm': q_norm[l], 'w_uq': w_uq[l],
              'kv_norm': kv_norm[l], 'w_uk': w_uk[l], 'w_uv': w_uv[l], 'g_conv': g_conv[l],
              'g_attn': g_attn[l], 'w_o': w_o[l], 'ln1_g': ln1_g[l], 'ln1_b': ln1_b[l],
              'w_router': w_router[l], 'router_bias': router_bias[l], 'w_gate': w_gate[l],
              'w_up': w_up[l], 'w_down': w_down[l], 'ws_gate': ws_gate[l], 'ws_up': ws_up[l],
              'ws_down': ws_down[l], 'ln2_g': ln2_g[l], 'ln2_b': ln2_b[l]}
        xp, c_new, k_new, conv_new = prompt_layer(xp, lw)
        kv_p.append(c_new); kpe_p.append(k_new); cs_p.append(conv_new)
        c_past = cache_kv_latent[l][page_table].reshape(db, -1, KV_LORA)
        kpe_past = cache_k_rope[l][page_table].reshape(db, -1, QK_ROPE)
        xs, c_new, k_new, conv_new = sample_layer(xs, c_past, kpe_past, state_conv[l], lw)
        kv_s.append(c_new); kpe_s.append(k_new); cs_s.append(conv_new)
    return (xp, xs, jnp.stack(kv_p), jnp.stack(kpe_p), jnp.stack(cs_p),
            jnp.stack(kv_s), jnp.stack(kpe_s), jnp.stack(cs_s))
```

```python
import functools
import math

import jax
import jax.numpy as jnp
from jax import lax
from jax.experimental import pallas as pl
from jax.experimental.pallas import tpu as pltpu

F32 = jnp.float32
BF16 = jnp.bfloat16
U32 = jnp.uint32

ROPE_BASE = 10000.0
NORM_EPS = 1e-6
LN_EPS = 1e-5
NEG_INF = -1e30
TOP_K = 8
N_GROUPS = 8
TOPK_GROUPS = 4
ROUTED_SCALE = 2.5
CONV_W = 3

LANES = 128
SLOT_ROWS = 128
VMEM_LIMIT = 52 * 1024 * 1024

_NT = (((1,), (1,)), ((), ()))


def _cparams(sem):
    return pltpu.CompilerParams(dimension_semantics=sem, vmem_limit_bytes=VMEM_LIMIT)


def _rms(x, g):
    return x * lax.rsqrt(jnp.mean(x * x, axis=-1, keepdims=True) + NORM_EPS) * g


def _layer_norm(x, g, b):
    mu = jnp.mean(x, axis=-1, keepdims=True)
    xc = x - mu
    var = jnp.mean(xc * xc, axis=-1, keepdims=True)
    return xc * lax.rsqrt(var + LN_EPS) * g + b


def _silu(x):
    return x / (1.0 + jnp.exp(-x))


def _rope_tile(p, c, s, half):
    lane = lax.broadcasted_iota(jnp.int32, p.shape, 1)
    swapped = jnp.where(lane < half, pltpu.roll(p, LANES - half, axis=1), pltpu.roll(p, half, axis=1))
    return p * c + swapped * s


def _pack_halves(x):
    half = x.shape[1] // 2
    bits = lax.bitcast_convert_type(x.astype(BF16).astype(F32), U32)
    return bits[:, half:] | (bits[:, :half] >> 16)


def _unpack_halves(p):
    lo = lax.bitcast_convert_type(p << 16, F32)
    hi = lax.bitcast_convert_type(p & jnp.uint32(0xFFFF0000), F32)
    return lo, hi


def _fold_kernel(a_ref, b_ref, o_ref, *, scale):
    o_ref[...] = jnp.dot(a_ref[...], b_ref[...], precision=lax.Precision.HIGHEST,
                         preferred_element_type=F32) * scale


def _fold_qlat(wuq_nope, wuk_t, scale):
    h, r, n = wuq_nope.shape
    c = wuk_t.shape[-1]
    return pl.pallas_call(
        functools.partial(_fold_kernel, scale=scale),
        grid=(h,),
        in_specs=[pl.BlockSpec((None, r, n), lambda i: (i, 0, 0)),
                  pl.BlockSpec((None, n, c), lambda i: (i, 0, 0))],
        out_specs=pl.BlockSpec((None, r, c), lambda i: (i, 0, 0)),
        out_shape=jax.ShapeDtypeStruct((h, r, c), F32),
        compiler_params=_cparams(("parallel",)),
    )(wuq_nope, wuk_t)


def _mixer_in_kernel(*refs, prompt, tiles_per_seq, tm, t_new, n_heads, conv_ch, q_lora, kv_lora, rope):
    if prompt:
        (x_ref, win_ref, wq_ref, qn_ref, kvn_ref, cw_ref, gc_ref, cos_ref, sin_ref,
         q_ref, kc_ref, ckv_ref, kpe_ref, u_ref, yc_ref, carry_ref) = refs
    else:
        (x_ref, win_ref, wq_ref, qn_ref, kvn_ref, cw_ref, gc_ref, cos_ref, sin_ref, p1_ref, p2_ref,
         q_ref, kc_ref, ckv_ref, kpe_ref, u_ref, yc_ref) = refs
    half = rope // 2
    c1, c2, c3 = conv_ch, 2 * conv_ch, 3 * conv_ch
    c4 = c3 + q_lora
    c5 = c4 + kv_lora

    h = jnp.dot(x_ref[...].astype(BF16), win_ref[...], preferred_element_type=F32)
    b_gate, c_gate, x_conv = h[:, :c1], h[:, c1:c2], h[:, c2:c3]
    q_a, c_kv, kp = h[:, c3:c4], h[:, c4:c5], h[:, c5:c5 + LANES]

    u = c_gate * x_conv
    u_ref[...] = u
    row = lax.broadcasted_iota(jnp.int32, (tm, 1), 0)
    if prompt:
        first = (pl.program_id(0) % tiles_per_seq) == 0
        prev1 = jnp.where(first, 0.0, carry_ref[7:8, :])
        prev2 = jnp.where(first, 0.0, carry_ref[6:7, :])
        t_in = row
        p1 = prev1
        p2 = jnp.where(row == 0, prev2, prev1)
        carry_ref[...] = u[tm - 8:, :]
    else:
        t_in = row & (t_new - 1)
        p1 = p1_ref[...]
        p2 = p2_ref[...]
    um1 = jnp.where(t_in == 0, p1, pltpu.roll(u, 1, axis=0))
    um2 = jnp.where(t_in < 2, p2, pltpu.roll(u, 2, axis=0))
    conv_y = cw_ref[0:1, :] * um2 + cw_ref[1:2, :] * um1 + cw_ref[2:3, :] * u
    yc_ref[...] = _rms(b_gate * conv_y, gc_ref[...]).astype(yc_ref.dtype)

    cos = cos_ref[...]
    sin = sin_ref[...]
    qn = _rms(q_a, qn_ref[...]).astype(BF16)
    q = jnp.dot(qn, wq_ref[...], preferred_element_type=F32)
    for hd in range(n_heads):
        o = hd * 2 * LANES
        q_ref[:, o:o + LANES] = q[:, o:o + LANES].astype(q_ref.dtype)
        q_ref[:, o + LANES:o + 2 * LANES] = _rope_tile(q[:, o + LANES:o + 2 * LANES], cos, sin, half).astype(q_ref.dtype)

    ckv = _rms(c_kv, kvn_ref[...])
    ckv_ref[...] = ckv
    kpr = _rope_tile(kp, cos, sin, half)
    kpe_ref[...] = kpr[:, :rope]
    kc_ref[:, :LANES] = ckv.astype(BF16)
    kc_ref[:, LANES:] = kpr.astype(BF16)


def _mixer_in(x2, win, wq, qn, kvn, cw, gc, cos, sin, prev, *, prompt, seq, tm, n_heads, rope, q_dtype):
    n, d = x2.shape
    conv_ch = cw.shape[-1]
    q_lora = qn.shape[-1]
    kv_lora = kvn.shape[-1]
    assert n % tm == 0 and kv_lora == LANES and tm % 8 == 0
    tiles_per_seq = seq // tm if prompt else 1
    if prompt:
        assert seq % tm == 0
    else:
        assert seq & (seq - 1) == 0 and tm % seq == 0
    const = lambda i: (0, 0)
    tile = lambda i: (i, 0)
    in_specs = [
        pl.BlockSpec((tm, d), tile),
        pl.BlockSpec(win.shape, const),
        pl.BlockSpec(wq.shape, const),
        pl.BlockSpec(qn.shape, const),
        pl.BlockSpec(kvn.shape, const),
        pl.BlockSpec(cw.shape, const),
        pl.BlockSpec(gc.shape, const),
    ]
    args = [x2, win, wq, qn, kvn, cw, gc, cos, sin]
    if prompt:
        in_specs += [pl.BlockSpec((tm, LANES), lambda i: (i % tiles_per_seq, 0))] * 2
        scratch = [pltpu.VMEM((8, conv_ch), F32)]
    else:
        in_specs += [pl.BlockSpec((tm, LANES), const)] * 2
        in_specs += [pl.BlockSpec((tm, conv_ch), tile)] * 2
        args += list(prev)
        scratch = []
    qw = wq.shape[-1]
    out_shape = [
        jax.ShapeDtypeStruct((n, qw), q_dtype),
        jax.ShapeDtypeStruct((n, 2 * LANES), BF16),
        jax.ShapeDtypeStruct((n, kv_lora), F32),
        jax.ShapeDtypeStruct((n, rope), F32),
        jax.ShapeDtypeStruct((n, conv_ch), F32),
        jax.ShapeDtypeStruct((n, conv_ch), BF16),
    ]
    out_specs = [
        pl.BlockSpec((tm, qw), tile),
        pl.BlockSpec((tm, 2 * LANES), tile),
        pl.BlockSpec((tm, kv_lora), tile),
        pl.BlockSpec((tm, rope), tile),
        pl.BlockSpec((tm, conv_ch), tile),
        pl.BlockSpec((tm, conv_ch), tile),
    ]
    return pl.pallas_call(
        functools.partial(_mixer_in_kernel, prompt=prompt, tiles_per_seq=tiles_per_seq, tm=tm, t_new=seq,
                          n_heads=n_heads, conv_ch=conv_ch, q_lora=q_lora, kv_lora=kv_lora, rope=rope),
        grid=(n // tm,),
        in_specs=in_specs,
        out_specs=out_specs,
        out_shape=out_shape,
        scratch_shapes=scratch,
        compiler_params=_cparams(("arbitrary",)),
    )(*args)


def _softmax_update(s, v, m_ref, l_ref, acc_ref):
    m_prev = m_ref[...]
    m_new = jnp.maximum(m_prev, jnp.max(s, axis=1, keepdims=True))
    alpha = jnp.exp(m_prev - m_new)
    p = jnp.exp(s - m_new)
    l_ref[...] = alpha * l_ref[...] + jnp.sum(p, axis=1, keepdims=True)
    acc_ref[...] = alpha * acc_ref[...] + jnp.dot(p.astype(BF16), v, preferred_element_type=F32)
    m_ref[...] = m_new


def _prompt_attn_kernel(q_ref, k_ref, o_ref, qs_ref, m_ref, l_ref, acc_ref, *, tq, n_heads):
    i = pl.program_id(1)
    for hd in range(n_heads):
        qs_ref[hd * tq:(hd + 1) * tq, :] = q_ref[:, hd * 2 * LANES:(hd + 1) * 2 * LANES]
    m_ref[...] = jnp.full(m_ref.shape, NEG_INF, F32)
    l_ref[...] = jnp.zeros(l_ref.shape, F32)
    acc_ref[...] = jnp.zeros(acc_ref.shape, F32)

    def step(j, masked):
        k = k_ref[pl.ds(pl.multiple_of(j * tq, tq), tq), :]
        s = lax.dot_general(qs_ref[...], k, _NT, preferred_element_type=F32)
        if masked:
            t = lax.broadcasted_iota(jnp.int32, s.shape, 0) & (tq - 1)
            c = lax.broadcasted_iota(jnp.int32, s.shape, 1)
            s = jnp.where(c <= t, s, NEG_INF)
        _softmax_update(s, k[:, :LANES], m_ref, l_ref, acc_ref)

    def body(j, carry):
        step(j, False)
        return carry

    lax.fori_loop(0, i, body, 0)
    step(i, True)
    out = acc_ref[...] / l_ref[...]
    for hd in range(n_heads):
        o_ref[:, hd * LANES:(hd + 1) * LANES] = out[hd * tq:(hd + 1) * tq, :].astype(o_ref.dtype)


def _prompt_attention(q, kc, *, batch, seq, tq, n_heads):
    n = q.shape[0]
    nq = seq // tq
    rows = n_heads * tq
    assert tq & (tq - 1) == 0 and seq % tq == 0
    return pl.pallas_call(
        functools.partial(_prompt_attn_kernel, tq=tq, n_heads=n_heads),
        grid=(batch, nq),
        in_specs=[pl.BlockSpec((tq, q.shape[1]), lambda b, i: (b * nq + i, 0)),
                  pl.BlockSpec((seq, kc.shape[1]), lambda b, i: (b, 0))],
        out_specs=pl.BlockSpec((tq, n_heads * LANES), lambda b, i: (b * nq + i, 0)),
        out_shape=jax.ShapeDtypeStruct((n, n_heads * LANES), BF16),
        scratch_shapes=[pltpu.VMEM((rows, 2 * LANES), BF16),
                        pltpu.VMEM((rows, 1), F32),
                        pltpu.VMEM((rows, 1), F32),
                        pltpu.VMEM((rows, LANES), F32)],
        compiler_params=_cparams(("parallel", "arbitrary")),
    )(q, kc)


def _sample_attn_kernel(pt_ref, q_ref, ckvn_ref, kpen_ref, *rest, pps, n_heads, t_new, rope, page):
    del pt_ref
    kv_refs = rest[:pps]
    rp_refs = rest[pps:2 * pps]
    o_ref, qs_ref, m_ref, l_ref, acc_ref = rest[2 * pps:]
    j = pl.program_id(1)

    @pl.when(j == 0)
    def _():
        for hd in range(n_heads):
            qs_ref[hd * t_new:(hd + 1) * t_new, :] = q_ref[:, hd * 2 * LANES:(hd + 1) * 2 * LANES]
        m_ref[...] = jnp.full(m_ref.shape, NEG_INF, F32)
        l_ref[...] = jnp.zeros(l_ref.shape, F32)
        acc_ref[...] = jnp.zeros(acc_ref.shape, F32)

    qs = qs_ref[...]
    ql = qs[:, :LANES].astype(BF16)
    qp = qs[:, LANES:LANES + rope].astype(BF16)
    kv = jnp.concatenate([r[...] for r in kv_refs], axis=0).astype(BF16)
    rp = jnp.concatenate([r[...] for r in rp_refs], axis=0).astype(BF16)
    s = (lax.dot_general(ql, kv, _NT, preferred_element_type=F32)
         + lax.dot_general(qp, rp, _NT, preferred_element_type=F32))
    _softmax_update(s, kv, m_ref, l_ref, acc_ref)

    @pl.when(j == pl.num_programs(1) - 1)
    def _():
        kn = jnp.concatenate([ckvn_ref[...], jnp.zeros((page - t_new, LANES), F32)], axis=0).astype(BF16)
        rn = jnp.concatenate([kpen_ref[...], jnp.zeros((page - t_new, rope), F32)], axis=0).astype(BF16)
        s2 = (lax.dot_general(ql, kn, _NT, preferred_element_type=F32)
              + lax.dot_general(qp, rn, _NT, preferred_element_type=F32))
        t = lax.broadcasted_iota(jnp.int32, s2.shape, 0) & (t_new - 1)
        c = lax.broadcasted_iota(jnp.int32, s2.shape, 1)
        s2 = jnp.where(c <= t, s2, NEG_INF)
        _softmax_update(s2, kn, m_ref, l_ref, acc_ref)
        out = acc_ref[...] / l_ref[...]
        for hd in range(n_heads):
            o_ref[:, hd * LANES:(hd + 1) * LANES] = out[hd * t_new:(hd + 1) * t_new, :]


def _sample_attention(q, ckv_new, kpe_new, cache_kv, cache_rope, page_table, layer, *, n_heads, t_new, pps):
    n = q.shape[0]
    db = n // t_new
    n_pages = page_table.shape[1]
    page = cache_kv.shape[2]
    rope = cache_rope.shape[-1]
    assert n_pages % pps == 0 and t_new == 8 and cache_kv.shape[-1] == LANES
    steps = n_pages // pps
    rows = n_heads * t_new

    def page_spec(width, k):
        return pl.BlockSpec((None, None, page, width), lambda b, j, pt, k=k: (layer, pt[b, j * pps + k], 0, 0))

    in_specs = [pl.BlockSpec((t_new, q.shape[1]), lambda b, j, pt: (b, 0)),
                pl.BlockSpec((t_new, LANES), lambda b, j, pt: (b, 0)),
                pl.BlockSpec((t_new, rope), lambda b, j, pt: (b, 0))]
    in_specs += [page_spec(LANES, k) for k in range(pps)]
    in_specs += [page_spec(rope, k) for k in range(pps)]
    return pl.pallas_call(
        functools.partial(_sample_attn_kernel, pps=pps, n_heads=n_heads, t_new=t_new, rope=rope, page=page),
        grid_spec=pltpu.PrefetchScalarGridSpec(
            num_scalar_prefetch=1,
            grid=(db, steps),
            in_specs=in_specs,
            out_specs=pl.BlockSpec((t_new, n_heads * LANES), lambda b, j, pt: (b, 0)),
            scratch_shapes=[pltpu.VMEM((rows, 2 * LANES), F32),
                            pltpu.VMEM((rows, 1), F32),
                            pltpu.VMEM((rows, 1), F32),
                            pltpu.VMEM((rows, LANES), F32)]),
        out_shape=jax.ShapeDtypeStruct((n, n_heads * LANES), F32),
        compiler_params=_cparams(("parallel", "arbitrary")),
    )(page_table, q, ckv_new, kpe_new, *([cache_kv] * pps), *([cache_rope] * pps))


def _mixer_out_kernel(xp_ref, xs_ref, ycp_ref, ycs_ref, atp_ref, ats_ref, wuv_ref, ga_ref, wo_ref, g_ref, b_ref,
                      x1_ref, x1p_ref, *, n_prompt_tiles, conv_ch, alpha):
    is_p = pl.program_id(0) < n_prompt_tiles
    x = jnp.where(is_p, xp_ref[...], xs_ref[...])
    yc = jnp.where(is_p, ycp_ref[...], ycs_ref[...])
    at = jnp.where(is_p, atp_ref[...], ats_ref[...].astype(BF16))
    o = jnp.dot(at, wuv_ref[...], preferred_element_type=F32)
    ya = _rms(o, ga_ref[...]).astype(BF16)
    mix = (jnp.dot(yc, wo_ref[:conv_ch, :], preferred_element_type=F32)
           + jnp.dot(ya, wo_ref[conv_ch:, :], preferred_element_type=F32))
    x1 = _layer_norm(alpha * x + mix, g_ref[...], b_ref[...])
    x1_ref[...] = x1
    x1p_ref[...] = _pack_halves(x1)


def _mixer_out(xp, xs, ycp, ycs, atp, ats, wuv_bd, ga, wo, g, b, *, tm, alpha):
    np_, d = xp.shape
    ns = xs.shape[0]
    assert np_ % tm == 0 and ns % tm == 0
    npt, nst = np_ // tm, ns // tm
    conv_ch = ycp.shape[1]
    p_map = lambda i: (jnp.minimum(i, npt - 1), 0)
    s_map = lambda i: (jnp.maximum(i - npt, 0), 0)
    const = lambda i: (0, 0)
    tile = lambda i: (i, 0)
    nt = np_ + ns
    return pl.pallas_call(
        functools.partial(_mixer_out_kernel, n_prompt_tiles=npt, conv_ch=conv_ch, alpha=alpha),
        grid=(npt + nst,),
        in_specs=[pl.BlockSpec((tm, d), p_map), pl.BlockSpec((tm, d), s_map),
                  pl.BlockSpec((tm, conv_ch), p_map), pl.BlockSpec((tm, conv_ch), s_map),
                  pl.BlockSpec((tm, atp.shape[1]), p_map), pl.BlockSpec((tm, ats.shape[1]), s_map),
                  pl.BlockSpec(wuv_bd.shape, const), pl.BlockSpec(ga.shape, const),
                  pl.BlockSpec(wo.shape, const), pl.BlockSpec(g.shape, const), pl.BlockSpec(b.shape, const)],
        out_specs=[pl.BlockSpec((tm, d), tile), pl.BlockSpec((tm, d // 2), tile)],
        out_shape=[jax.ShapeDtypeStruct((nt, d), F32), jax.ShapeDtypeStruct((nt, d // 2), U32)],
        compiler_params=_cparams(("parallel",)),
    )(xp, xs, ycp, ycs, atp, ats, wuv_bd, ga, wo, g, b)


def _router_kernel(x_ref, wh_ref, wl_ref, b_ref, tri_ref, idx_ref, w_ref, rank_ref, cnt_ref, base_ref,
                   *, tt, n_exp):
    @pl.when(pl.program_id(0) == 0)
    def _():
        base_ref[...] = jnp.zeros(base_ref.shape, F32)

    x = x_ref[...]
    xh = x.astype(BF16)
    xl = (x - xh.astype(F32)).astype(BF16)
    wh = wh_ref[...]
    z = (lax.dot_general(wh, xh, _NT, preferred_element_type=F32)
         + lax.dot_general(wh, xl, _NT, preferred_element_type=F32)
         + lax.dot_general(wl_ref[...], xh, _NT, preferred_element_type=F32))
    s = 1.0 / (1.0 + jnp.exp(-z))
    sb = s + b_ref[...]

    gsz = n_exp // N_GROUPS
    git = lax.broadcasted_iota(jnp.int32, (gsz, tt), 0).astype(F32)
    blocks, gscore = [], []
    for g in range(N_GROUPS):
        blk = sb[g * gsz:(g + 1) * gsz, :]
        m1 = jnp.max(blk, axis=0, keepdims=True)
        f1 = jnp.min(jnp.where(blk == m1, git, float(gsz)), axis=0, keepdims=True)
        m2 = jnp.max(jnp.where(git == f1, -jnp.inf, blk), axis=0, keepdims=True)
        blocks.append(blk)
        gscore.append(m1 + m2)
    masked = []
    for g in range(N_GROUPS):
        ahead = jnp.zeros((1, tt), F32)
        for g2 in range(N_GROUPS):
            if g2 == g:
                continue
            beats = (gscore[g2] >= gscore[g]) if g2 < g else (gscore[g2] > gscore[g])
            ahead = ahead + jnp.where(beats, 1.0, 0.0)
        masked.append(jnp.where(ahead < float(TOPK_GROUPS), blocks[g], -jnp.inf))
    vals = jnp.concatenate(masked, axis=0)

    rowi = lax.broadcasted_iota(jnp.int32, (n_exp, tt), 0).astype(F32)
    chosen = jnp.zeros((n_exp, tt), F32)
    picks, wks = [], []
    for k in range(TOP_K):
        m = jnp.max(vals, axis=0, keepdims=True)
        ik = jnp.min(jnp.where(vals == m, rowi, float(n_exp)), axis=0, keepdims=True)
        hit = rowi == ik
        wks.append(jnp.sum(jnp.where(hit, s, 0.0), axis=0, keepdims=True))
        vals = jnp.where(hit, -jnp.inf, vals)
        chosen = chosen + jnp.where(hit, 1.0, 0.0)
        picks.append(ik)
    wsum = wks[0]
    for k in range(1, TOP_K):
        wsum = wsum + wks[k]

    incl = jnp.dot(chosen.astype(BF16), tri_ref[...], preferred_element_type=F32)
    rnk = base_ref[...] + (incl - chosen)
    for k in range(TOP_K):
        idx_ref[k:k + 1, :] = picks[k].astype(jnp.int32)
        w_ref[k:k + 1, :] = wks[k] / wsum * ROUTED_SCALE
        rk = jnp.sum(jnp.where(rowi == picks[k], rnk, 0.0), axis=0, keepdims=True)
        rank_ref[k:k + 1, :] = rk.astype(jnp.int32)
    base = base_ref[...] + jnp.sum(chosen, axis=1, keepdims=True)
    base_ref[...] = base
    cnt_ref[...] = jnp.broadcast_to(base, cnt_ref.shape)


def _router(x1, wr_hi, wr_lo, bias, *, tt):
    nt, d = x1.shape
    n_exp = wr_hi.shape[0]
    assert nt % tt == 0
    tri = jnp.triu(jnp.ones((tt, tt), BF16))
    const = lambda i: (0, 0)
    col = lambda i: (0, i)
    return pl.pallas_call(
        functools.partial(_router_kernel, tt=tt, n_exp=n_exp),
        grid=(nt // tt,),
        in_specs=[pl.BlockSpec((tt, d), lambda i: (i, 0)),
                  pl.BlockSpec(wr_hi.shape, const), pl.BlockSpec(wr_lo.shape, const),
                  pl.BlockSpec(bias.shape, const), pl.BlockSpec(tri.shape, const)],
        out_specs=[pl.BlockSpec((TOP_K, tt), col), pl.BlockSpec((TOP_K, tt), col),
                   pl.BlockSpec((TOP_K, tt), col), pl.BlockSpec((n_exp, LANES), const)],
        out_shape=[jax.ShapeDtypeStruct((TOP_K, nt), jnp.int32), jax.ShapeDtypeStruct((TOP_K, nt), F32),
                   jax.ShapeDtypeStruct((TOP_K, nt), jnp.int32), jax.ShapeDtypeStruct((n_exp, LANES), F32)],
        scratch_shapes=[pltpu.VMEM((n_exp, 1), F32)],
        compiler_params=_cparams(("arbitrary",)),
    )(x1, wr_hi, wr_lo, bias, tri)


def _dispatch_kernel(ps_ref, bc_ref, idx_ref, rank_ref, x_ref, o_ref, zbuf, sem, zsem, *, td):
    i = pl.program_id(0)
    n = pl.num_programs(0)
    slot = i % 2

    @pl.when(i == 0)
    def _():
        zbuf[...] = jnp.zeros(zbuf.shape, zbuf.dtype)
        n_blocks = o_ref.shape[0] // SLOT_ROWS

        def fill(b, carry):
            @pl.when(bc_ref[b] < SLOT_ROWS)
            def _():
                pltpu.make_async_copy(zbuf, o_ref.at[pl.ds(b * SLOT_ROWS, SLOT_ROWS)], zsem).start()
            return carry

        def fill_done(b, carry):
            @pl.when(bc_ref[b] < SLOT_ROWS)
            def _():
                pltpu.make_async_copy(zbuf, o_ref.at[pl.ds(0, SLOT_ROWS)], zsem).wait()
            return carry

        lax.fori_loop(0, n_blocks, fill, 0)
        lax.fori_loop(0, n_blocks, fill_done, 0)

    def per_token(t, carry):
        tok = i * td + t
        for k in range(TOP_K):
            dst = ps_ref[idx_ref[k, t]] + rank_ref[k, t]
            pltpu.make_async_copy(x_ref.at[tok], o_ref.at[dst], sem.at[slot]).start()
        return carry

    lax.fori_loop(0, td, per_token, 0)

    def drain(s):
        pltpu.make_async_copy(x_ref.at[pl.ds(0, TOP_K * td)], o_ref.at[pl.ds(0, TOP_K * td)], sem.at[s]).wait()

    @pl.when(i > 0)
    def _():
        drain(1 - slot)

    @pl.when(i == n - 1)
    def _():
        drain(slot)


def _dispatch(pad_start, block_cnt, idx_t, rank_t, x1p, *, td):
    nt, dp = x1p.shape
    n_slots = block_cnt.shape[0] * SLOT_ROWS
    assert nt % td == 0 and n_slots >= TOP_K * td and nt >= TOP_K * td
    smem_blk = pl.BlockSpec((TOP_K, td), lambda i, ps, bc: (0, i), memory_space=pltpu.SMEM)
    return pl.pallas_call(
        functools.partial(_dispatch_kernel, td=td),
        grid_spec=pltpu.PrefetchScalarGridSpec(
            num_scalar_prefetch=2,
            grid=(nt // td,),
            in_specs=[smem_blk, smem_blk, pl.BlockSpec(memory_space=pl.ANY)],
            out_specs=pl.BlockSpec(memory_space=pl.ANY),
            scratch_shapes=[pltpu.VMEM((SLOT_ROWS, dp), x1p.dtype), pltpu.SemaphoreType.DMA((2,)),
                            pltpu.SemaphoreType.DMA(())]),
        out_shape=jax.ShapeDtypeStruct((n_slots, dp), x1p.dtype),
        compiler_params=_cparams(("arbitrary",)),
    )(pad_start, block_cnt, idx_t, rank_t, x1p)


def _experts_kernel(be_ref, bc_ref, xs_ref, wg_ref, wu_ref, wd_ref, ys_ref, wg_s, wu_s, wd_s):
    i = pl.program_id(0)
    changed = (i == 0) | (be_ref[i] != be_ref[jnp.maximum(i - 1, 0)])

    @pl.when(changed)
    def _():
        wg_s[...] = wg_ref[...].astype(BF16)
        wu_s[...] = wu_ref[...].astype(BF16)
        wd_s[...] = wd_ref[...].astype(BF16)

    cnt = bc_ref[i]

    @pl.when(cnt > 0)
    def _():
        lo, hi = _unpack_halves(xs_ref[...])
        x = jnp.concatenate([lo, hi], axis=1).astype(BF16)
        g = jnp.dot(x, wg_s[...], preferred_element_type=F32)
        u = jnp.dot(x, wu_s[...], preferred_element_type=F32)
        hmid = (_silu(g) * u).astype(BF16)
        ys_ref[...] = _pack_halves(jnp.dot(hmid, wd_s[...], preferred_element_type=F32))

    @pl.when(cnt == 0)
    def _():
        ys_ref[...] = jnp.zeros(ys_ref.shape, ys_ref.dtype)


def _experts(block_e, block_cnt, xs, w_gate, w_up, w_down, layer):
    n_slots, dp = xs.shape
    n_blocks = n_slots // SLOT_ROWS
    d, de = w_gate.shape[-2:]
    assert d == 2 * dp
    return pl.pallas_call(
        _experts_kernel,
        grid_spec=pltpu.PrefetchScalarGridSpec(
            num_scalar_prefetch=2,
            grid=(n_blocks,),
            in_specs=[pl.BlockSpec((SLOT_ROWS, dp), lambda i, be, bc: (i, 0)),
                      pl.BlockSpec((None, None, d, de), lambda i, be, bc: (layer, be[i], 0, 0)),
                      pl.BlockSpec((None, None, d, de), lambda i, be, bc: (layer, be[i], 0, 0)),
                      pl.BlockSpec((None, None, de, d), lambda i, be, bc: (layer, be[i], 0, 0))],
            out_specs=pl.BlockSpec((SLOT_ROWS, dp), lambda i, be, bc: (i, 0)),
            scratch_shapes=[pltpu.VMEM((d, de), BF16), pltpu.VMEM((d, de), BF16), pltpu.VMEM((de, d), BF16)]),
        out_shape=jax.ShapeDtypeStruct((n_slots, dp), U32),
        compiler_params=_cparams(("arbitrary",)),
    )(block_e, block_cnt, xs, w_gate, w_up, w_down)


def _combine_kernel(ps_ref, idx0_ref, rank0_ref, idx1_ref, rank1_ref, w_ref, x1_ref, wsg_ref, wsu_ref,
                    wsd_ref, g_ref, b_ref, ys_ref, y_ref, buf, sem, *, tc, alpha):
    i = pl.program_id(0)
    n = pl.num_programs(0)
    slot = i % 2

    def gather(idx_ref, rank_ref, s):
        def per_token(t, carry):
            for k in range(TOP_K):
                src = ps_ref[idx_ref[k, t]] + rank_ref[k, t]
                pltpu.make_async_copy(ys_ref.at[src], buf.at[s, k * tc + t], sem.at[s]).start()
            return carry
        lax.fori_loop(0, tc, per_token, 0)

    @pl.when(i == 0)
    def _():
        gather(idx0_ref, rank0_ref, 0)

    @pl.when(i + 1 < n)
    def _():
        gather(idx1_ref, rank1_ref, 1 - slot)

    x1 = x1_ref[...]
    xb = x1.astype(BF16)
    hs = (_silu(jnp.dot(xb, wsg_ref[...], preferred_element_type=F32))
          * jnp.dot(xb, wsu_ref[...], preferred_element_type=F32)).astype(BF16)
    shared = jnp.dot(hs, wsd_ref[...], preferred_element_type=F32)

    pltpu.make_async_copy(ys_ref.at[pl.ds(0, TOP_K * tc)], buf.at[slot], sem.at[slot]).wait()
    w = w_ref[...]
    half = x1.shape[1] // 2
    lo_acc = jnp.zeros((tc, half), F32)
    hi_acc = jnp.zeros((tc, half), F32)
    for k in range(TOP_K):
        lo, hi = _unpack_halves(buf[slot, k * tc:(k + 1) * tc, :])
        lo_acc = lo_acc + w[:, k:k + 1] * lo
        hi_acc = hi_acc + w[:, k:k + 1] * hi
    moe = jnp.concatenate([lo_acc, hi_acc], axis=1) + shared
    y_ref[...] = _layer_norm(alpha * x1 + moe, g_ref[...], b_ref[...])


def _combine(pad_start, idx_t, rank_t, w_tok, x1, wsg, wsu, wsd, g, b, ys, *, tc, alpha):
    nt, d = x1.shape
    n_tiles = nt // tc
    assert nt % tc == 0 and ys.shape[0] >= TOP_K * tc
    cur = pl.BlockSpec((TOP_K, tc), lambda i, ps: (0, i), memory_space=pltpu.SMEM)
    nxt = pl.BlockSpec((TOP_K, tc), lambda i, ps: (0, jnp.minimum(i + 1, n_tiles - 1)), memory_space=pltpu.SMEM)
    const = lambda i, ps: (0, 0)
    tile = lambda i, ps: (i, 0)
    return pl.pallas_call(
        functools.partial(_combine_kernel, tc=tc, alpha=alpha),
        grid_spec=pltpu.PrefetchScalarGridSpec(
            num_scalar_prefetch=1,
            grid=(n_tiles,),
            in_specs=[cur, cur, nxt, nxt,
                      pl.BlockSpec((tc, TOP_K), tile),
                      pl.BlockSpec((tc, d), tile),
                      pl.BlockSpec(wsg.shape, const), pl.BlockSpec(wsu.shape, const), pl.BlockSpec(wsd.shape, const),
                      pl.BlockSpec(g.shape, const), pl.BlockSpec(b.shape, const),
                      pl.BlockSpec(memory_space=pl.ANY)],
            out_specs=pl.BlockSpec((tc, d), tile),
            scratch_shapes=[pltpu.VMEM((2, TOP_K * tc, ys.shape[1]), ys.dtype), pltpu.SemaphoreType.DMA((2,))]),
        out_shape=jax.ShapeDtypeStruct((nt, d), F32),
        compiler_params=_cparams(("arbitrary",)),
    )(pad_start, idx_t, rank_t, idx_t, rank_t, w_tok, x1, wsg, wsu, wsd, g, b, ys)


def _rope_cs(pos, rope):
    half = rope // 2
    inv = ROPE_BASE ** (-jnp.arange(half, dtype=F32) / half)
    ang = pos.astype(F32)[:, None] * inv[None, :]
    cos, sin = jnp.cos(ang), jnp.sin(ang)
    pad = jnp.zeros((pos.shape[0], LANES - rope), F32)
    return (jnp.concatenate([cos, cos, pad], axis=1), jnp.concatenate([-sin, sin, pad], axis=1))


def _pick_tile(n, pref):
    t = pref
    while n % t:
        t //= 2
    return t


def kernel(x_prompt, x_sample, cache_kv_latent, cache_k_rope, state_conv, page_table, w_in, conv_w, q_norm, w_uq, kv_norm, w_uk, w_uv, g_conv, g_attn, w_o, ln1_g, ln1_b, w_router, router_bias, w_gate, w_up, w_down, ws_gate, ws_up, ws_down, ln2_g, ln2_b):
    depth = w_in.shape[0]
    bsz, seq, d = x_prompt.shape
    db, t_new, _ = x_sample.shape
    kv_lora, n_heads, nope = w_uk.shape[1:]
    v_dim = w_uv.shape[-1]
    rope = cache_k_rope.shape[-1]
    q_lora = q_norm.shape[-1]
    conv_ch = conv_w.shape[-1]
    n_exp = w_router.shape[-1]
    page = cache_kv_latent.shape[2]
    past_len = page_table.shape[1] * page
    alpha = (2.0 * depth) ** 0.25
    scale = float((nope + rope) ** -0.5)
    in_cols = w_in.shape[-1]
    in_pad = -(-(in_cols - rope + LANES) // LANES) * LANES
    n_p, n_s = bsz * seq, db * t_new
    nt = n_p + n_s

    tm_p = _pick_tile(seq, 256)
    tm_s = _pick_tile(n_s, 256)
    tq = _pick_tile(seq, 256)
    tm_o = _pick_tile(math.gcd(n_p, n_s), 256)
    tt = _pick_tile(math.gcd(n_p, n_s), 512)
    td = _pick_tile(nt, 256)
    while TOP_K * td > nt:
        td //= 2
    tc = _pick_tile(nt, 128)
    pps = _pick_tile(page_table.shape[1], 16)

    cos_p, sin_p = _rope_cs(jnp.arange(seq, dtype=jnp.int32), rope)
    cos_s, sin_s = _rope_cs(past_len + jnp.arange(t_new, dtype=jnp.int32), rope)
    cos_s, sin_s = jnp.tile(cos_s, (tm_s // t_new, 1)), jnp.tile(sin_s, (tm_s // t_new, 1))

    xp, xs_ = x_prompt.reshape(n_p, d), x_sample.reshape(n_s, d)
    outs = [[] for _ in range(6)]
    for l in range(depth):
        win = jnp.pad(w_in[l], ((0, 0), (0, in_pad - in_cols))).astype(BF16)
        wuq3 = w_uq[l].reshape(q_lora, n_heads, nope + rope)
        qlat = _fold_qlat(jnp.transpose(wuq3[:, :, :nope], (1, 0, 2)), jnp.transpose(w_uk[l], (1, 2, 0)), scale)
        wq_rope = jnp.transpose(wuq3[:, :, nope:], (1, 0, 2)) * scale
        wq = jnp.concatenate([qlat, wq_rope, jnp.zeros((n_heads, q_lora, LANES - rope), F32)], axis=2)
        wq = jnp.transpose(wq, (1, 0, 2)).reshape(q_lora, n_heads * 2 * LANES).astype(BF16)
        eye = jnp.eye(n_heads, dtype=F32)
        wuv_bd = jnp.einsum('chv,hg->hcgv', w_uv[l], eye).reshape(n_heads * kv_lora, n_heads * v_dim).astype(BF16)
        wo = w_o[l].astype(BF16)
        wr_t = w_router[l].T
        wr_hi = wr_t.astype(BF16)
        wr_lo = (wr_t - wr_hi.astype(F32)).astype(BF16)
        row = lambda v: v[l].reshape(1, -1)

        common = (win, wq, row(q_norm), row(kv_norm), conv_w[l], row(g_conv))
        q_p, kc_p, ckv_p, kpe_p, u_p, yc_p = _mixer_in(
            xp, *common, cos_p, sin_p, None, prompt=True, seq=seq, tm=tm_p, n_heads=n_heads, rope=rope, q_dtype=BF16)
        st = state_conv[l].astype(F32)
        zero = jnp.zeros((db, conv_ch), F32)
        prev1 = jnp.stack([st[:, 1]] + [zero] * (t_new - 1), axis=1).reshape(n_s, conv_ch)
        prev2 = jnp.stack([st[:, 0], st[:, 1]] + [zero] * (t_new - 2), axis=1).reshape(n_s, conv_ch)
        q_s, kc_s, ckv_s, kpe_s, u_s, yc_s = _mixer_in(
            xs_, *common, cos_s, sin_s, (prev1, prev2), prompt=False, seq=t_new, tm=tm_s, n_heads=n_heads, rope=rope,
            q_dtype=F32)
        del kc_s

        at_p = _prompt_attention(q_p, kc_p, batch=bsz, seq=seq, tq=tq, n_heads=n_heads)
        at_s = _sample_attention(q_s, ckv_s, kpe_s, cache_kv_latent, cache_k_rope, page_table, l,
                                 n_heads=n_heads, t_new=t_new, pps=pps)

        x1, x1p = _mixer_out(xp, xs_, yc_p, yc_s, at_p, at_s, wuv_bd, row(g_attn), wo, row(ln1_g), row(ln1_b),
                             tm=tm_o, alpha=alpha)

        idx_t, w_t, rank_t, cnt = _router(x1, wr_hi, wr_lo, router_bias[l].reshape(n_exp, 1), tt=tt)
        counts = cnt[:, 0].astype(jnp.int32)
        padded = (counts + SLOT_ROWS - 1) // SLOT_ROWS * SLOT_ROWS
        pad_end = jnp.cumsum(padded)
        pad_start = (pad_end - padded).astype(jnp.int32)
        n_blocks = -(-(nt * TOP_K) // SLOT_ROWS) + n_exp
        block_pos = jnp.arange(n_blocks, dtype=jnp.int32) * SLOT_ROWS
        block_e = jnp.minimum(jnp.searchsorted(pad_end, block_pos, side='right'), n_exp - 1).astype(jnp.int32)
        block_cnt = jnp.clip(counts[block_e] - (block_pos - pad_start[block_e]), 0, SLOT_ROWS).astype(jnp.int32)

        xs_sorted = _dispatch(pad_start, block_cnt, idx_t, rank_t, x1p, td=td)
        ys = _experts(block_e, block_cnt, xs_sorted, w_gate, w_up, w_down, l)
        y = _combine(pad_start, idx_t, rank_t, w_t.T, x1, ws_gate[l].astype(BF16), ws_up[l].astype(BF16),
                     ws_down[l].astype(BF16), row(ln2_g), row(ln2_b), ys, tc=tc, alpha=alpha)

        xp, xs_ = y[:n_p], y[n_p:]
        outs[0].append(ckv_p.reshape(bsz, seq, kv_lora))
        outs[1].append(kpe_p.reshape(bsz, seq, rope))
        outs[2].append(u_p.reshape(bsz, seq, conv_ch)[:, seq - (CONV_W - 1):])
        outs[3].append(ckv_s.reshape(db, t_new, kv_lora))
        outs[4].append(kpe_s.reshape(db, t_new, rope))
        outs[5].append(u_s.reshape(db, t_new, conv_ch)[:, t_new - (CONV_W - 1):])
    return (xp.reshape(bsz, seq, d), xs_.reshape(db, t_new, d)) + tuple(jnp.stack(o) for o in outs)
```

```python
import functools
import math

import jax
import jax.numpy as jnp
from jax import lax
from jax.experimental import pallas as pl
from jax.experimental.pallas import tpu as pltpu

F32 = jnp.float32
BF16 = jnp.bfloat16
U32 = jnp.uint32

ROPE_BASE = 10000.0
NORM_EPS = 1e-6
LN_EPS = 1e-5
NEG_INF = -1e30
TOP_K = 8
N_GROUPS = 8
TOPK_GROUPS = 4
ROUTED_SCALE = 2.5
CONV_W = 3

LANES = 128
SLOT_ROWS = 256
VMEM_LIMIT = 52 * 1024 * 1024

_NT = (((1,), (1,)), ((), ()))


def _cparams(sem):
    return pltpu.CompilerParams(dimension_semantics=sem, vmem_limit_bytes=VMEM_LIMIT)


def _rms(x, g):
    return x * lax.rsqrt(jnp.mean(x * x, axis=-1, keepdims=True) + NORM_EPS) * g


def _layer_norm(x, g, b):
    mu = jnp.mean(x, axis=-1, keepdims=True)
    xc = x - mu
    var = jnp.mean(xc * xc, axis=-1, keepdims=True)
    return xc * lax.rsqrt(var + LN_EPS) * g + b


def _silu(x):
    return x / (1.0 + jnp.exp(-x))


def _rope_tile(p, c, s, half):
    lane = lax.broadcasted_iota(jnp.int32, p.shape, 1)
    swapped = jnp.where(lane < half, pltpu.roll(p, LANES - half, axis=1), pltpu.roll(p, half, axis=1))
    return p * c + swapped * s


def _pack_rows(x, ref):
    r = x.shape[0]
    half = x.shape[1] // 2
    sub = half // LANES
    bits = lax.bitcast_convert_type(x.astype(BF16).astype(F32), U32)
    words = bits[:, half:] | (bits[:, :half] >> 16)
    for j in range(sub):
        ref[pl.ds(j, r, stride=sub), :] = words[:, j * LANES:(j + 1) * LANES]


def _unpack_rows(ref, first, r, sub):
    lo, hi = [], []
    for j in range(sub):
        w = ref[pl.ds(first * sub + j, r, stride=sub), :]
        lo.append(lax.bitcast_convert_type(w << 16, F32))
        hi.append(lax.bitcast_convert_type(w & jnp.uint32(0xFFFF0000), F32))
    return lo, hi


def _fold_kernel(a_ref, b_ref, o_ref, *, scale):
    o_ref[...] = jnp.dot(a_ref[...], b_ref[...], precision=lax.Precision.HIGHEST,
                         preferred_element_type=F32) * scale


def _fold_qlat(wuq_nope, wuk_t, scale):
    h, r, n = wuq_nope.shape
    c = wuk_t.shape[-1]
    return pl.pallas_call(
        functools.partial(_fold_kernel, scale=scale),
        grid=(h,),
        in_specs=[pl.BlockSpec((None, r, n), lambda i: (i, 0, 0)),
                  pl.BlockSpec((None, n, c), lambda i: (i, 0, 0))],
        out_specs=pl.BlockSpec((None, r, c), lambda i: (i, 0, 0)),
        out_shape=jax.ShapeDtypeStruct((h, r, c), F32),
        compiler_params=_cparams(("parallel",)),
    )(wuq_nope, wuk_t)


def _mixer_in_kernel(*refs, prompt, tiles_per_seq, tm, t_new, n_heads, conv_ch, q_lora, kv_lora, rope):
    if prompt:
        (x_ref, win_ref, wq_ref, qn_ref, kvn_ref, cw_ref, gc_ref, cos_ref, sin_ref,
         q_ref, kc_ref, ckv_ref, kpe_ref, u_ref, yc_ref, carry_ref) = refs
    else:
        (x_ref, win_ref, wq_ref, qn_ref, kvn_ref, cw_ref, gc_ref, cos_ref, sin_ref, p1_ref, p2_ref,
         q_ref, kc_ref, ckv_ref, kpe_ref, u_ref, yc_ref) = refs
    half = rope // 2
    c1, c2, c3 = conv_ch, 2 * conv_ch, 3 * conv_ch
    c4 = c3 + q_lora
    c5 = c4 + kv_lora

    h = jnp.dot(x_ref[...].astype(BF16), win_ref[...], preferred_element_type=F32)
    b_gate, c_gate, x_conv = h[:, :c1], h[:, c1:c2], h[:, c2:c3]
    q_a, c_kv, kp = h[:, c3:c4], h[:, c4:c5], h[:, c5:c5 + LANES]

    u = c_gate * x_conv
    u_ref[...] = u
    row = lax.broadcasted_iota(jnp.int32, (tm, 1), 0)
    if prompt:
        first = (pl.program_id(0) % tiles_per_seq) == 0
        prev1 = jnp.where(first, 0.0, carry_ref[7:8, :])
        prev2 = jnp.where(first, 0.0, carry_ref[6:7, :])
        t_in = row
        p1 = prev1
        p2 = jnp.where(row == 0, prev2, prev1)
        carry_ref[...] = u[tm - 8:, :]
    else:
        t_in = row & (t_new - 1)
        p1 = p1_ref[...]
        p2 = p2_ref[...]
    um1 = jnp.where(t_in == 0, p1, pltpu.roll(u, 1, axis=0))
    um2 = jnp.where(t_in < 2, p2, pltpu.roll(u, 2, axis=0))
    conv_y = cw_ref[0:1, :] * um2 + cw_ref[1:2, :] * um1 + cw_ref[2:3, :] * u
    yc_ref[...] = _rms(b_gate * conv_y, gc_ref[...]).astype(yc_ref.dtype)

    cos = cos_ref[...]
    sin = sin_ref[...]
    qn = _rms(q_a, qn_ref[...]).astype(BF16)
    q = jnp.dot(qn, wq_ref[...], preferred_element_type=F32)
    for hd in range(n_heads):
        o = hd * 2 * LANES
        q_ref[:, o:o + LANES] = q[:, o:o + LANES].astype(q_ref.dtype)
        q_ref[:, o + LANES:o + 2 * LANES] = _rope_tile(q[:, o + LANES:o + 2 * LANES], cos, sin, half).astype(q_ref.dtype)

    ckv = _rms(c_kv, kvn_ref[...])
    ckv_ref[...] = ckv
    kpr = _rope_tile(kp, cos, sin, half)
    kpe_ref[...] = kpr[:, :rope]
    kc_ref[:, :LANES] = ckv.astype(BF16)
    kc_ref[:, LANES:] = kpr.astype(BF16)


def _mixer_in(x2, win, wq, qn, kvn, cw, gc, cos, sin, prev, *, prompt, seq, tm, n_heads, rope, q_dtype):
    n, d = x2.shape
    conv_ch = cw.shape[-1]
    q_lora = qn.shape[-1]
    kv_lora = kvn.shape[-1]
    assert n % tm == 0 and kv_lora == LANES and tm % 8 == 0
    tiles_per_seq = seq // tm if prompt else 1
    if prompt:
        assert seq % tm == 0
    else:
        assert seq & (seq - 1) == 0 and tm % seq == 0
    const = lambda i: (0, 0)
    tile = lambda i: (i, 0)
    in_specs = [
        pl.BlockSpec((tm, d), tile),
        pl.BlockSpec(win.shape, const),
        pl.BlockSpec(wq.shape, const),
        pl.BlockSpec(qn.shape, const),
        pl.BlockSpec(kvn.shape, const),
        pl.BlockSpec(cw.shape, const),
        pl.BlockSpec(gc.shape, const),
    ]
    args = [x2, win, wq, qn, kvn, cw, gc, cos, sin]
    if prompt:
        in_specs += [pl.BlockSpec((tm, LANES), lambda i: (i % tiles_per_seq, 0))] * 2
        scratch = [pltpu.VMEM((8, conv_ch), F32)]
    else:
        in_specs += [pl.BlockSpec((tm, LANES), const)] * 2
        in_specs += [pl.BlockSpec((tm, conv_ch), tile)] * 2
        args += list(prev)
        scratch = []
    qw = wq.shape[-1]
    out_shape = [
        jax.ShapeDtypeStruct((n, qw), q_dtype),
        jax.ShapeDtypeStruct((n, 2 * LANES), BF16),
        jax.ShapeDtypeStruct((n, kv_lora), F32),
        jax.ShapeDtypeStruct((n, rope), F32),
        jax.ShapeDtypeStruct((n, conv_ch), F32),
        jax.ShapeDtypeStruct((n, conv_ch), BF16),
    ]
    out_specs = [
        pl.BlockSpec((tm, qw), tile),
        pl.BlockSpec((tm, 2 * LANES), tile),
        pl.BlockSpec((tm, kv_lora), tile),
        pl.BlockSpec((tm, rope), tile),
        pl.BlockSpec((tm, conv_ch), tile),
        pl.BlockSpec((tm, conv_ch), tile),
    ]
    return pl.pallas_call(
        functools.partial(_mixer_in_kernel, prompt=prompt, tiles_per_seq=tiles_per_seq, tm=tm, t_new=seq,
                          n_heads=n_heads, conv_ch=conv_ch, q_lora=q_lora, kv_lora=kv_lora, rope=rope),
        grid=(n // tm,),
        in_specs=in_specs,
        out_specs=out_specs,
        out_shape=out_shape,
        scratch_shapes=scratch,
        compiler_params=_cparams(("arbitrary",)),
    )(*args)


def _softmax_update(s, v, m_ref, l_ref, acc_ref):
    m_prev = m_ref[...]
    m_new = jnp.maximum(m_prev, jnp.max(s, axis=1, keepdims=True))
    alpha = jnp.exp2(m_prev - m_new)
    p = jnp.exp2(s - m_new)
    l_ref[...] = alpha * l_ref[...] + jnp.sum(p, axis=1, keepdims=True)
    acc_ref[...] = alpha * acc_ref[...] + jnp.dot(p.astype(BF16), v, preferred_element_type=F32)
    m_ref[...] = m_new


def _prompt_attn_kernel(q_ref, k_ref, o_ref, qs_ref, m_ref, l_ref, acc_ref, *, tq, n_heads, chunk_heads):
    i = pl.program_id(1)
    cr = chunk_heads * tq
    n_chunks = n_heads // chunk_heads
    for hd in range(n_heads):
        qs_ref[hd * tq:(hd + 1) * tq, :] = q_ref[:, hd * 2 * LANES:(hd + 1) * 2 * LANES]
    m_ref[...] = jnp.full(m_ref.shape, NEG_INF, F32)
    l_ref[...] = jnp.zeros(l_ref.shape, F32)
    acc_ref[...] = jnp.zeros(acc_ref.shape, F32)

    def step(j, masked):
        k = k_ref[pl.ds(pl.multiple_of(j * tq, tq), tq), :]
        v = k[:, :LANES]
        for c in range(n_chunks):
            rows = slice(c * cr, (c + 1) * cr)
            s = lax.dot_general(qs_ref[rows, :], k, _NT, preferred_element_type=F32)
            if masked:
                t = lax.broadcasted_iota(jnp.int32, s.shape, 0) & (tq - 1)
                col = lax.broadcasted_iota(jnp.int32, s.shape, 1)
                s = jnp.where(col <= t, s, NEG_INF)
            parts = [s[:, w * LANES:(w + 1) * LANES] for w in range(tq // LANES)]
            mc = parts[0]
            for part in parts[1:]:
                mc = jnp.maximum(mc, part)
            m_prev = m_ref[rows, :]
            m_new = jnp.maximum(m_prev, jnp.max(mc, axis=1, keepdims=True))
            alpha = jnp.exp2(m_prev - m_new)
            ps = [jnp.exp2(part - m_new) for part in parts]
            psum = ps[0]
            for pp in ps[1:]:
                psum = psum + pp
            l_ref[rows, :] = alpha * l_ref[rows, :] + psum
            p = jnp.concatenate(ps, axis=1).astype(BF16)
            acc_ref[rows, :] = alpha * acc_ref[rows, :] + jnp.dot(p, v, preferred_element_type=F32)
            m_ref[rows, :] = m_new

    def body(j, carry):
        step(j, False)
        return carry

    lax.fori_loop(0, i, body, 0)
    step(i, True)
    out = acc_ref[...] / jnp.sum(l_ref[...], axis=1, keepdims=True)
    for hd in range(n_heads):
        o_ref[:, hd * LANES:(hd + 1) * LANES] = out[hd * tq:(hd + 1) * tq, :].astype(o_ref.dtype)


def _prompt_attention(q, kc, *, batch, seq, tq, n_heads):
    n = q.shape[0]
    nq = seq // tq
    rows = n_heads * tq
    chunk_heads = 2 if n_heads % 2 == 0 else 1
    assert tq & (tq - 1) == 0 and seq % tq == 0 and tq % LANES == 0
    return pl.pallas_call(
        functools.partial(_prompt_attn_kernel, tq=tq, n_heads=n_heads, chunk_heads=chunk_heads),
        grid=(batch, nq),
        in_specs=[pl.BlockSpec((tq, q.shape[1]), lambda b, i: (b * nq + i, 0)),
                  pl.BlockSpec((seq, kc.shape[1]), lambda b, i: (b, 0))],
        out_specs=pl.BlockSpec((tq, n_heads * LANES), lambda b, i: (b * nq + i, 0)),
        out_shape=jax.ShapeDtypeStruct((n, n_heads * LANES), BF16),
        scratch_shapes=[pltpu.VMEM((rows, 2 * LANES), BF16),
                        pltpu.VMEM((rows, LANES), F32),
                        pltpu.VMEM((rows, LANES), F32),
                        pltpu.VMEM((rows, LANES), F32)],
        compiler_params=_cparams(("parallel", "arbitrary")),
    )(q, kc)


def _sample_attn_kernel(pt_ref, q_ref, ckvn_ref, kpen_ref, *rest, pps, n_heads, t_new, rope, page):
    del pt_ref
    kv_refs = rest[:pps]
    rp_refs = rest[pps:2 * pps]
    o_ref, qs_ref, m_ref, l_ref, acc_ref = rest[2 * pps:]
    j = pl.program_id(1)

    @pl.when(j == 0)
    def _():
        for hd in range(n_heads):
            qs_ref[hd * t_new:(hd + 1) * t_new, :] = q_ref[:, hd * 2 * LANES:(hd + 1) * 2 * LANES]
        m_ref[...] = jnp.full(m_ref.shape, NEG_INF, F32)
        l_ref[...] = jnp.zeros(l_ref.shape, F32)
        acc_ref[...] = jnp.zeros(acc_ref.shape, F32)

    qs = qs_ref[...]
    ql = qs[:, :LANES].astype(BF16)
    qp = qs[:, LANES:LANES + rope].astype(BF16)
    kv = jnp.concatenate([r[...] for r in kv_refs], axis=0).astype(BF16)
    rp = jnp.concatenate([r[...] for r in rp_refs], axis=0).astype(BF16)
    s = (lax.dot_general(ql, kv, _NT, preferred_element_type=F32)
         + lax.dot_general(qp, rp, _NT, preferred_element_type=F32))
    _softmax_update(s, kv, m_ref, l_ref, acc_ref)

    @pl.when(j == pl.num_programs(1) - 1)
    def _():
        kn = jnp.concatenate([ckvn_ref[...], jnp.zeros((page - t_new, LANES), F32)], axis=0).astype(BF16)
        rn = jnp.concatenate([kpen_ref[...], jnp.zeros((page - t_new, rope), F32)], axis=0).astype(BF16)
        s2 = (lax.dot_general(ql, kn, _NT, preferred_element_type=F32)
              + lax.dot_general(qp, rn, _NT, preferred_element_type=F32))
        t = lax.broadcasted_iota(jnp.int32, s2.shape, 0) & (t_new - 1)
        c = lax.broadcasted_iota(jnp.int32, s2.shape, 1)
        s2 = jnp.where(c <= t, s2, NEG_INF)
        _softmax_update(s2, kn, m_ref, l_ref, acc_ref)
        out = acc_ref[...] / l_ref[...]
        for hd in range(n_heads):
            o_ref[:, hd * LANES:(hd + 1) * LANES] = out[hd * t_new:(hd + 1) * t_new, :]


def _sample_attention(q, ckv_new, kpe_new, cache_kv, cache_rope, page_table, layer, *, n_heads, t_new, pps):
    n = q.shape[0]
    db = n // t_new
    n_pages = page_table.shape[1]
    page = cache_kv.shape[2]
    rope = cache_rope.shape[-1]
    assert n_pages % pps == 0 and t_new == 8 and cache_kv.shape[-1] == LANES
    steps = n_pages // pps
    rows = n_heads * t_new

    def page_spec(width, k):
        return pl.BlockSpec((None, None, page, width), lambda b, j, pt, k=k: (layer, pt[b, j * pps + k], 0, 0))

    in_specs = [pl.BlockSpec((t_new, q.shape[1]), lambda b, j, pt: (b, 0)),
                pl.BlockSpec((t_new, LANES), lambda b, j, pt: (b, 0)),
                pl.BlockSpec((t_new, rope), lambda b, j, pt: (b, 0))]
    in_specs += [page_spec(LANES, k) for k in range(pps)]
    in_specs += [page_spec(rope, k) for k in range(pps)]
    return pl.pallas_call(
        functools.partial(_sample_attn_kernel, pps=pps, n_heads=n_heads, t_new=t_new, rope=rope, page=page),
        grid_spec=pltpu.PrefetchScalarGridSpec(
            num_scalar_prefetch=1,
            grid=(db, steps),
            in_specs=in_specs,
            out_specs=pl.BlockSpec((t_new, n_heads * LANES), lambda b, j, pt: (b, 0)),
            scratch_shapes=[pltpu.VMEM((rows, 2 * LANES), F32),
                            pltpu.VMEM((rows, 1), F32),
                            pltpu.VMEM((rows, 1), F32),
                            pltpu.VMEM((rows, LANES), F32)]),
        out_shape=jax.ShapeDtypeStruct((n, n_heads * LANES), F32),
        compiler_params=_cparams(("parallel", "arbitrary")),
    )(page_table, q, ckv_new, kpe_new, *([cache_kv] * pps), *([cache_rope] * pps))


def _mixer_out_kernel(xp_ref, xs_ref, ycp_ref, ycs_ref, atp_ref, ats_ref, wuv_ref, ga_ref, wo_ref, g_ref, b_ref,
                      x1_ref, x1p_ref, *, n_prompt_tiles, conv_ch, alpha):
    is_p = pl.program_id(0) < n_prompt_tiles
    x = jnp.where(is_p, xp_ref[...], xs_ref[...])
    yc = jnp.where(is_p, ycp_ref[...], ycs_ref[...])
    at = jnp.where(is_p, atp_ref[...], ats_ref[...].astype(BF16))
    o = jnp.dot(at, wuv_ref[...], preferred_element_type=F32)
    ya = _rms(o, ga_ref[...]).astype(BF16)
    mix = (jnp.dot(yc, wo_ref[:conv_ch, :], preferred_element_type=F32)
           + jnp.dot(ya, wo_ref[conv_ch:, :], preferred_element_type=F32))
    x1 = _layer_norm(alpha * x + mix, g_ref[...], b_ref[...])
    x1_ref[...] = x1
    _pack_rows(x1, x1p_ref)


def _mixer_out(xp, xs, ycp, ycs, atp, ats, wuv_bd, ga, wo, g, b, *, tm, alpha):
    np_, d = xp.shape
    ns = xs.shape[0]
    assert np_ % tm == 0 and ns % tm == 0
    npt, nst = np_ // tm, ns // tm
    conv_ch = ycp.shape[1]
    p_map = lambda i: (jnp.minimum(i, npt - 1), 0)
    s_map = lambda i: (jnp.maximum(i - npt, 0), 0)
    const = lambda i: (0, 0)
    tile = lambda i: (i, 0)
    nt = np_ + ns
    sub = d // (2 * LANES)
    return pl.pallas_call(
        functools.partial(_mixer_out_kernel, n_prompt_tiles=npt, conv_ch=conv_ch, alpha=alpha),
        grid=(npt + nst,),
        in_specs=[pl.BlockSpec((tm, d), p_map), pl.BlockSpec((tm, d), s_map),
                  pl.BlockSpec((tm, conv_ch), p_map), pl.BlockSpec((tm, conv_ch), s_map),
                  pl.BlockSpec((tm, atp.shape[1]), p_map), pl.BlockSpec((tm, ats.shape[1]), s_map),
                  pl.BlockSpec(wuv_bd.shape, const), pl.BlockSpec(ga.shape, const),
                  pl.BlockSpec(wo.shape, const), pl.BlockSpec(g.shape, const), pl.BlockSpec(b.shape, const)],
        out_specs=[pl.BlockSpec((tm, d), tile), pl.BlockSpec((tm * sub, LANES), tile)],
        out_shape=[jax.ShapeDtypeStruct((nt, d), F32), jax.ShapeDtypeStruct((nt * sub, LANES), U32)],
        compiler_params=_cparams(("parallel",)),
    )(xp, xs, ycp, ycs, atp, ats, wuv_bd, ga, wo, g, b)


def _router_kernel(x_ref, wh_ref, wl_ref, b_ref, tri_ref, idx_ref, w_ref, rank_ref, cnt_ref, base_ref,
                   *, tt, n_exp):
    @pl.when(pl.program_id(0) == 0)
    def _():
        base_ref[...] = jnp.zeros(base_ref.shape, F32)

    x = x_ref[...]
    xh = x.astype(BF16)
    xl = (x - xh.astype(F32)).astype(BF16)
    wh = wh_ref[...]
    z = (lax.dot_general(wh, xh, _NT, preferred_element_type=F32)
         + lax.dot_general(wh, xl, _NT, preferred_element_type=F32)
         + lax.dot_general(wl_ref[...], xh, _NT, preferred_element_type=F32))
    s = 1.0 / (1.0 + jnp.exp(-z))
    sb = s + b_ref[...]

    gsz = n_exp // N_GROUPS
    git = lax.broadcasted_iota(jnp.int32, (gsz, tt), 0).astype(F32)
    blocks, gscore = [], []
    for g in range(N_GROUPS):
        blk = sb[g * gsz:(g + 1) * gsz, :]
        m1 = jnp.max(blk, axis=0, keepdims=True)
        f1 = jnp.min(jnp.where(blk == m1, git, float(gsz)), axis=0, keepdims=True)
        m2 = jnp.max(jnp.where(git == f1, -jnp.inf, blk), axis=0, keepdims=True)
        blocks.append(blk)
        gscore.append(m1 + m2)
    masked = []
    for g in range(N_GROUPS):
        ahead = jnp.zeros((1, tt), F32)
        for g2 in range(N_GROUPS):
            if g2 == g:
                continue
            beats = (gscore[g2] >= gscore[g]) if g2 < g else (gscore[g2] > gscore[g])
            ahead = ahead + jnp.where(beats, 1.0, 0.0)
        masked.append(jnp.where(ahead < float(TOPK_GROUPS), blocks[g], -jnp.inf))
    vals = jnp.concatenate(masked, axis=0)

    rowi = lax.broadcasted_iota(jnp.int32, (n_exp, tt), 0).astype(F32)
    chosen = jnp.zeros((n_exp, tt), F32)
    picks, wks = [], []
    for k in range(TOP_K):
        m = jnp.max(vals, axis=0, keepdims=True)
        ik = jnp.min(jnp.where(vals == m, rowi, float(n_exp)), axis=0, keepdims=True)
        hit = rowi == ik
        wks.append(jnp.sum(jnp.where(hit, s, 0.0), axis=0, keepdims=True))
        vals = jnp.where(hit, -jnp.inf, vals)
        chosen = chosen + jnp.where(hit, 1.0, 0.0)
        picks.append(ik)
    wsum = wks[0]
    for k in range(1, TOP_K):
        wsum = wsum + wks[k]

    incl = jnp.dot(chosen.astype(BF16), tri_ref[...], preferred_element_type=F32)
    rnk = base_ref[...] + (incl - chosen)
    for k in range(TOP_K):
        idx_ref[k:k + 1, :] = picks[k].astype(jnp.int32)
        w_ref[k:k + 1, :] = wks[k] / wsum * ROUTED_SCALE
        rk = jnp.sum(jnp.where(rowi == picks[k], rnk, 0.0), axis=0, keepdims=True)
        rank_ref[k:k + 1, :] = rk.astype(jnp.int32)
    base = base_ref[...] + jnp.sum(chosen, axis=1, keepdims=True)
    base_ref[...] = base
    cnt_ref[...] = jnp.broadcast_to(base, cnt_ref.shape)


def _router(x1, wr_hi, wr_lo, bias, *, tt):
    nt, d = x1.shape
    n_exp = wr_hi.shape[0]
    assert nt % tt == 0
    tri = jnp.triu(jnp.ones((tt, tt), BF16))
    const = lambda i: (0, 0)
    col = lambda i: (0, i)
    return pl.pallas_call(
        functools.partial(_router_kernel, tt=tt, n_exp=n_exp),
        grid=(nt // tt,),
        in_specs=[pl.BlockSpec((tt, d), lambda i: (i, 0)),
                  pl.BlockSpec(wr_hi.shape, const), pl.BlockSpec(wr_lo.shape, const),
                  pl.BlockSpec(bias.shape, const), pl.BlockSpec(tri.shape, const)],
        out_specs=[pl.BlockSpec((TOP_K, tt), col), pl.BlockSpec((TOP_K, tt), col),
                   pl.BlockSpec((TOP_K, tt), col), pl.BlockSpec((n_exp, LANES), const)],
        out_shape=[jax.ShapeDtypeStruct((TOP_K, nt), jnp.int32), jax.ShapeDtypeStruct((TOP_K, nt), F32),
                   jax.ShapeDtypeStruct((TOP_K, nt), jnp.int32), jax.ShapeDtypeStruct((n_exp, LANES), F32)],
        scratch_shapes=[pltpu.VMEM((n_exp, 1), F32)],
        compiler_params=_cparams(("arbitrary",)),
    )(x1, wr_hi, wr_lo, bias, tri)


def _dispatch_kernel(bc_ref, dest_ref, x_ref, o_ref, zbuf, sem, zsem, *, td, sub):
    i = pl.program_id(0)
    blk = SLOT_ROWS * sub

    @pl.when(i == 0)
    def _():
        zbuf[...] = jnp.zeros(zbuf.shape, zbuf.dtype)
        n_blocks = o_ref.shape[0] // blk

        def fill(b, carry):
            @pl.when(bc_ref[b] < SLOT_ROWS)
            def _():
                pltpu.make_async_copy(zbuf, o_ref.at[pl.ds(pl.multiple_of(b * blk, blk), blk)], zsem).start()
            return carry

        def fill_done(b, carry):
            @pl.when(bc_ref[b] < SLOT_ROWS)
            def _():
                pltpu.make_async_copy(zbuf, o_ref.at[pl.ds(0, blk)], zsem).wait()
            return carry

        lax.fori_loop(0, n_blocks, fill, 0)
        lax.fori_loop(0, n_blocks, fill_done, 0)

    def per_token(t, carry):
        src = x_ref.at[pl.ds(pl.multiple_of(t * sub, sub), sub)]
        for k in range(TOP_K):
            dst = pl.multiple_of(dest_ref[k, t] * sub, sub)
            pltpu.make_async_copy(src, o_ref.at[pl.ds(dst, sub)], sem).start(priority=k % 2)
        return carry

    lax.fori_loop(0, td, per_token, 0)
    for _ in range(TOP_K):
        pltpu.make_async_copy(x_ref, o_ref.at[pl.ds(0, td * sub)], sem).wait()


def _dispatch(block_cnt, dest_t, x1p, *, td, sub):
    nt = x1p.shape[0] // sub
    n_slots = block_cnt.shape[0] * SLOT_ROWS
    assert nt % td == 0 and n_slots >= td
    return pl.pallas_call(
        functools.partial(_dispatch_kernel, td=td, sub=sub),
        grid_spec=pltpu.PrefetchScalarGridSpec(
            num_scalar_prefetch=1,
            grid=(nt // td,),
            in_specs=[pl.BlockSpec((TOP_K, td), lambda i, bc: (0, i), memory_space=pltpu.SMEM),
                      pl.BlockSpec((td * sub, LANES), lambda i, bc: (i, 0))],
            out_specs=pl.BlockSpec(memory_space=pl.ANY),
            scratch_shapes=[pltpu.VMEM((SLOT_ROWS * sub, LANES), x1p.dtype), pltpu.SemaphoreType.DMA(()),
                            pltpu.SemaphoreType.DMA(())]),
        out_shape=jax.ShapeDtypeStruct((n_slots * sub, LANES), x1p.dtype),
        compiler_params=_cparams(("arbitrary",)),
    )(block_cnt, dest_t, x1p)


def _experts_kernel(be_ref, bc_ref, xs_ref, wg_ref, wu_ref, wd_ref, ys_ref, wg_s, wu_s, wd_s):
    i = pl.program_id(0)
    changed = (i == 0) | (be_ref[i] != be_ref[jnp.maximum(i - 1, 0)])

    @pl.when(changed)
    def _():
        wg_s[...] = wg_ref[...].astype(BF16)
        wu_s[...] = wu_ref[...].astype(BF16)
        wd_s[...] = wd_ref[...].astype(BF16)

    cnt = bc_ref[i]

    @pl.when(cnt > 0)
    def _():
        sub = xs_ref.shape[0] // SLOT_ROWS
        lo, hi = _unpack_rows(xs_ref, 0, SLOT_ROWS, sub)
        x = jnp.concatenate(lo + hi, axis=1).astype(BF16)
        g = jnp.dot(x, wg_s[...], preferred_element_type=F32)
        u = jnp.dot(x, wu_s[...], preferred_element_type=F32)
        hmid = (_silu(g) * u).astype(BF16)
        _pack_rows(jnp.dot(hmid, wd_s[...], preferred_element_type=F32), ys_ref)

    @pl.when(cnt == 0)
    def _():
        ys_ref[...] = jnp.zeros(ys_ref.shape, ys_ref.dtype)


def _experts(block_e, block_cnt, xs, w_gate, w_up, w_down, layer):
    n_blocks = block_e.shape[0]
    d, de = w_gate.shape[-2:]
    sub = d // (2 * LANES)
    assert xs.shape == (n_blocks * SLOT_ROWS * sub, LANES)
    slots = pl.BlockSpec((SLOT_ROWS * sub, LANES), lambda i, be, bc: (i, 0))
    return pl.pallas_call(
        _experts_kernel,
        grid_spec=pltpu.PrefetchScalarGridSpec(
            num_scalar_prefetch=2,
            grid=(n_blocks,),
            in_specs=[slots,
                      pl.BlockSpec((None, None, d, de), lambda i, be, bc: (layer, be[i], 0, 0)),
                      pl.BlockSpec((None, None, d, de), lambda i, be, bc: (layer, be[i], 0, 0)),
                      pl.BlockSpec((None, None, de, d), lambda i, be, bc: (layer, be[i], 0, 0))],
            out_specs=slots,
            scratch_shapes=[pltpu.VMEM((d, de), BF16), pltpu.VMEM((d, de), BF16), pltpu.VMEM((de, d), BF16)]),
        out_shape=jax.ShapeDtypeStruct(xs.shape, U32),
        compiler_params=_cparams(("arbitrary",)),
    )(block_e, block_cnt, xs, w_gate, w_up, w_down)


def _combine_kernel(dest0_ref, dest1_ref, w_ref, x1_ref, wsg_ref, wsu_ref,
                    wsd_ref, g_ref, b_ref, ys_ref, y_ref, buf, sem, *, tc, alpha, sub):
    i = pl.program_id(0)
    n = pl.num_programs(0)
    slot = i % 2

    def gather_row(dest_ref, s, k, t):
        src = pl.multiple_of(dest_ref[k, t] * sub, sub)
        dst = (k * tc + t) * sub
        if not isinstance(dst, int):
            dst = pl.multiple_of(dst, sub)
        pltpu.make_async_copy(ys_ref.at[pl.ds(src, sub)], buf.at[s, pl.ds(dst, sub)], sem.at[s]).start(priority=k % 2)

    @pl.when(i == 0)
    def _():
        def per_token(t, carry):
            for k in range(TOP_K):
                gather_row(dest0_ref, 0, k, t)
            return carry
        lax.fori_loop(0, tc, per_token, 0)

    for t in range(tc):
        for k in range(TOP_K):
            gather_row(dest1_ref, 1 - slot, k, t)

    x1 = x1_ref[...]
    xb = x1.astype(BF16)
    hs = (_silu(jnp.dot(xb, wsg_ref[...], preferred_element_type=F32))
          * jnp.dot(xb, wsu_ref[...], preferred_element_type=F32)).astype(BF16)
    shared = jnp.dot(hs, wsd_ref[...], preferred_element_type=F32)

    def drain(s):
        pltpu.make_async_copy(ys_ref.at[pl.ds(0, TOP_K * tc * sub)], buf.at[s], sem.at[s]).wait()

    drain(slot)
    w = w_ref[...]
    lo_acc = [jnp.zeros((tc, LANES), F32) for _ in range(sub)]
    hi_acc = [jnp.zeros((tc, LANES), F32) for _ in range(sub)]
    for k in range(TOP_K):
        lo, hi = _unpack_rows(buf.at[slot], k * tc, tc, sub)
        wk = w[:, k:k + 1]
        for j in range(sub):
            lo_acc[j] = lo_acc[j] + wk * lo[j]
            hi_acc[j] = hi_acc[j] + wk * hi[j]
    moe = jnp.concatenate(lo_acc + hi_acc, axis=1) + shared
    y_ref[...] = _layer_norm(alpha * x1 + moe, g_ref[...], b_ref[...])

    @pl.when(i == n - 1)
    def _():
        drain(1 - slot)


def _combine(dest_t, w_tok, x1, wsg, wsu, wsd, g, b, ys, *, tc, alpha):
    nt, d = x1.shape
    n_tiles = nt // tc
    sub = d // (2 * LANES)
    assert nt % tc == 0 and ys.shape[0] >= TOP_K * tc * sub
    cur = pl.BlockSpec((TOP_K, tc), lambda i: (0, i), memory_space=pltpu.SMEM)
    nxt = pl.BlockSpec((TOP_K, tc), lambda i: (0, jnp.minimum(i + 1, n_tiles - 1)), memory_space=pltpu.SMEM)
    const = lambda i: (0, 0)
    tile = lambda i: (i, 0)
    return pl.pallas_call(
        functools.partial(_combine_kernel, tc=tc, alpha=alpha, sub=sub),
        grid=(n_tiles,),
        in_specs=[cur, nxt,
                  pl.BlockSpec((tc, TOP_K), tile),
                  pl.BlockSpec((tc, d), tile),
                  pl.BlockSpec(wsg.shape, const), pl.BlockSpec(wsu.shape, const), pl.BlockSpec(wsd.shape, const),
                  pl.BlockSpec(g.shape, const), pl.BlockSpec(b.shape, const),
                  pl.BlockSpec(memory_space=pl.ANY)],
        out_specs=pl.BlockSpec((tc, d), tile),
        out_shape=jax.ShapeDtypeStruct((nt, d), F32),
        scratch_shapes=[pltpu.VMEM((2, TOP_K * tc * sub, LANES), ys.dtype), pltpu.SemaphoreType.DMA((2,))],
        compiler_params=_cparams(("arbitrary",)),
    )(dest_t, dest_t, w_tok, x1, wsg, wsu, wsd, g, b, ys)


def _rope_cs(pos, rope):
    half = rope // 2
    inv = ROPE_BASE ** (-jnp.arange(half, dtype=F32) / half)
    ang = pos.astype(F32)[:, None] * inv[None, :]
    cos, sin = jnp.cos(ang), jnp.sin(ang)
    pad = jnp.zeros((pos.shape[0], LANES - rope), F32)
    return (jnp.concatenate([cos, cos, pad], axis=1), jnp.concatenate([-sin, sin, pad], axis=1))


def _pick_tile(n, pref):
    t = pref
    while n % t:
        t //= 2
    return t


def kernel(x_prompt, x_sample, cache_kv_latent, cache_k_rope, state_conv, page_table, w_in, conv_w, q_norm, w_uq, kv_norm, w_uk, w_uv, g_conv, g_attn, w_o, ln1_g, ln1_b, w_router, router_bias, w_gate, w_up, w_down, ws_gate, ws_up, ws_down, ln2_g, ln2_b):
    depth = w_in.shape[0]
    bsz, seq, d = x_prompt.shape
    db, t_new, _ = x_sample.shape
    kv_lora, n_heads, nope = w_uk.shape[1:]
    v_dim = w_uv.shape[-1]
    rope = cache_k_rope.shape[-1]
    q_lora = q_norm.shape[-1]
    conv_ch = conv_w.shape[-1]
    n_exp = w_router.shape[-1]
    page = cache_kv_latent.shape[2]
    past_len = page_table.shape[1] * page
    alpha = (2.0 * depth) ** 0.25
    scale = float((nope + rope) ** -0.5) * math.log2(math.e)
    in_cols = w_in.shape[-1]
    in_pad = -(-(in_cols - rope + LANES) // LANES) * LANES
    n_p, n_s = bsz * seq, db * t_new
    nt = n_p + n_s

    tm_p = _pick_tile(seq, 256)
    tm_s = _pick_tile(n_s, 256)
    tq = _pick_tile(seq, 256)
    tm_o = _pick_tile(math.gcd(n_p, n_s), 256)
    tt = _pick_tile(math.gcd(n_p, n_s), 512)
    td = _pick_tile(nt, 1024)
    tc = _pick_tile(nt, 128)
    pps = _pick_tile(page_table.shape[1], 16)

    cos_p, sin_p = _rope_cs(jnp.arange(seq, dtype=jnp.int32), rope)
    cos_s, sin_s = _rope_cs(past_len + jnp.arange(t_new, dtype=jnp.int32), rope)
    cos_s, sin_s = jnp.tile(cos_s, (tm_s // t_new, 1)), jnp.tile(sin_s, (tm_s // t_new, 1))

    xp, xs_ = x_prompt.reshape(n_p, d), x_sample.reshape(n_s, d)
    outs = [[] for _ in range(6)]
    for l in range(depth):
        win = jnp.pad(w_in[l], ((0, 0), (0, in_pad - in_cols))).astype(BF16)
        wuq3 = w_uq[l].reshape(q_lora, n_heads, nope + rope)
        qlat = _fold_qlat(jnp.transpose(wuq3[:, :, :nope], (1, 0, 2)), jnp.transpose(w_uk[l], (1, 2, 0)), scale)
        wq_rope = jnp.transpose(wuq3[:, :, nope:], (1, 0, 2)) * scale
        wq = jnp.concatenate([qlat, wq_rope, jnp.zeros((n_heads, q_lora, LANES - rope), F32)], axis=2)
        wq = jnp.transpose(wq, (1, 0, 2)).reshape(q_lora, n_heads * 2 * LANES).astype(BF16)
        eye = jnp.eye(n_heads, dtype=F32)
        wuv_bd = jnp.einsum('chv,hg->hcgv', w_uv[l], eye).reshape(n_heads * kv_lora, n_heads * v_dim).astype(BF16)
        wo = w_o[l].astype(BF16)
        wr_t = w_router[l].T
        wr_hi = wr_t.astype(BF16)
        wr_lo = (wr_t - wr_hi.astype(F32)).astype(BF16)
        row = lambda v: v[l].reshape(1, -1)

        common = (win, wq, row(q_norm), row(kv_norm), conv_w[l], row(g_conv))
        q_p, kc_p, ckv_p, kpe_p, u_p, yc_p = _mixer_in(
            xp, *common, cos_p, sin_p, None, prompt=True, seq=seq, tm=tm_p, n_heads=n_heads, rope=rope, q_dtype=BF16)
        st = state_conv[l].astype(F32)
        zero = jnp.zeros((db, conv_ch), F32)
        prev1 = jnp.stack([st[:, 1]] + [zero] * (t_new - 1), axis=1).reshape(n_s, conv_ch)
        prev2 = jnp.stack([st[:, 0], st[:, 1]] + [zero] * (t_new - 2), axis=1).reshape(n_s, conv_ch)
        q_s, kc_s, ckv_s, kpe_s, u_s, yc_s = _mixer_in(
            xs_, *common, cos_s, sin_s, (prev1, prev2), prompt=False, seq=t_new, tm=tm_s, n_heads=n_heads, rope=rope,
            q_dtype=F32)
        del kc_s

        at_p = _prompt_attention(q_p, kc_p, batch=bsz, seq=seq, tq=tq, n_heads=n_heads)
        at_s = _sample_attention(q_s, ckv_s, kpe_s, cache_kv_latent, cache_k_rope, page_table, l,
                                 n_heads=n_heads, t_new=t_new, pps=pps)

        x1, x1p = _mixer_out(xp, xs_, yc_p, yc_s, at_p, at_s, wuv_bd, row(g_attn), wo, row(ln1_g), row(ln1_b),
                             tm=tm_o, alpha=alpha)

        idx_t, w_t, rank_t, cnt = _router(x1, wr_hi, wr_lo, router_bias[l].reshape(n_exp, 1), tt=tt)
        counts = cnt[:, 0].astype(jnp.int32)
        padded = (counts + SLOT_ROWS - 1) // SLOT_ROWS * SLOT_ROWS
        pad_end = jnp.cumsum(padded)
        pad_start = (pad_end - padded).astype(jnp.int32)
        n_blocks = -(-(nt * TOP_K) // SLOT_ROWS) + n_exp
        block_pos = jnp.arange(n_blocks, dtype=jnp.int32) * SLOT_ROWS
        block_e = jnp.minimum(jnp.searchsorted(pad_end, block_pos, side='right'), n_exp - 1).astype(jnp.int32)
        block_cnt = jnp.clip(counts[block_e] - (block_pos - pad_start[block_e]), 0, SLOT_ROWS).astype(jnp.int32)

        dest_t = pad_start[idx_t] + rank_t

        xs_sorted = _dispatch(block_cnt, dest_t, x1p, td=td, sub=d // (2 * LANES))
        ys = _experts(block_e, block_cnt, xs_sorted, w_gate, w_up, w_down, l)
        y = _combine(dest_t, w_t.T, x1, ws_gate[l].astype(BF16), ws_up[l].astype(BF16),
                     ws_down[l].astype(BF16), row(ln2_g), row(ln2_b), ys, tc=tc, alpha=alpha)

        xp, xs_ = y[:n_p], y[n_p:]
        outs[0].append(ckv_p.reshape(bsz, seq, kv_lora))
        outs[1].append(kpe_p.reshape(bsz, seq, rope))
        outs[2].append(u_p.reshape(bsz, seq, conv_ch)[:, seq - (CONV_W - 1):])
        outs[3].append(ckv_s.reshape(db, t_new, kv_lora))
        outs[4].append(kpe_s.reshape(db, t_new, rope))
        outs[5].append(u_s.reshape(db, t_new, conv_ch)[:, t_new - (CONV_W - 1):])
    return (xp.reshape(bsz, seq, d), xs_.reshape(db, t_new, d)) + tuple(jnp.stack(o) for o in outs)
```

```python
import functools
import math

import jax
import jax.numpy as jnp
from jax import lax
from jax.experimental import pallas as pl
from jax.experimental.pallas import tpu as pltpu

F32 = jnp.float32
BF16 = jnp.bfloat16
U32 = jnp.uint32

ROPE_BASE = 10000.0
NORM_EPS = 1e-6
LN_EPS = 1e-5
NEG_INF = -1e30
TOP_K = 8
N_GROUPS = 8
TOPK_GROUPS = 4
ROUTED_SCALE = 2.5
CONV_W = 3

LANES = 128
SLOT_ROWS = 256
VMEM_LIMIT = 52 * 1024 * 1024

_NT = (((1,), (1,)), ((), ()))


def _cparams(sem):
    return pltpu.CompilerParams(dimension_semantics=sem, vmem_limit_bytes=VMEM_LIMIT)


def _rms(x, g):
    return x * lax.rsqrt(jnp.mean(x * x, axis=-1, keepdims=True) + NORM_EPS) * g


def _layer_norm(x, g, b):
    mu = jnp.mean(x, axis=-1, keepdims=True)
    xc = x - mu
    var = jnp.mean(xc * xc, axis=-1, keepdims=True)
    return xc * lax.rsqrt(var + LN_EPS) * g + b


def _silu(x):
    return x / (1.0 + jnp.exp(-x))


def _rope_tile(p, c, s, half):
    lane = lax.broadcasted_iota(jnp.int32, p.shape, 1)
    swapped = jnp.where(lane < half, pltpu.roll(p, LANES - half, axis=1), pltpu.roll(p, half, axis=1))
    return p * c + swapped * s


def _pack_rows(x, ref):
    r = x.shape[0]
    half = x.shape[1] // 2
    sub = half // LANES
    bits = lax.bitcast_convert_type(x.astype(BF16).astype(F32), U32)
    words = bits[:, half:] | (bits[:, :half] >> 16)
    for j in range(sub):
        ref[pl.ds(j, r, stride=sub), :] = words[:, j * LANES:(j + 1) * LANES]


def _unpack_rows(ref, first, r, sub):
    lo, hi = [], []
    for j in range(sub):
        w = ref[pl.ds(first * sub + j, r, stride=sub), :]
        lo.append(lax.bitcast_convert_type(w << 16, F32))
        hi.append(lax.bitcast_convert_type(w & jnp.uint32(0xFFFF0000), F32))
    return lo, hi


def _fold_kernel(a_ref, b_ref, o_ref, *, scale):
    o_ref[...] = jnp.dot(a_ref[...], b_ref[...], precision=lax.Precision.HIGHEST,
                         preferred_element_type=F32) * scale


def _fold_qlat(wuq_nope, wuk_t, scale):
    h, r, n = wuq_nope.shape
    c = wuk_t.shape[-1]
    return pl.pallas_call(
        functools.partial(_fold_kernel, scale=scale),
        grid=(h,),
        in_specs=[pl.BlockSpec((None, r, n), lambda i: (i, 0, 0)),
                  pl.BlockSpec((None, n, c), lambda i: (i, 0, 0))],
        out_specs=pl.BlockSpec((None, r, c), lambda i: (i, 0, 0)),
        out_shape=jax.ShapeDtypeStruct((h, r, c), F32),
        compiler_params=_cparams(("parallel",)),
    )(wuq_nope, wuk_t)


def _mixer_in_kernel(*refs, prompt, tiles_per_seq, tm, t_new, n_heads, conv_ch, q_lora, kv_lora, rope):
    if prompt:
        (x_ref, win_ref, wq_ref, qn_ref, kvn_ref, cw_ref, gc_ref, cos_ref, sin_ref,
         q_ref, kc_ref, ckv_ref, kpe_ref, u_ref, yc_ref, carry_ref) = refs
    else:
        (x_ref, win_ref, wq_ref, qn_ref, kvn_ref, cw_ref, gc_ref, cos_ref, sin_ref, p1_ref, p2_ref,
         q_ref, kc_ref, ckv_ref, kpe_ref, u_ref, yc_ref) = refs
    half = rope // 2
    c1, c2, c3 = conv_ch, 2 * conv_ch, 3 * conv_ch
    c4 = c3 + q_lora
    c5 = c4 + kv_lora

    h = jnp.dot(x_ref[...].astype(BF16), win_ref[...], preferred_element_type=F32)
    b_gate, c_gate, x_conv = h[:, :c1], h[:, c1:c2], h[:, c2:c3]
    q_a, c_kv, kp = h[:, c3:c4], h[:, c4:c5], h[:, c5:c5 + LANES]

    u = c_gate * x_conv
    u_ref[...] = u
    row = lax.broadcasted_iota(jnp.int32, (tm, 1), 0)
    if prompt:
        first = (pl.program_id(0) % tiles_per_seq) == 0
        prev1 = jnp.where(first, 0.0, carry_ref[7:8, :])
        prev2 = jnp.where(first, 0.0, carry_ref[6:7, :])
        t_in = row
        p1 = prev1
        p2 = jnp.where(row == 0, prev2, prev1)
        carry_ref[...] = u[tm - 8:, :]
    else:
        t_in = row & (t_new - 1)
        p1 = p1_ref[...]
        p2 = p2_ref[...]
    um1 = jnp.where(t_in == 0, p1, pltpu.roll(u, 1, axis=0))
    um2 = jnp.where(t_in < 2, p2, pltpu.roll(u, 2, axis=0))
    conv_y = cw_ref[0:1, :] * um2 + cw_ref[1:2, :] * um1 + cw_ref[2:3, :] * u
    yc_ref[...] = _rms(b_gate * conv_y, gc_ref[...]).astype(yc_ref.dtype)

    cos = cos_ref[...]
    sin = sin_ref[...]
    qn = _rms(q_a, qn_ref[...]).astype(BF16)
    q = jnp.dot(qn, wq_ref[...], preferred_element_type=F32)
    for hd in range(n_heads):
        o = hd * 2 * LANES
        q_ref[:, o:o + LANES] = q[:, o:o + LANES].astype(q_ref.dtype)
        q_ref[:, o + LANES:o + 2 * LANES] = _rope_tile(q[:, o + LANES:o + 2 * LANES], cos, sin, half).astype(q_ref.dtype)

    ckv = _rms(c_kv, kvn_ref[...])
    ckv_ref[...] = ckv
    kpr = _rope_tile(kp, cos, sin, half)
    kpe_ref[...] = kpr[:, :rope]
    kc_ref[:, :LANES] = ckv.astype(BF16)
    kc_ref[:, LANES:] = kpr.astype(BF16)


def _mixer_in(x2, win, wq, qn, kvn, cw, gc, cos, sin, prev, *, prompt, seq, tm, n_heads, rope, q_dtype):
    n, d = x2.shape
    conv_ch = cw.shape[-1]
    q_lora = qn.shape[-1]
    kv_lora = kvn.shape[-1]
    assert n % tm == 0 and kv_lora == LANES and tm % 8 == 0
    tiles_per_seq = seq // tm if prompt else 1
    if prompt:
        assert seq % tm == 0
    else:
        assert seq & (seq - 1) == 0 and tm % seq == 0
    const = lambda i: (0, 0)
    tile = lambda i: (i, 0)
    in_specs = [
        pl.BlockSpec((tm, d), tile),
        pl.BlockSpec(win.shape, const),
        pl.BlockSpec(wq.shape, const),
        pl.BlockSpec(qn.shape, const),
        pl.BlockSpec(kvn.shape, const),
        pl.BlockSpec(cw.shape, const),
        pl.BlockSpec(gc.shape, const),
    ]
    args = [x2, win, wq, qn, kvn, cw, gc, cos, sin]
    if prompt:
        in_specs += [pl.BlockSpec((tm, LANES), lambda i: (i % tiles_per_seq, 0))] * 2
        scratch = [pltpu.VMEM((8, conv_ch), F32)]
    else:
        in_specs += [pl.BlockSpec((tm, LANES), const)] * 2
        in_specs += [pl.BlockSpec((tm, conv_ch), tile)] * 2
        args += list(prev)
        scratch = []
    qw = wq.shape[-1]
    out_shape = [
        jax.ShapeDtypeStruct((n, qw), q_dtype),
        jax.ShapeDtypeStruct((n, 2 * LANES), BF16),
        jax.ShapeDtypeStruct((n, kv_lora), F32),
        jax.ShapeDtypeStruct((n, rope), F32),
        jax.ShapeDtypeStruct((n, conv_ch), F32),
        jax.ShapeDtypeStruct((n, conv_ch), BF16),
    ]
    out_specs = [
        pl.BlockSpec((tm, qw), tile),
        pl.BlockSpec((tm, 2 * LANES), tile),
        pl.BlockSpec((tm, kv_lora), tile),
        pl.BlockSpec((tm, rope), tile),
        pl.BlockSpec((tm, conv_ch), tile),
        pl.BlockSpec((tm, conv_ch), tile),
    ]
    return pl.pallas_call(
        functools.partial(_mixer_in_kernel, prompt=prompt, tiles_per_seq=tiles_per_seq, tm=tm, t_new=seq,
                          n_heads=n_heads, conv_ch=conv_ch, q_lora=q_lora, kv_lora=kv_lora, rope=rope),
        grid=(n // tm,),
        in_specs=in_specs,
        out_specs=out_specs,
        out_shape=out_shape,
        scratch_shapes=scratch,
        compiler_params=_cparams(("arbitrary",)),
    )(*args)


def _softmax_update(s, v, m_ref, l_ref, acc_ref):
    m_prev = m_ref[...]
    m_new = jnp.maximum(m_prev, jnp.max(s, axis=1, keepdims=True))
    alpha = jnp.exp2(m_prev - m_new)
    p = jnp.exp2(s - m_new)
    l_ref[...] = alpha * l_ref[...] + jnp.sum(p, axis=1, keepdims=True)
    acc_ref[...] = alpha * acc_ref[...] + jnp.dot(p.astype(BF16), v, preferred_element_type=F32)
    m_ref[...] = m_new


def _prompt_attn_kernel(q_ref, k_ref, o_ref, qs_ref, m_ref, l_ref, acc_ref, *, tq, n_heads, chunk_heads):
    i = pl.program_id(1)
    cr = chunk_heads * tq
    n_chunks = n_heads // chunk_heads
    for hd in range(n_heads):
        qs_ref[hd * tq:(hd + 1) * tq, :] = q_ref[:, hd * 2 * LANES:(hd + 1) * 2 * LANES]
    m_ref[...] = jnp.full(m_ref.shape, NEG_INF, F32)
    l_ref[...] = jnp.zeros(l_ref.shape, F32)
    acc_ref[...] = jnp.zeros(acc_ref.shape, F32)

    def step(j, masked):
        k = k_ref[pl.ds(pl.multiple_of(j * tq, tq), tq), :]
        v = k[:, :LANES]
        for c in range(n_chunks):
            rows = slice(c * cr, (c + 1) * cr)
            s = lax.dot_general(qs_ref[rows, :], k, _NT, preferred_element_type=F32)
            if masked:
                t = lax.broadcasted_iota(jnp.int32, s.shape, 0) & (tq - 1)
                col = lax.broadcasted_iota(jnp.int32, s.shape, 1)
                s = jnp.where(col <= t, s, NEG_INF)
            parts = [s[:, w * LANES:(w + 1) * LANES] for w in range(tq // LANES)]
            mc = parts[0]
            for part in parts[1:]:
                mc = jnp.maximum(mc, part)
            m_prev = m_ref[rows, :]
            m_new = jnp.maximum(m_prev, jnp.max(mc, axis=1, keepdims=True))
            alpha = jnp.exp2(m_prev - m_new)
            ps = [jnp.exp2(part - m_new) for part in parts]
            psum = ps[0]
            for pp in ps[1:]:
                psum = psum + pp
            l_ref[rows, :] = alpha * l_ref[rows, :] + psum
            p = jnp.concatenate(ps, axis=1).astype(BF16)
            acc_ref[rows, :] = alpha * acc_ref[rows, :] + jnp.dot(p, v, preferred_element_type=F32)
            m_ref[rows, :] = m_new

    def body(j, carry):
        step(j, False)
        return carry

    lax.fori_loop(0, i, body, 0)
    step(i, True)
    out = acc_ref[...] / jnp.sum(l_ref[...], axis=1, keepdims=True)
    for hd in range(n_heads):
        o_ref[:, hd * LANES:(hd + 1) * LANES] = out[hd * tq:(hd + 1) * tq, :].astype(o_ref.dtype)


def _prompt_attention(q, kc, *, batch, seq, tq, n_heads):
    n = q.shape[0]
    nq = seq // tq
    rows = n_heads * tq
    chunk_heads = 2 if n_heads % 2 == 0 else 1
    assert tq & (tq - 1) == 0 and seq % tq == 0 and tq % LANES == 0
    return pl.pallas_call(
        functools.partial(_prompt_attn_kernel, tq=tq, n_heads=n_heads, chunk_heads=chunk_heads),
        grid=(batch, nq),
        in_specs=[pl.BlockSpec((tq, q.shape[1]), lambda b, i: (b * nq + i, 0)),
                  pl.BlockSpec((seq, kc.shape[1]), lambda b, i: (b, 0))],
        out_specs=pl.BlockSpec((tq, n_heads * LANES), lambda b, i: (b * nq + i, 0)),
        out_shape=jax.ShapeDtypeStruct((n, n_heads * LANES), BF16),
        scratch_shapes=[pltpu.VMEM((rows, 2 * LANES), BF16),
                        pltpu.VMEM((rows, LANES), F32),
                        pltpu.VMEM((rows, LANES), F32),
                        pltpu.VMEM((rows, LANES), F32)],
        compiler_params=_cparams(("parallel", "arbitrary")),
    )(q, kc)


def _sample_attn_kernel(pt_ref, q_ref, ckvn_ref, kpen_ref, kv_hbm, rp_hbm, o_ref, kvbuf, rpbuf, qs_ref, m_ref, l_ref,
                        acc_ref, sem, *, layer, n_pages, cpages, n_heads, t_new, rope, page):
    b = pl.program_id(0)
    nb = pl.num_programs(0)
    slot = b % 2

    def fetch(seq, s):
        def body(p, carry):
            pg = pt_ref[seq, p]
            pltpu.make_async_copy(kv_hbm.at[layer, pg], kvbuf.at[s, p], sem.at[0, s]).start()
            pltpu.make_async_copy(rp_hbm.at[layer, pg], rpbuf.at[s, p], sem.at[1, s]).start()
            return carry
        lax.fori_loop(0, n_pages, body, 0)

    @pl.when(b == 0)
    def _():
        fetch(0, 0)

    @pl.when(b + 1 < nb)
    def _():
        fetch(b + 1, 1 - slot)

    for hd in range(n_heads):
        qs_ref[hd * t_new:(hd + 1) * t_new, :] = q_ref[:, hd * 2 * LANES:(hd + 1) * 2 * LANES]
    m_ref[...] = jnp.full(m_ref.shape, NEG_INF, F32)
    l_ref[...] = jnp.zeros(l_ref.shape, F32)
    acc_ref[...] = jnp.zeros(acc_ref.shape, F32)
    qs = qs_ref[...]
    ql = qs[:, :LANES].astype(BF16)
    qp = qs[:, LANES:LANES + rope].astype(BF16)

    pltpu.make_async_copy(kv_hbm.at[layer, pl.ds(0, n_pages)], kvbuf.at[slot], sem.at[0, slot]).wait()
    pltpu.make_async_copy(rp_hbm.at[layer, pl.ds(0, n_pages)], rpbuf.at[slot], sem.at[1, slot]).wait()
    for c in range(n_pages // cpages):
        kv = kvbuf[slot, c * cpages:(c + 1) * cpages].reshape(cpages * page, LANES).astype(BF16)
        rp = jnp.concatenate([rpbuf[slot, c * cpages + p] for p in range(cpages)], axis=1).astype(BF16)
        s = (lax.dot_general(ql, kv, _NT, preferred_element_type=F32)
             + jnp.dot(qp, rp, preferred_element_type=F32))
        _softmax_update(s, kv, m_ref, l_ref, acc_ref)

    kn = jnp.concatenate([ckvn_ref[...], jnp.zeros((page - t_new, LANES), F32)], axis=0).astype(BF16)
    rn = jnp.concatenate([kpen_ref[...], jnp.zeros((page - t_new, rope), F32)], axis=0).astype(BF16)
    s2 = (lax.dot_general(ql, kn, _NT, preferred_element_type=F32)
          + lax.dot_general(qp, rn, _NT, preferred_element_type=F32))
    t = lax.broadcasted_iota(jnp.int32, s2.shape, 0) & (t_new - 1)
    col = lax.broadcasted_iota(jnp.int32, s2.shape, 1)
    s2 = jnp.where(col <= t, s2, NEG_INF)
    _softmax_update(s2, kn, m_ref, l_ref, acc_ref)
    out = acc_ref[...] / l_ref[...]
    for hd in range(n_heads):
        o_ref[:, hd * LANES:(hd + 1) * LANES] = out[hd * t_new:(hd + 1) * t_new, :]


def _sample_attention(q, ckv_new, kpe_new, cache_kv, cache_rope_t, page_table, layer, *, n_heads, t_new, cpages):
    n = q.shape[0]
    db = n // t_new
    n_pages = page_table.shape[1]
    page = cache_kv.shape[2]
    rope = cache_rope_t.shape[2]
    assert n_pages % cpages == 0 and t_new == 8 and cache_kv.shape[-1] == LANES and cache_rope_t.shape[3] == page
    rows = n_heads * t_new
    per_seq = lambda b, pt: (b, 0)
    return pl.pallas_call(
        functools.partial(_sample_attn_kernel, layer=layer, n_pages=n_pages, cpages=cpages, n_heads=n_heads,
                          t_new=t_new, rope=rope, page=page),
        grid_spec=pltpu.PrefetchScalarGridSpec(
            num_scalar_prefetch=1,
            grid=(db,),
            in_specs=[pl.BlockSpec((t_new, q.shape[1]), per_seq),
                      pl.BlockSpec((t_new, LANES), per_seq),
                      pl.BlockSpec((t_new, rope), per_seq),
                      pl.BlockSpec(memory_space=pl.ANY),
                      pl.BlockSpec(memory_space=pl.ANY)],
            out_specs=pl.BlockSpec((t_new, n_heads * LANES), per_seq),
            scratch_shapes=[pltpu.VMEM((2, n_pages, page, LANES), cache_kv.dtype),
                            pltpu.VMEM((2, n_pages, rope, page), cache_rope_t.dtype),
                            pltpu.VMEM((rows, 2 * LANES), F32),
                            pltpu.VMEM((rows, 1), F32),
                            pltpu.VMEM((rows, 1), F32),
                            pltpu.VMEM((rows, LANES), F32),
                            pltpu.SemaphoreType.DMA((2, 2))]),
        out_shape=jax.ShapeDtypeStruct((n, n_heads * LANES), F32),
        compiler_params=_cparams(("arbitrary",)),
    )(page_table, q, ckv_new, kpe_new, cache_kv, cache_rope_t)


def _mixer_out_kernel(xp_ref, xs_ref, ycp_ref, ycs_ref, atp_ref, ats_ref, wuv_ref, ga_ref, wo_ref, g_ref, b_ref,
                      x1_ref, x1p_ref, *, n_prompt_tiles, conv_ch, alpha):
    is_p = pl.program_id(0) < n_prompt_tiles
    x = jnp.where(is_p, xp_ref[...], xs_ref[...])
    yc = jnp.where(is_p, ycp_ref[...], ycs_ref[...])
    at = jnp.where(is_p, atp_ref[...], ats_ref[...].astype(BF16))
    o = jnp.dot(at, wuv_ref[...], preferred_element_type=F32)
    ya = _rms(o, ga_ref[...]).astype(BF16)
    mix = (jnp.dot(yc, wo_ref[:conv_ch, :], preferred_element_type=F32)
           + jnp.dot(ya, wo_ref[conv_ch:, :], preferred_element_type=F32))
    x1 = _layer_norm(alpha * x + mix, g_ref[...], b_ref[...])
    x1_ref[...] = x1
    _pack_rows(x1, x1p_ref)


def _mixer_out(xp, xs, ycp, ycs, atp, ats, wuv_bd, ga, wo, g, b, *, tm, alpha):
    np_, d = xp.shape
    ns = xs.shape[0]
    assert np_ % tm == 0 and ns % tm == 0
    npt, nst = np_ // tm, ns // tm
    conv_ch = ycp.shape[1]
    p_map = lambda i: (jnp.minimum(i, npt - 1), 0)
    s_map = lambda i: (jnp.maximum(i - npt, 0), 0)
    const = lambda i: (0, 0)
    tile = lambda i: (i, 0)
    nt = np_ + ns
    sub = d // (2 * LANES)
    return pl.pallas_call(
        functools.partial(_mixer_out_kernel, n_prompt_tiles=npt, conv_ch=conv_ch, alpha=alpha),
        grid=(npt + nst,),
        in_specs=[pl.BlockSpec((tm, d), p_map), pl.BlockSpec((tm, d), s_map),
                  pl.BlockSpec((tm, conv_ch), p_map), pl.BlockSpec((tm, conv_ch), s_map),
                  pl.BlockSpec((tm, atp.shape[1]), p_map), pl.BlockSpec((tm, ats.shape[1]), s_map),
                  pl.BlockSpec(wuv_bd.shape, const), pl.BlockSpec(ga.shape, const),
                  pl.BlockSpec(wo.shape, const), pl.BlockSpec(g.shape, const), pl.BlockSpec(b.shape, const)],
        out_specs=[pl.BlockSpec((tm, d), tile), pl.BlockSpec((tm * sub, LANES), tile)],
        out_shape=[jax.ShapeDtypeStruct((nt, d), F32), jax.ShapeDtypeStruct((nt * sub, LANES), U32)],
        compiler_params=_cparams(("parallel",)),
    )(xp, xs, ycp, ycs, atp, ats, wuv_bd, ga, wo, g, b)


def _router_kernel(x_ref, wh_ref, wl_ref, b_ref, tri_ref, idx_ref, w_ref, rank_ref, cnt_ref, base_ref,
                   *, tt, n_exp):
    @pl.when(pl.program_id(0) == 0)
    def _():
        base_ref[...] = jnp.zeros(base_ref.shape, F32)

    x = x_ref[...]
    xh = x.astype(BF16)
    xl = (x - xh.astype(F32)).astype(BF16)
    wh = wh_ref[...]
    z = (lax.dot_general(wh, xh, _NT, preferred_element_type=F32)
         + lax.dot_general(wh, xl, _NT, preferred_element_type=F32)
         + lax.dot_general(wl_ref[...], xh, _NT, preferred_element_type=F32))
    s = 1.0 / (1.0 + jnp.exp(-z))
    sb = s + b_ref[...]

    gsz = n_exp // N_GROUPS
    git = lax.broadcasted_iota(jnp.int32, (gsz, tt), 0).astype(F32)
    blocks, gscore = [], []
    for g in range(N_GROUPS):
        blk = sb[g * gsz:(g + 1) * gsz, :]
        m1 = jnp.max(blk, axis=0, keepdims=True)
        f1 = jnp.min(jnp.where(blk == m1, git, float(gsz)), axis=0, keepdims=True)
        m2 = jnp.max(jnp.where(git == f1, -jnp.inf, blk), axis=0, keepdims=True)
        blocks.append(blk)
        gscore.append(m1 + m2)
    masked = []
    for g in range(N_GROUPS):
        ahead = jnp.zeros((1, tt), F32)
        for g2 in range(N_GROUPS):
            if g2 == g:
                continue
            beats = (gscore[g2] >= gscore[g]) if g2 < g else (gscore[g2] > gscore[g])
            ahead = ahead + jnp.where(beats, 1.0, 0.0)
        masked.append(jnp.where(ahead < float(TOPK_GROUPS), blocks[g], -jnp.inf))
    vals = jnp.concatenate(masked, axis=0)

    rowi = lax.broadcasted_iota(jnp.int32, (n_exp, tt), 0).astype(F32)
    chosen = jnp.zeros((n_exp, tt), F32)
    picks, wks = [], []
    for k in range(TOP_K):
        m = jnp.max(vals, axis=0, keepdims=True)
        ik = jnp.min(jnp.where(vals == m, rowi, float(n_exp)), axis=0, keepdims=True)
        hit = rowi == ik
        wks.append(jnp.sum(jnp.where(hit, s, 0.0), axis=0, keepdims=True))
        vals = jnp.where(hit, -jnp.inf, vals)
        chosen = chosen + jnp.where(hit, 1.0, 0.0)
        picks.append(ik)
    wsum = wks[0]
    for k in range(1, TOP_K):
        wsum = wsum + wks[k]

    incl = jnp.dot(chosen.astype(BF16), tri_ref[...], preferred_element_type=F32)
    rnk = base_ref[...] + (incl - chosen)
    for k in range(TOP_K):
        idx_ref[k:k + 1, :] = picks[k].astype(jnp.int32)
        w_ref[k:k + 1, :] = wks[k] / wsum * ROUTED_SCALE
        rk = jnp.sum(jnp.where(rowi == picks[k], rnk, 0.0), axis=0, keepdims=True)
        rank_ref[k:k + 1, :] = rk.astype(jnp.int32)
    base = base_ref[...] + jnp.sum(chosen, axis=1, keepdims=True)
    base_ref[...] = base
    cnt_ref[...] = jnp.broadcast_to(base, cnt_ref.shape)


def _router(x1, wr_hi, wr_lo, bias, *, tt):
    nt, d = x1.shape
    n_exp = wr_hi.shape[0]
    assert nt % tt == 0
    tri = jnp.triu(jnp.ones((tt, tt), BF16))
    const = lambda i: (0, 0)
    col = lambda i: (0, i)
    return pl.pallas_call(
        functools.partial(_router_kernel, tt=tt, n_exp=n_exp),
        grid=(nt // tt,),
        in_specs=[pl.BlockSpec((tt, d), lambda i: (i, 0)),
                  pl.BlockSpec(wr_hi.shape, const), pl.BlockSpec(wr_lo.shape, const),
                  pl.BlockSpec(bias.shape, const), pl.BlockSpec(tri.shape, const)],
        out_specs=[pl.BlockSpec((TOP_K, tt), col), pl.BlockSpec((TOP_K, tt), col),
                   pl.BlockSpec((TOP_K, tt), col), pl.BlockSpec((n_exp, LANES), const)],
        out_shape=[jax.ShapeDtypeStruct((TOP_K, nt), jnp.int32), jax.ShapeDtypeStruct((TOP_K, nt), F32),
                   jax.ShapeDtypeStruct((TOP_K, nt), jnp.int32), jax.ShapeDtypeStruct((n_exp, LANES), F32)],
        scratch_shapes=[pltpu.VMEM((n_exp, 1), F32)],
        compiler_params=_cparams(("arbitrary",)),
    )(x1, wr_hi, wr_lo, bias, tri)


def _slot_rows_kernel(idx_ref, rank_ref, ps_ref, o_ref, *, n_exp):
    tt = idx_ref.shape[1]
    rowi = lax.broadcasted_iota(jnp.int32, (n_exp, tt), 0)
    ps = ps_ref[...]
    for k in range(TOP_K):
        start = jnp.sum(jnp.where(rowi == idx_ref[k:k + 1, :], ps, 0.0), axis=0, keepdims=True)
        o_ref[k:k + 1, :] = start.astype(jnp.int32) + rank_ref[k:k + 1, :]


def _slot_rows(idx_t, rank_t, pad_start, *, tt):
    nt = idx_t.shape[1]
    n_exp = pad_start.shape[0]
    assert nt % tt == 0 and n_exp * SLOT_ROWS + nt * TOP_K < 2 ** 24
    col = lambda i: (0, i)
    return pl.pallas_call(
        functools.partial(_slot_rows_kernel, n_exp=n_exp),
        grid=(nt // tt,),
        in_specs=[pl.BlockSpec((TOP_K, tt), col), pl.BlockSpec((TOP_K, tt), col),
                  pl.BlockSpec((n_exp, 1), lambda i: (0, 0))],
        out_specs=pl.BlockSpec((TOP_K, tt), col),
        out_shape=jax.ShapeDtypeStruct((TOP_K, nt), jnp.int32),
        compiler_params=_cparams(("parallel",)),
    )(idx_t, rank_t, pad_start.astype(F32).reshape(n_exp, 1))


def _dispatch_kernel(ps_ref, nb_ref, cnt_ref, dest_ref, x_ref, o_ref, zbuf, sem, zsem, *, td, sub, n_exp):
    i = pl.program_id(0)
    blk = SLOT_ROWS * sub

    @pl.when(i == 0)
    def _():
        zbuf[...] = jnp.zeros(zbuf.shape, zbuf.dtype)
        n_blocks = o_ref.shape[0] // blk
        used = ps_ref[n_exp - 1] // SLOT_ROWS + nb_ref[n_exp - 1]

        def zero_block(b):
            return pltpu.make_async_copy(zbuf, o_ref.at[pl.ds(pl.multiple_of(b * blk, blk), blk)], zsem)

        def partial(e):
            return (cnt_ref[e] & (SLOT_ROWS - 1)) != 0

        def fill(e, carry):
            @pl.when(partial(e))
            def _():
                zero_block(ps_ref[e] // SLOT_ROWS + nb_ref[e] - 1).start()
            return carry

        def fill_done(e, carry):
            @pl.when(partial(e))
            def _():
                zero_block(0).wait()
            return carry

        def tail(b, carry):
            zero_block(b).start()
            return carry

        def tail_done(b, carry):
            zero_block(0).wait()
            return carry

        lax.fori_loop(0, n_exp, fill, 0)
        lax.fori_loop(used, n_blocks, tail, 0)
        lax.fori_loop(0, n_exp, fill_done, 0)
        lax.fori_loop(used, n_blocks, tail_done, 0)

    def per_token(t, carry):
        src = x_ref.at[pl.ds(pl.multiple_of(t * sub, sub), sub)]
        for k in range(TOP_K):
            dst = pl.multiple_of(dest_ref[k, t] * sub, sub)
            pltpu.make_async_copy(src, o_ref.at[pl.ds(dst, sub)], sem).start(priority=k % 2)
        return carry

    lax.fori_loop(0, td, per_token, 0)
    for _ in range(TOP_K):
        pltpu.make_async_copy(x_ref, o_ref.at[pl.ds(0, td * sub)], sem).wait()


def _dispatch(pad_start, n_blk, counts, dest_t, x1p, n_slots, *, td, sub):
    nt = x1p.shape[0] // sub
    assert nt % td == 0 and n_slots >= td and n_slots % SLOT_ROWS == 0
    return pl.pallas_call(
        functools.partial(_dispatch_kernel, td=td, sub=sub, n_exp=counts.shape[0]),
        grid_spec=pltpu.PrefetchScalarGridSpec(
            num_scalar_prefetch=3,
            grid=(nt // td,),
            in_specs=[pl.BlockSpec((TOP_K, td), lambda i, *_: (0, i), memory_space=pltpu.SMEM),
                      pl.BlockSpec((td * sub, LANES), lambda i, *_: (i, 0))],
            out_specs=pl.BlockSpec(memory_space=pl.ANY),
            scratch_shapes=[pltpu.VMEM((SLOT_ROWS * sub, LANES), x1p.dtype), pltpu.SemaphoreType.DMA(()),
                            pltpu.SemaphoreType.DMA(())]),
        out_shape=jax.ShapeDtypeStruct((n_slots * sub, LANES), x1p.dtype),
        compiler_params=_cparams(("arbitrary",)),
    )(pad_start, n_blk, counts, dest_t, x1p)


def _experts_kernel(ps_ref, nb_ref, wg_ref, wu_ref, wd_ref, xs_hbm, ys_hbm, wg_s, wu_s, wd_s, xbuf, ybuf,
                    xsem, ysem, zsem, *, sub):
    e = pl.program_id(0)
    n_exp = pl.num_programs(0)
    blk = SLOT_ROWS * sub
    n_blocks = xs_hbm.shape[0] // blk

    def block_rows(first_block, j):
        return pl.ds(pl.multiple_of((first_block + j) * blk, blk), blk)

    @pl.when(e == 0)
    def _():
        used = ps_ref[n_exp - 1] // SLOT_ROWS + nb_ref[n_exp - 1]
        ybuf[1] = jnp.zeros(ybuf.shape[1:], ybuf.dtype)

        def fill(b, carry):
            pltpu.make_async_copy(ybuf.at[1], ys_hbm.at[block_rows(b, 0)], zsem).start()
            return carry

        def fill_done(b, carry):
            pltpu.make_async_copy(ybuf.at[1], ys_hbm.at[block_rows(0, 0)], zsem).wait()
            return carry

        lax.fori_loop(used, n_blocks, fill, 0)
        lax.fori_loop(used, n_blocks, fill_done, 0)

    nb = nb_ref[e]
    first = ps_ref[e] // SLOT_ROWS

    def x_copy(j, s):
        return pltpu.make_async_copy(xs_hbm.at[block_rows(first, j)], xbuf.at[s], xsem.at[s])

    def y_copy(j, s):
        return pltpu.make_async_copy(ybuf.at[s], ys_hbm.at[block_rows(first, j)], ysem.at[s])

    @pl.when(nb > 0)
    def _():
        x_copy(0, 0).start()
        wg_s[...] = wg_ref[...].astype(BF16)
        wu_s[...] = wu_ref[...].astype(BF16)
        wd_s[...] = wd_ref[...].astype(BF16)

        def pair(p, carry):
            for s in (0, 1):
                j = 2 * p + s

                @pl.when(j < nb)
                def _():
                    x_copy(j, s).wait()

                    @pl.when(j + 1 < nb)
                    def _():
                        x_copy(j + 1, 1 - s).start()

                    @pl.when(j >= 2)
                    def _():
                        y_copy(j - 2, s).wait()

                    lo, hi = _unpack_rows(xbuf.at[s], 0, SLOT_ROWS, sub)
                    x = jnp.concatenate(lo + hi, axis=1).astype(BF16)
                    g = jnp.dot(x, wg_s[...], preferred_element_type=F32)
                    u = jnp.dot(x, wu_s[...], preferred_element_type=F32)
                    hmid = (_silu(g) * u).astype(BF16)
                    _pack_rows(jnp.dot(hmid, wd_s[...], preferred_element_type=F32), ybuf.at[s])
                    y_copy(j, s).start()
            return carry

        lax.fori_loop(0, (nb + 1) // 2, pair, 0)
        y_copy(0, 0).wait()

        @pl.when(nb >= 2)
        def _():
            y_copy(1, 1).wait()


def _experts(pad_start, n_blk, xs, w_gate, w_up, w_down, layer):
    n_exp, d, de = w_gate.shape[-3:]
    sub = d // (2 * LANES)
    blk = SLOT_ROWS * sub
    assert xs.shape[0] % blk == 0 and xs.shape[1] == LANES
    return pl.pallas_call(
        functools.partial(_experts_kernel, sub=sub),
        grid_spec=pltpu.PrefetchScalarGridSpec(
            num_scalar_prefetch=2,
            grid=(n_exp,),
            in_specs=[pl.BlockSpec((None, None, d, de), lambda e, ps, nb: (layer, e, 0, 0)),
                      pl.BlockSpec((None, None, d, de), lambda e, ps, nb: (layer, e, 0, 0)),
                      pl.BlockSpec((None, None, de, d), lambda e, ps, nb: (layer, e, 0, 0)),
                      pl.BlockSpec(memory_space=pl.ANY)],
            out_specs=pl.BlockSpec(memory_space=pl.ANY),
            scratch_shapes=[pltpu.VMEM((d, de), BF16), pltpu.VMEM((d, de), BF16), pltpu.VMEM((de, d), BF16),
                            pltpu.VMEM((2, blk, LANES), U32), pltpu.VMEM((2, blk, LANES), U32),
                            pltpu.SemaphoreType.DMA((2,)), pltpu.SemaphoreType.DMA((2,)),
                            pltpu.SemaphoreType.DMA(())]),
        out_shape=jax.ShapeDtypeStruct(xs.shape, U32),
        compiler_params=_cparams(("arbitrary",)),
    )(pad_start, n_blk, w_gate, w_up, w_down, xs)


def _combine_kernel(dest0_ref, dest1_ref, w_ref, x1_ref, wsg_ref, wsu_ref,
                    wsd_ref, g_ref, b_ref, ys_ref, yp_ref, ysm_ref, buf, sem, *, tc, alpha, sub, n_prompt_tiles):
    i = pl.program_id(0)
    n = pl.num_programs(0)
    slot = i % 2

    def gather_row(dest_ref, s, k, t):
        src = pl.multiple_of(dest_ref[k, t] * sub, sub)
        dst = (k * tc + t) * sub
        if not isinstance(dst, int):
            dst = pl.multiple_of(dst, sub)
        pltpu.make_async_copy(ys_ref.at[pl.ds(src, sub)], buf.at[s, pl.ds(dst, sub)], sem.at[s]).start(priority=k % 2)

    @pl.when(i == 0)
    def _():
        def per_token(t, carry):
            for k in range(TOP_K):
                gather_row(dest0_ref, 0, k, t)
            return carry
        lax.fori_loop(0, tc, per_token, 0)

    for t in range(tc):
        for k in range(TOP_K):
            gather_row(dest1_ref, 1 - slot, k, t)

    x1 = x1_ref[...]
    xb = x1.astype(BF16)
    hs = (_silu(jnp.dot(xb, wsg_ref[...], preferred_element_type=F32))
          * jnp.dot(xb, wsu_ref[...], preferred_element_type=F32)).astype(BF16)
    shared = jnp.dot(hs, wsd_ref[...], preferred_element_type=F32)

    def drain(s):
        pltpu.make_async_copy(ys_ref.at[pl.ds(0, TOP_K * tc * sub)], buf.at[s], sem.at[s]).wait()

    drain(slot)
    w = w_ref[...]
    lo_acc = [jnp.zeros((tc, LANES), F32) for _ in range(sub)]
    hi_acc = [jnp.zeros((tc, LANES), F32) for _ in range(sub)]
    for k in range(TOP_K):
        lo, hi = _unpack_rows(buf.at[slot], k * tc, tc, sub)
        wk = w[:, k:k + 1]
        for j in range(sub):
            lo_acc[j] = lo_acc[j] + wk * lo[j]
            hi_acc[j] = hi_acc[j] + wk * hi[j]
    moe = jnp.concatenate(lo_acc + hi_acc, axis=1) + shared
    y = _layer_norm(alpha * x1 + moe, g_ref[...], b_ref[...])

    @pl.when(i < n_prompt_tiles)
    def _():
        yp_ref[...] = y

    @pl.when(i >= n_prompt_tiles)
    def _():
        ysm_ref[...] = y

    @pl.when(i == n - 1)
    def _():
        drain(1 - slot)


def _combine(dest_t, w_tok, x1, wsg, wsu, wsd, g, b, ys, *, tc, alpha, n_prompt):
    nt, d = x1.shape
    n_tiles = nt // tc
    sub = d // (2 * LANES)
    assert nt % tc == 0 and n_prompt % tc == 0 and 0 < n_prompt < nt and ys.shape[0] >= TOP_K * tc * sub
    npt = n_prompt // tc
    cur = pl.BlockSpec((TOP_K, tc), lambda i: (0, i), memory_space=pltpu.SMEM)
    nxt = pl.BlockSpec((TOP_K, tc), lambda i: (0, jnp.minimum(i + 1, n_tiles - 1)), memory_space=pltpu.SMEM)
    const = lambda i: (0, 0)
    tile = lambda i: (i, 0)
    return pl.pallas_call(
        functools.partial(_combine_kernel, tc=tc, alpha=alpha, sub=sub, n_prompt_tiles=npt),
        grid=(n_tiles,),
        in_specs=[cur, nxt,
                  pl.BlockSpec((tc, TOP_K), tile),
                  pl.BlockSpec((tc, d), tile),
                  pl.BlockSpec(wsg.shape, const), pl.BlockSpec(wsu.shape, const), pl.BlockSpec(wsd.shape, const),
                  pl.BlockSpec(g.shape, const), pl.BlockSpec(b.shape, const),
                  pl.BlockSpec(memory_space=pl.ANY)],
        out_specs=[pl.BlockSpec((tc, d), lambda i: (jnp.minimum(i, npt - 1), 0)),
                   pl.BlockSpec((tc, d), lambda i: (jnp.maximum(i - npt, 0), 0))],
        out_shape=[jax.ShapeDtypeStruct((n_prompt, d), F32), jax.ShapeDtypeStruct((nt - n_prompt, d), F32)],
        scratch_shapes=[pltpu.VMEM((2, TOP_K * tc * sub, LANES), ys.dtype), pltpu.SemaphoreType.DMA((2,))],
        compiler_params=_cparams(("arbitrary",)),
    )(dest_t, dest_t, w_tok, x1, wsg, wsu, wsd, g, b, ys)


def _rope_cs(pos, rope):
    half = rope // 2
    inv = ROPE_BASE ** (-jnp.arange(half, dtype=F32) / half)
    ang = pos.astype(F32)[:, None] * inv[None, :]
    cos, sin = jnp.cos(ang), jnp.sin(ang)
    pad = jnp.zeros((pos.shape[0], LANES - rope), F32)
    return (jnp.concatenate([cos, cos, pad], axis=1), jnp.concatenate([-sin, sin, pad], axis=1))


def _pick_tile(n, pref):
    t = pref
    while n % t:
        t //= 2
    return t


def kernel(x_prompt, x_sample, cache_kv_latent, cache_k_rope, state_conv, page_table, w_in, conv_w, q_norm, w_uq, kv_norm, w_uk, w_uv, g_conv, g_attn, w_o, ln1_g, ln1_b, w_router, router_bias, w_gate, w_up, w_down, ws_gate, ws_up, ws_down, ln2_g, ln2_b):
    depth = w_in.shape[0]
    bsz, seq, d = x_prompt.shape
    db, t_new, _ = x_sample.shape
    kv_lora, n_heads, nope = w_uk.shape[1:]
    v_dim = w_uv.shape[-1]
    rope = cache_k_rope.shape[-1]
    q_lora = q_norm.shape[-1]
    conv_ch = conv_w.shape[-1]
    n_exp = w_router.shape[-1]
    page = cache_kv_latent.shape[2]
    past_len = page_table.shape[1] * page
    alpha = (2.0 * depth) ** 0.25
    scale = float((nope + rope) ** -0.5) * math.log2(math.e)
    in_cols = w_in.shape[-1]
    in_pad = -(-(in_cols - rope + LANES) // LANES) * LANES
    n_p, n_s = bsz * seq, db * t_new
    nt = n_p + n_s

    tm_p = _pick_tile(seq, 256)
    tm_s = _pick_tile(n_s, 256)
    tq = _pick_tile(seq, 256)
    tm_o = _pick_tile(math.gcd(n_p, n_s), 256)
    tt = _pick_tile(math.gcd(n_p, n_s), 512)
    td = _pick_tile(nt, 1024)
    tc = _pick_tile(math.gcd(n_p, n_s), 128)
    pps = _pick_tile(page_table.shape[1], 16)
    cache_rope_t = jnp.swapaxes(cache_k_rope, 2, 3)

    cos_p, sin_p = _rope_cs(jnp.arange(seq, dtype=jnp.int32), rope)
    cos_s, sin_s = _rope_cs(past_len + jnp.arange(t_new, dtype=jnp.int32), rope)
    cos_s, sin_s = jnp.tile(cos_s, (tm_s // t_new, 1)), jnp.tile(sin_s, (tm_s // t_new, 1))

    xp, xs_ = x_prompt.reshape(n_p, d), x_sample.reshape(n_s, d)
    outs = [[] for _ in range(6)]
    for l in range(depth):
        win = jnp.pad(w_in[l], ((0, 0), (0, in_pad - in_cols))).astype(BF16)
        wuq3 = w_uq[l].reshape(q_lora, n_heads, nope + rope)
        qlat = _fold_qlat(jnp.transpose(wuq3[:, :, :nope], (1, 0, 2)), jnp.transpose(w_uk[l], (1, 2, 0)), scale)
        wq_rope = jnp.transpose(wuq3[:, :, nope:], (1, 0, 2)) * scale
        wq = jnp.concatenate([qlat, wq_rope, jnp.zeros((n_heads, q_lora, LANES - rope), F32)], axis=2)
        wq = jnp.transpose(wq, (1, 0, 2)).reshape(q_lora, n_heads * 2 * LANES).astype(BF16)
        eye = jnp.eye(n_heads, dtype=F32)
        wuv_bd = jnp.einsum('chv,hg->hcgv', w_uv[l], eye).reshape(n_heads * kv_lora, n_heads * v_dim).astype(BF16)
        wo = w_o[l].astype(BF16)
        wr_t = w_router[l].T
        wr_hi = wr_t.astype(BF16)
        wr_lo = (wr_t - wr_hi.astype(F32)).astype(BF16)
        row = lambda v: v[l].reshape(1, -1)

        common = (win, wq, row(q_norm), row(kv_norm), conv_w[l], row(g_conv))
        q_p, kc_p, ckv_p, kpe_p, u_p, yc_p = _mixer_in(
            xp, *common, cos_p, sin_p, None, prompt=True, seq=seq, tm=tm_p, n_heads=n_heads, rope=rope, q_dtype=BF16)
        st = state_conv[l].astype(F32)
        zero = jnp.zeros((db, conv_ch), F32)
        prev1 = jnp.stack([st[:, 1]] + [zero] * (t_new - 1), axis=1).reshape(n_s, conv_ch)
        prev2 = jnp.stack([st[:, 0], st[:, 1]] + [zero] * (t_new - 2), axis=1).reshape(n_s, conv_ch)
        q_s, kc_s, ckv_s, kpe_s, u_s, yc_s = _mixer_in(
            xs_, *common, cos_s, sin_s, (prev1, prev2), prompt=False, seq=t_new, tm=tm_s, n_heads=n_heads, rope=rope,
            q_dtype=F32)
        del kc_s

        at_p = _prompt_attention(q_p, kc_p, batch=bsz, seq=seq, tq=tq, n_heads=n_heads)
        at_s = _sample_attention(q_s, ckv_s, kpe_s, cache_kv_latent, cache_rope_t, page_table, l,
                                 n_heads=n_heads, t_new=t_new, cpages=pps)

        x1, x1p = _mixer_out(xp, xs_, yc_p, yc_s, at_p, at_s, wuv_bd, row(g_attn), wo, row(ln1_g), row(ln1_b),
                             tm=tm_o, alpha=alpha)

        idx_t, w_t, rank_t, cnt = _router(x1, wr_hi, wr_lo, router_bias[l].reshape(n_exp, 1), tt=tt)
        counts = cnt[:, 0].astype(jnp.int32)
        padded = (counts + SLOT_ROWS - 1) // SLOT_ROWS * SLOT_ROWS
        pad_end = jnp.cumsum(padded)
        pad_start = (pad_end - padded).astype(jnp.int32)
        n_blk = (padded // SLOT_ROWS).astype(jnp.int32)
        n_blocks = -(-(nt * TOP_K) // SLOT_ROWS) + n_exp

        dest_t = _slot_rows(idx_t, rank_t, pad_start, tt=tt)
        xs_sorted = _dispatch(pad_start, n_blk, counts, dest_t, x1p, n_blocks * SLOT_ROWS, td=td, sub=d // (2 * LANES))
        ys = _experts(pad_start, n_blk, xs_sorted, w_gate, w_up, w_down, l)
        xp, xs_ = _combine(dest_t, w_t.T, x1, ws_gate[l].astype(BF16), ws_up[l].astype(BF16),
                           ws_down[l].astype(BF16), row(ln2_g), row(ln2_b), ys, tc=tc, alpha=alpha, n_prompt=n_p)
        outs[0].append(ckv_p.reshape(bsz, seq, kv_lora))
        outs[1].append(kpe_p.reshape(bsz, seq, rope))
        outs[2].append(u_p.reshape(bsz, seq, conv_ch)[:, seq - (CONV_W - 1):])
        outs[3].append(ckv_s.reshape(db, t_new, kv_lora))
        outs[4].append(kpe_s.reshape(db, t_new, rope))
        outs[5].append(u_s.reshape(db, t_new, conv_ch)[:, t_new - (CONV_W - 1):])
    return (xp.reshape(bsz, seq, d), xs_.reshape(db, t_new, d)) + tuple(jnp.stack(o) for o in outs)
```

```python
import functools
import math

import jax
import jax.numpy as jnp
from jax import lax
from jax.experimental import pallas as pl
from jax.experimental.pallas import tpu as pltpu

F32 = jnp.float32
BF16 = jnp.bfloat16
U32 = jnp.uint32

ROPE_BASE = 10000.0
NORM_EPS = 1e-6
LN_EPS = 1e-5
NEG_INF = -1e30
TOP_K = 8
N_GROUPS = 8
TOPK_GROUPS = 4
ROUTED_SCALE = 2.5
CONV_W = 3

LANES = 128
SLOT_ROWS = 256
VMEM_LIMIT = 52 * 1024 * 1024

_NT = (((1,), (1,)), ((), ()))


def _cparams(sem):
    return pltpu.CompilerParams(dimension_semantics=sem, vmem_limit_bytes=VMEM_LIMIT)


def _rms(x, g):
    return x * lax.rsqrt(jnp.mean(x * x, axis=-1, keepdims=True) + NORM_EPS) * g


def _layer_norm(x, g, b):
    mu = jnp.mean(x, axis=-1, keepdims=True)
    xc = x - mu
    var = jnp.mean(xc * xc, axis=-1, keepdims=True)
    return xc * lax.rsqrt(var + LN_EPS) * g + b


def _silu(x):
    return x / (1.0 + jnp.exp(-x))


def _rope_tile(p, c, s, half):
    lane = lax.broadcasted_iota(jnp.int32, p.shape, 1)
    swapped = jnp.where(lane < half, pltpu.roll(p, LANES - half, axis=1), pltpu.roll(p, half, axis=1))
    return p * c + swapped * s


def _pack_rows(x, ref):
    r = x.shape[0]
    half = x.shape[1] // 2
    sub = half // LANES
    bits = lax.bitcast_convert_type(x.astype(BF16).astype(F32), U32)
    words = bits[:, half:] | (bits[:, :half] >> 16)
    for j in range(sub):
        ref[pl.ds(j, r, stride=sub), :] = words[:, j * LANES:(j + 1) * LANES]


def _unpack_rows(ref, first, r, sub):
    lo, hi = [], []
    for j in range(sub):
        w = ref[pl.ds(first * sub + j, r, stride=sub), :]
        lo.append(lax.bitcast_convert_type(w << 16, F32))
        hi.append(lax.bitcast_convert_type(w & jnp.uint32(0xFFFF0000), F32))
    return lo, hi


def _fold_kernel(a_ref, b_ref, o_ref, *, scale):
    o_ref[...] = jnp.dot(a_ref[...], b_ref[...], precision=lax.Precision.HIGHEST,
                         preferred_element_type=F32) * scale


def _fold_qlat(wuq_nope, wuk_t, scale):
    h, r, n = wuq_nope.shape
    c = wuk_t.shape[-1]
    return pl.pallas_call(
        functools.partial(_fold_kernel, scale=scale),
        grid=(h,),
        in_specs=[pl.BlockSpec((None, r, n), lambda i: (i, 0, 0)),
                  pl.BlockSpec((None, n, c), lambda i: (i, 0, 0))],
        out_specs=pl.BlockSpec((None, r, c), lambda i: (i, 0, 0)),
        out_shape=jax.ShapeDtypeStruct((h, r, c), F32),
        compiler_params=_cparams(("parallel",)),
    )(wuq_nope, wuk_t)


def _mixer_in_kernel(*refs, prompt, tiles_per_seq, tm, n_chunks, t_new, n_heads, conv_ch, q_lora, kv_lora, rope):
    if prompt:
        (x_ref, win_ref, wq_ref, qn_ref, kvn_ref, cw_ref, gc_ref, cos_ref, sin_ref,
         q_ref, kc_ref, ckv_ref, kpe_ref, u_ref, yc_ref, carry_ref) = refs
    else:
        (x_ref, win_ref, wq_ref, qn_ref, kvn_ref, cw_ref, gc_ref, cos_ref, sin_ref, p1_ref, p2_ref,
         q_ref, kc_ref, ckv_ref, kpe_ref, u_ref, yc_ref) = refs
    half = rope // 2
    c1, c2, c3 = conv_ch, 2 * conv_ch, 3 * conv_ch
    c4 = c3 + q_lora
    c5 = c4 + kv_lora

    cm = tm // n_chunks
    row = lax.broadcasted_iota(jnp.int32, (cm, 1), 0)
    u_prev = None
    for c in range(n_chunks):
        rows = slice(c * cm, (c + 1) * cm)
        h = jnp.dot(x_ref[rows, :].astype(BF16), win_ref[...], preferred_element_type=F32)
        b_gate, c_gate, x_conv = h[:, :c1], h[:, c1:c2], h[:, c2:c3]
        q_a, c_kv, kp = h[:, c3:c4], h[:, c4:c5], h[:, c5:c5 + LANES]

        u = c_gate * x_conv
        u_ref[rows, :] = u
        if prompt:
            if c == 0:
                first = (pl.program_id(0) % tiles_per_seq) == 0
                prev1 = jnp.where(first, 0.0, carry_ref[7:8, :])
                prev2 = jnp.where(first, 0.0, carry_ref[6:7, :])
            else:
                prev1 = u_prev[cm - 1:cm, :]
                prev2 = u_prev[cm - 2:cm - 1, :]
            t_in = row
            p1 = prev1
            p2 = jnp.where(row == 0, prev2, prev1)
        else:
            t_in = row & (t_new - 1)
            p1 = p1_ref[rows, :]
            p2 = p2_ref[rows, :]
        um1 = jnp.where(t_in == 0, p1, pltpu.roll(u, 1, axis=0))
        um2 = jnp.where(t_in < 2, p2, pltpu.roll(u, 2, axis=0))
        conv_y = cw_ref[0:1, :] * um2 + cw_ref[1:2, :] * um1 + cw_ref[2:3, :] * u
        yc_ref[rows, :] = _rms(b_gate * conv_y, gc_ref[...]).astype(yc_ref.dtype)
        u_prev = u

        cos = cos_ref[rows, :]
        sin = sin_ref[rows, :]
        qn = _rms(q_a, qn_ref[...]).astype(BF16)
        q = jnp.dot(qn, wq_ref[...], preferred_element_type=F32)
        for hd in range(n_heads):
            o = hd * 2 * LANES
            q_ref[rows, o:o + LANES] = q[:, o:o + LANES].astype(q_ref.dtype)
            q_ref[rows, o + LANES:o + 2 * LANES] = _rope_tile(q[:, o + LANES:o + 2 * LANES], cos, sin,
                                                              half).astype(q_ref.dtype)

        ckv = _rms(c_kv, kvn_ref[...])
        ckv_ref[rows, :] = ckv
        kpr = _rope_tile(kp, cos, sin, half)
        kpe_ref[rows, :] = kpr[:, :rope]
        kc_ref[rows, :LANES] = ckv.astype(BF16)
        kc_ref[rows, LANES:] = kpr.astype(BF16)
    if prompt:
        carry_ref[...] = u_prev[cm - 8:, :]


def _mixer_in(x2, win, wq, qn, kvn, cw, gc, cos, sin, prev, *, prompt, seq, tm, n_heads, rope, q_dtype):
    n, d = x2.shape
    conv_ch = cw.shape[-1]
    q_lora = qn.shape[-1]
    kv_lora = kvn.shape[-1]
    assert n % tm == 0 and kv_lora == LANES and tm % 8 == 0
    tiles_per_seq = seq // tm if prompt else 1
    if prompt:
        assert seq % tm == 0
    else:
        assert seq & (seq - 1) == 0 and tm % seq == 0
    const = lambda i: (0, 0)
    tile = lambda i: (i, 0)
    in_specs = [
        pl.BlockSpec((tm, d), tile),
        pl.BlockSpec(win.shape, const),
        pl.BlockSpec(wq.shape, const),
        pl.BlockSpec(qn.shape, const),
        pl.BlockSpec(kvn.shape, const),
        pl.BlockSpec(cw.shape, const),
        pl.BlockSpec(gc.shape, const),
    ]
    args = [x2, win, wq, qn, kvn, cw, gc, cos, sin]
    if prompt:
        in_specs += [pl.BlockSpec((tm, LANES), lambda i: (i % tiles_per_seq, 0))] * 2
        scratch = [pltpu.VMEM((8, conv_ch), F32)]
    else:
        in_specs += [pl.BlockSpec((tm, LANES), const)] * 2
        in_specs += [pl.BlockSpec((tm, conv_ch), tile)] * 2
        args += list(prev)
        scratch = []
    qw = wq.shape[-1]
    out_shape = [
        jax.ShapeDtypeStruct((n, qw), q_dtype),
        jax.ShapeDtypeStruct((n, 2 * LANES), BF16),
        jax.ShapeDtypeStruct((n, kv_lora), F32),
        jax.ShapeDtypeStruct((n, rope), F32),
        jax.ShapeDtypeStruct((n, conv_ch), F32),
        jax.ShapeDtypeStruct((n, conv_ch), BF16),
    ]
    out_specs = [
        pl.BlockSpec((tm, qw), tile),
        pl.BlockSpec((tm, 2 * LANES), tile),
        pl.BlockSpec((tm, kv_lora), tile),
        pl.BlockSpec((tm, rope), tile),
        pl.BlockSpec((tm, conv_ch), tile),
        pl.BlockSpec((tm, conv_ch), tile),
    ]
    return pl.pallas_call(
        functools.partial(_mixer_in_kernel, prompt=prompt, tiles_per_seq=tiles_per_seq, tm=tm,
                          n_chunks=2 if tm % 32 == 0 else 1, t_new=seq,
                          n_heads=n_heads, conv_ch=conv_ch, q_lora=q_lora, kv_lora=kv_lora, rope=rope),
        grid=(n // tm,),
        in_specs=in_specs,
        out_specs=out_specs,
        out_shape=out_shape,
        scratch_shapes=scratch,
        compiler_params=_cparams(("arbitrary",)),
    )(*args)


def _softmax_update(s, v, m_ref, l_ref, acc_ref):
    m_prev = m_ref[...]
    m_new = jnp.maximum(m_prev, jnp.max(s, axis=1, keepdims=True))
    alpha = jnp.exp2(m_prev - m_new)
    p = jnp.exp2(s - m_new)
    l_ref[...] = alpha * l_ref[...] + jnp.sum(p, axis=1, keepdims=True)
    acc_ref[...] = alpha * acc_ref[...] + jnp.dot(p.astype(BF16), v, preferred_element_type=F32)
    m_ref[...] = m_new


def _prompt_attn_kernel(q_ref, k_ref, o_ref, qs_ref, m_ref, l_ref, acc_ref, *, tq, n_heads, chunk_heads):
    i = pl.program_id(1)
    cr = chunk_heads * tq
    n_chunks = n_heads // chunk_heads
    for hd in range(n_heads):
        qs_ref[hd * tq:(hd + 1) * tq, :] = q_ref[:, hd * 2 * LANES:(hd + 1) * 2 * LANES]
    m_ref[...] = jnp.full(m_ref.shape, NEG_INF, F32)
    l_ref[...] = jnp.zeros(l_ref.shape, F32)
    acc_ref[...] = jnp.zeros(acc_ref.shape, F32)

    def step(j, masked):
        k = k_ref[pl.ds(pl.multiple_of(j * tq, tq), tq), :]
        v = k[:, :LANES]
        for c in range(n_chunks):
            rows = slice(c * cr, (c + 1) * cr)
            s = lax.dot_general(qs_ref[rows, :], k, _NT, preferred_element_type=F32)
            if masked:
                t = lax.broadcasted_iota(jnp.int32, s.shape, 0) & (tq - 1)
                col = lax.broadcasted_iota(jnp.int32, s.shape, 1)
                s = jnp.where(col <= t, s, NEG_INF)
            parts = [s[:, w * LANES:(w + 1) * LANES] for w in range(tq // LANES)]
            mc = parts[0]
            for part in parts[1:]:
                mc = jnp.maximum(mc, part)
            m_prev = m_ref[rows, :]
            m_new = jnp.maximum(m_prev, jnp.max(mc, axis=1, keepdims=True))
            alpha = jnp.exp2(m_prev - m_new)
            ps = [jnp.exp2(part - m_new) for part in parts]
            psum = ps[0]
            for pp in ps[1:]:
                psum = psum + pp
            l_ref[rows, :] = alpha * l_ref[rows, :] + psum
            p = jnp.concatenate(ps, axis=1).astype(BF16)
            acc_ref[rows, :] = alpha * acc_ref[rows, :] + jnp.dot(p, v, preferred_element_type=F32)
            m_ref[rows, :] = m_new

    def body(j, carry):
        step(j, False)
        return carry

    lax.fori_loop(0, i, body, 0)
    step(i, True)
    out = acc_ref[...] / jnp.sum(l_ref[...], axis=1, keepdims=True)
    for hd in range(n_heads):
        o_ref[:, hd * LANES:(hd + 1) * LANES] = out[hd * tq:(hd + 1) * tq, :].astype(o_ref.dtype)


def _prompt_attention(q, kc, *, batch, seq, tq, n_heads):
    n = q.shape[0]
    nq = seq // tq
    rows = n_heads * tq
    chunk_heads = 2 if n_heads % 2 == 0 else 1
    assert tq & (tq - 1) == 0 and seq % tq == 0 and tq % LANES == 0
    return pl.pallas_call(
        functools.partial(_prompt_attn_kernel, tq=tq, n_heads=n_heads, chunk_heads=chunk_heads),
        grid=(batch, nq),
        in_specs=[pl.BlockSpec((tq, q.shape[1]), lambda b, i: (b * nq + i, 0)),
                  pl.BlockSpec((seq, kc.shape[1]), lambda b, i: (b, 0))],
        out_specs=pl.BlockSpec((tq, n_heads * LANES), lambda b, i: (b * nq + i, 0)),
        out_shape=jax.ShapeDtypeStruct((n, n_heads * LANES), BF16),
        scratch_shapes=[pltpu.VMEM((rows, 2 * LANES), BF16),
                        pltpu.VMEM((rows, LANES), F32),
                        pltpu.VMEM((rows, LANES), F32),
                        pltpu.VMEM((rows, LANES), F32)],
        compiler_params=_cparams(("parallel", "arbitrary")),
    )(q, kc)


def _sample_attn_kernel(pt_ref, q_ref, ckvn_ref, kpen_ref, kv_hbm, rp_hbm, o_ref, kvbuf, rpbuf, qs_ref, m_ref, l_ref,
                        acc_ref, sem, *, layer, n_pages, cpages, n_heads, t_new, rope, page):
    b = pl.program_id(0)
    nb = pl.num_programs(0)
    slot = b % 2

    def fetch(seq, s):
        def body(p, carry):
            pg = pt_ref[seq, p]
            pltpu.make_async_copy(kv_hbm.at[layer, pg], kvbuf.at[s, p], sem.at[0, s]).start()
            pltpu.make_async_copy(rp_hbm.at[layer, pg], rpbuf.at[s, p], sem.at[1, s]).start()
            return carry
        lax.fori_loop(0, n_pages, body, 0)

    @pl.when(b == 0)
    def _():
        fetch(0, 0)

    @pl.when(b + 1 < nb)
    def _():
        fetch(b + 1, 1 - slot)

    for hd in range(n_heads):
        qs_ref[hd * t_new:(hd + 1) * t_new, :] = q_ref[:, hd * 2 * LANES:(hd + 1) * 2 * LANES]
    m_ref[...] = jnp.full(m_ref.shape, NEG_INF, F32)
    l_ref[...] = jnp.zeros(l_ref.shape, F32)
    acc_ref[...] = jnp.zeros(acc_ref.shape, F32)
    qs = qs_ref[...]
    ql = qs[:, :LANES].astype(BF16)
    qp = qs[:, LANES:LANES + rope].astype(BF16)

    pltpu.make_async_copy(kv_hbm.at[layer, pl.ds(0, n_pages)], kvbuf.at[slot], sem.at[0, slot]).wait()
    pltpu.make_async_copy(rp_hbm.at[layer, pl.ds(0, n_pages)], rpbuf.at[slot], sem.at[1, slot]).wait()
    for c in range(n_pages // cpages):
        kv = kvbuf[slot, c * cpages:(c + 1) * cpages].reshape(cpages * page, LANES).astype(BF16)
        rp = jnp.concatenate([rpbuf[slot, c * cpages + p] for p in range(cpages)], axis=1).astype(BF16)
        s = (lax.dot_general(ql, kv, _NT, preferred_element_type=F32)
             + jnp.dot(qp, rp, preferred_element_type=F32))
        _softmax_update(s, kv, m_ref, l_ref, acc_ref)

    kn = jnp.concatenate([ckvn_ref[...], jnp.zeros((page - t_new, LANES), F32)], axis=0).astype(BF16)
    rn = jnp.concatenate([kpen_ref[...], jnp.zeros((page - t_new, rope), F32)], axis=0).astype(BF16)
    s2 = (lax.dot_general(ql, kn, _NT, preferred_element_type=F32)
          + lax.dot_general(qp, rn, _NT, preferred_element_type=F32))
    t = lax.broadcasted_iota(jnp.int32, s2.shape, 0) & (t_new - 1)
    col = lax.broadcasted_iota(jnp.int32, s2.shape, 1)
    s2 = jnp.where(col <= t, s2, NEG_INF)
    _softmax_update(s2, kn, m_ref, l_ref, acc_ref)
    out = acc_ref[...] / l_ref[...]
    for hd in range(n_heads):
        o_ref[:, hd * LANES:(hd + 1) * LANES] = out[hd * t_new:(hd + 1) * t_new, :]


def _sample_attention(q, ckv_new, kpe_new, cache_kv, cache_rope_t, page_table, layer, *, n_heads, t_new, cpages):
    n = q.shape[0]
    db = n // t_new
    n_pages = page_table.shape[1]
    page = cache_kv.shape[2]
    rope = cache_rope_t.shape[2]
    assert n_pages % cpages == 0 and t_new == 8 and cache_kv.shape[-1] == LANES and cache_rope_t.shape[3] == page
    rows = n_heads * t_new
    per_seq = lambda b, pt: (b, 0)
    return pl.pallas_call(
        functools.partial(_sample_attn_kernel, layer=layer, n_pages=n_pages, cpages=cpages, n_heads=n_heads,
                          t_new=t_new, rope=rope, page=page),
        grid_spec=pltpu.PrefetchScalarGridSpec(
            num_scalar_prefetch=1,
            grid=(db,),
            in_specs=[pl.BlockSpec((t_new, q.shape[1]), per_seq),
                      pl.BlockSpec((t_new, LANES), per_seq),
                      pl.BlockSpec((t_new, rope), per_seq),
                      pl.BlockSpec(memory_space=pl.ANY),
                      pl.BlockSpec(memory_space=pl.ANY)],
            out_specs=pl.BlockSpec((t_new, n_heads * LANES), per_seq),
            scratch_shapes=[pltpu.VMEM((2, n_pages, page, LANES), cache_kv.dtype),
                            pltpu.VMEM((2, n_pages, rope, page), cache_rope_t.dtype),
                            pltpu.VMEM((rows, 2 * LANES), F32),
                            pltpu.VMEM((rows, 1), F32),
                            pltpu.VMEM((rows, 1), F32),
                            pltpu.VMEM((rows, LANES), F32),
                            pltpu.SemaphoreType.DMA((2, 2))]),
        out_shape=jax.ShapeDtypeStruct((n, n_heads * LANES), F32),
        compiler_params=_cparams(("arbitrary",)),
    )(page_table, q, ckv_new, kpe_new, cache_kv, cache_rope_t)


def _mixer_out_kernel(xp_ref, xs_ref, ycp_ref, ycs_ref, atp_ref, ats_ref, wuv_ref, ga_ref, wo_ref, g_ref, b_ref,
                      x1_ref, x1p_ref, *, n_prompt_tiles, conv_ch, alpha):
    is_p = pl.program_id(0) < n_prompt_tiles
    tm = x1_ref.shape[0]
    sub = x1p_ref.shape[0] // tm
    n_chunks = 2 if tm % 32 == 0 else 1
    cm = tm // n_chunks
    for c in range(n_chunks):
        rows = slice(c * cm, (c + 1) * cm)
        x = jnp.where(is_p, xp_ref[rows, :], xs_ref[rows, :])
        yc = jnp.where(is_p, ycp_ref[rows, :], ycs_ref[rows, :])
        at = jnp.where(is_p, atp_ref[rows, :], ats_ref[rows, :].astype(BF16))
        o = jnp.dot(at, wuv_ref[...], preferred_element_type=F32)
        ya = _rms(o, ga_ref[...]).astype(BF16)
        mix = (jnp.dot(yc, wo_ref[:conv_ch, :], preferred_element_type=F32)
               + jnp.dot(ya, wo_ref[conv_ch:, :], preferred_element_type=F32))
        x1 = _layer_norm(alpha * x + mix, g_ref[...], b_ref[...])
        x1_ref[rows, :] = x1
        _pack_rows(x1, x1p_ref.at[pl.ds(c * cm * sub, cm * sub)])


def _mixer_out(xp, xs, ycp, ycs, atp, ats, wuv_bd, ga, wo, g, b, *, tm, alpha):
    np_, d = xp.shape
    ns = xs.shape[0]
    assert np_ % tm == 0 and ns % tm == 0
    npt, nst = np_ // tm, ns // tm
    conv_ch = ycp.shape[1]
    p_map = lambda i: (jnp.minimum(i, npt - 1), 0)
    s_map = lambda i: (jnp.maximum(i - npt, 0), 0)
    const = lambda i: (0, 0)
    tile = lambda i: (i, 0)
    nt = np_ + ns
    sub = d // (2 * LANES)
    return pl.pallas_call(
        functools.partial(_mixer_out_kernel, n_prompt_tiles=npt, conv_ch=conv_ch, alpha=alpha),
        grid=(npt + nst,),
        in_specs=[pl.BlockSpec((tm, d), p_map), pl.BlockSpec((tm, d), s_map),
                  pl.BlockSpec((tm, conv_ch), p_map), pl.BlockSpec((tm, conv_ch), s_map),
                  pl.BlockSpec((tm, atp.shape[1]), p_map), pl.BlockSpec((tm, ats.shape[1]), s_map),
                  pl.BlockSpec(wuv_bd.shape, const), pl.BlockSpec(ga.shape, const),
                  pl.BlockSpec(wo.shape, const), pl.BlockSpec(g.shape, const), pl.BlockSpec(b.shape, const)],
        out_specs=[pl.BlockSpec((tm, d), tile), pl.BlockSpec((tm * sub, LANES), tile)],
        out_shape=[jax.ShapeDtypeStruct((nt, d), F32), jax.ShapeDtypeStruct((nt * sub, LANES), U32)],
        compiler_params=_cparams(("parallel",)),
    )(xp, xs, ycp, ycs, atp, ats, wuv_bd, ga, wo, g, b)


def _router_kernel(x_ref, wh_ref, wl_ref, b_ref, tri_ref, idx_ref, w_ref, rank_ref, cnt_ref, base_ref,
                   *, tt, n_exp):
    @pl.when(pl.program_id(0) == 0)
    def _():
        base_ref[...] = jnp.zeros(base_ref.shape, F32)

    x = x_ref[...]
    xh = x.astype(BF16)
    xl = (x - xh.astype(F32)).astype(BF16)
    wh = wh_ref[...]
    z = (lax.dot_general(wh, xh, _NT, preferred_element_type=F32)
         + lax.dot_general(wh, xl, _NT, preferred_element_type=F32)
         + lax.dot_general(wl_ref[...], xh, _NT, preferred_element_type=F32))
    s = 1.0 / (1.0 + jnp.exp(-z))
    sb = s + b_ref[...]

    gsz = n_exp // N_GROUPS
    git = lax.broadcasted_iota(jnp.int32, (gsz, tt), 0).astype(F32)
    blocks, gscore = [], []
    for g in range(N_GROUPS):
        blk = sb[g * gsz:(g + 1) * gsz, :]
        m1 = jnp.max(blk, axis=0, keepdims=True)
        f1 = jnp.min(jnp.where(blk == m1, git, float(gsz)), axis=0, keepdims=True)
        m2 = jnp.max(jnp.where(git == f1, -jnp.inf, blk), axis=0, keepdims=True)
        blocks.append(blk)
        gscore.append(m1 + m2)
    masked = []
    for g in range(N_GROUPS):
        ahead = jnp.zeros((1, tt), F32)
        for g2 in range(N_GROUPS):
            if g2 == g:
                continue
            beats = (gscore[g2] >= gscore[g]) if g2 < g else (gscore[g2] > gscore[g])
            ahead = ahead + jnp.where(beats, 1.0, 0.0)
        masked.append(jnp.where(ahead < float(TOPK_GROUPS), blocks[g], -jnp.inf))
    vals = jnp.concatenate(masked, axis=0)

    rowi = lax.broadcasted_iota(jnp.int32, (n_exp, tt), 0).astype(F32)
    chosen = jnp.zeros((n_exp, tt), F32)
    picks, wks = [], []
    for k in range(TOP_K):
        m = jnp.max(vals, axis=0, keepdims=True)
        ik = jnp.min(jnp.where(vals == m, rowi, float(n_exp)), axis=0, keepdims=True)
        hit = rowi == ik
        wks.append(jnp.sum(jnp.where(hit, s, 0.0), axis=0, keepdims=True))
        vals = jnp.where(hit, -jnp.inf, vals)
        chosen = chosen + jnp.where(hit, 1.0, 0.0)
        picks.append(ik)
    wsum = wks[0]
    for k in range(1, TOP_K):
        wsum = wsum + wks[k]

    incl = jnp.dot(chosen.astype(BF16), tri_ref[...], preferred_element_type=F32)
    rnk = base_ref[...] + (incl - chosen)
    for k in range(TOP_K):
        idx_ref[k:k + 1, :] = picks[k].astype(jnp.int32)
        w_ref[k:k + 1, :] = wks[k] / wsum * ROUTED_SCALE
        rk = jnp.sum(jnp.where(rowi == picks[k], rnk, 0.0), axis=0, keepdims=True)
        rank_ref[k:k + 1, :] = rk.astype(jnp.int32)
    base = base_ref[...] + jnp.sum(chosen, axis=1, keepdims=True)
    base_ref[...] = base
    cnt_ref[...] = jnp.broadcast_to(base, cnt_ref.shape)


def _router(x1, wr_hi, wr_lo, bias, *, tt):
    nt, d = x1.shape
    n_exp = wr_hi.shape[0]
    assert nt % tt == 0
    tri = jnp.triu(jnp.ones((tt, tt), BF16))
    const = lambda i: (0, 0)
    col = lambda i: (0, i)
    return pl.pallas_call(
        functools.partial(_router_kernel, tt=tt, n_exp=n_exp),
        grid=(nt // tt,),
        in_specs=[pl.BlockSpec((tt, d), lambda i: (i, 0)),
                  pl.BlockSpec(wr_hi.shape, const), pl.BlockSpec(wr_lo.shape, const),
                  pl.BlockSpec(bias.shape, const), pl.BlockSpec(tri.shape, const)],
        out_specs=[pl.BlockSpec((TOP_K, tt), col), pl.BlockSpec((TOP_K, tt), col),
                   pl.BlockSpec((TOP_K, tt), col), pl.BlockSpec((n_exp, LANES), const)],
        out_shape=[jax.ShapeDtypeStruct((TOP_K, nt), jnp.int32), jax.ShapeDtypeStruct((TOP_K, nt), F32),
                   jax.ShapeDtypeStruct((TOP_K, nt), jnp.int32), jax.ShapeDtypeStruct((n_exp, LANES), F32)],
        scratch_shapes=[pltpu.VMEM((n_exp, 1), F32)],
        compiler_params=_cparams(("arbitrary",)),
    )(x1, wr_hi, wr_lo, bias, tri)


def _slot_rows_kernel(idx_ref, rank_ref, ps_ref, o_ref, *, n_exp):
    tt = idx_ref.shape[1]
    rowi = lax.broadcasted_iota(jnp.int32, (n_exp, tt), 0)
    ps = ps_ref[...]
    for k in range(TOP_K):
        start = jnp.sum(jnp.where(rowi == idx_ref[k:k + 1, :], ps, 0.0), axis=0, keepdims=True)
        o_ref[k:k + 1, :] = start.astype(jnp.int32) + rank_ref[k:k + 1, :]


def _slot_rows(idx_t, rank_t, pad_start, *, tt):
    nt = idx_t.shape[1]
    n_exp = pad_start.shape[0]
    assert nt % tt == 0 and n_exp * SLOT_ROWS + nt * TOP_K < 2 ** 24
    col = lambda i: (0, i)
    return pl.pallas_call(
        functools.partial(_slot_rows_kernel, n_exp=n_exp),
        grid=(nt // tt,),
        in_specs=[pl.BlockSpec((TOP_K, tt), col), pl.BlockSpec((TOP_K, tt), col),
                  pl.BlockSpec((n_exp, 1), lambda i: (0, 0))],
        out_specs=pl.BlockSpec((TOP_K, tt), col),
        out_shape=jax.ShapeDtypeStruct((TOP_K, nt), jnp.int32),
        compiler_params=_cparams(("parallel",)),
    )(idx_t, rank_t, pad_start.astype(F32).reshape(n_exp, 1))


def _dispatch_kernel(ps_ref, nb_ref, cnt_ref, dest_ref, x_ref, o_ref, zbuf, sem, zsem, *, td, sub, n_exp):
    i = pl.program_id(0)
    blk = SLOT_ROWS * sub

    @pl.when(i == 0)
    def _():
        zbuf[...] = jnp.zeros(zbuf.shape, zbuf.dtype)
        n_blocks = o_ref.shape[0] // blk
        used = ps_ref[n_exp - 1] // SLOT_ROWS + nb_ref[n_exp - 1]

        def zero_block(b):
            return pltpu.make_async_copy(zbuf, o_ref.at[pl.ds(pl.multiple_of(b * blk, blk), blk)], zsem)

        def partial(e):
            return (cnt_ref[e] & (SLOT_ROWS - 1)) != 0

        def fill(e, carry):
            @pl.when(partial(e))
            def _():
                zero_block(ps_ref[e] // SLOT_ROWS + nb_ref[e] - 1).start()
            return carry

        def fill_done(e, carry):
            @pl.when(partial(e))
            def _():
                zero_block(0).wait()
            return carry

        def tail(b, carry):
            zero_block(b).start()
            return carry

        def tail_done(b, carry):
            zero_block(0).wait()
            return carry

        lax.fori_loop(0, n_exp, fill, 0)
        lax.fori_loop(used, n_blocks, tail, 0)
        lax.fori_loop(0, n_exp, fill_done, 0)
        lax.fori_loop(used, n_blocks, tail_done, 0)

    def per_token(t, carry):
        src = x_ref.at[pl.ds(pl.multiple_of(t * sub, sub), sub)]
        for k in range(TOP_K):
            dst = pl.multiple_of(dest_ref[k, t] * sub, sub)
            pltpu.make_async_copy(src, o_ref.at[pl.ds(dst, sub)], sem).start(priority=k % 2)
        return carry

    lax.fori_loop(0, td, per_token, 0)
    for _ in range(TOP_K):
        pltpu.make_async_copy(x_ref, o_ref.at[pl.ds(0, td * sub)], sem).wait()


def _dispatch(pad_start, n_blk, counts, dest_t, x1p, n_slots, *, td, sub):
    nt = x1p.shape[0] // sub
    assert nt % td == 0 and n_slots >= td and n_slots % SLOT_ROWS == 0
    return pl.pallas_call(
        functools.partial(_dispatch_kernel, td=td, sub=sub, n_exp=counts.shape[0]),
        grid_spec=pltpu.PrefetchScalarGridSpec(
            num_scalar_prefetch=3,
            grid=(nt // td,),
            in_specs=[pl.BlockSpec((TOP_K, td), lambda i, *_: (0, i), memory_space=pltpu.SMEM),
                      pl.BlockSpec((td * sub, LANES), lambda i, *_: (i, 0))],
            out_specs=pl.BlockSpec(memory_space=pl.ANY),
            scratch_shapes=[pltpu.VMEM((SLOT_ROWS * sub, LANES), x1p.dtype), pltpu.SemaphoreType.DMA(()),
                            pltpu.SemaphoreType.DMA(())]),
        out_shape=jax.ShapeDtypeStruct((n_slots * sub, LANES), x1p.dtype),
        compiler_params=_cparams(("arbitrary",)),
    )(pad_start, n_blk, counts, dest_t, x1p)


EXPERT_RING = 4


def _experts_kernel(ps_ref, nb_ref, wg_ref, wu_ref, wd_ref, xs_hbm, ys_hbm, wg_s, wu_s, wd_s, xbuf, ybuf,
                    xsem, ysem, zsem, *, sub, n_exp):
    e = pl.program_id(0)
    blk = SLOT_ROWS * sub
    n_blocks = xs_hbm.shape[0] // blk
    used = ps_ref[n_exp - 1] // SLOT_ROWS + nb_ref[n_exp - 1]

    def rows(g):
        return pl.ds(pl.multiple_of(g * blk, blk), blk)

    def x_copy(g, s):
        return pltpu.make_async_copy(xs_hbm.at[rows(g)], xbuf.at[s], xsem.at[s])

    def y_copy(g, s):
        return pltpu.make_async_copy(ybuf.at[s], ys_hbm.at[rows(g)], ysem.at[s])

    @pl.when(e == 0)
    def _():
        zs = EXPERT_RING - 1
        ybuf[zs] = jnp.zeros(ybuf.shape[1:], ybuf.dtype)

        def fill(b, carry):
            pltpu.make_async_copy(ybuf.at[zs], ys_hbm.at[rows(b)], zsem).start()
            return carry

        def fill_done(b, carry):
            pltpu.make_async_copy(ybuf.at[zs], ys_hbm.at[rows(0)], zsem).wait()
            return carry

        lax.fori_loop(used, n_blocks, fill, 0)
        lax.fori_loop(used, n_blocks, fill_done, 0)
        for s in range(EXPERT_RING):
            @pl.when(s < used)
            def _():
                x_copy(s, s).start()

    nb = nb_ref[e]
    first = ps_ref[e] // SLOT_ROWS

    @pl.when(nb > 0)
    def _():
        wg_s[...] = wg_ref[...].astype(BF16)
        wu_s[...] = wu_ref[...].astype(BF16)
        wd_s[...] = wd_ref[...].astype(BF16)

        def body(j, carry):
            g = first + j
            s = g & (EXPERT_RING - 1)
            x_copy(g, s).wait()

            @pl.when(g >= EXPERT_RING)
            def _():
                y_copy(g, s).wait()

            lo, hi = _unpack_rows(xbuf.at[s], 0, SLOT_ROWS, sub)
            x = jnp.concatenate(lo + hi, axis=1).astype(BF16)
            gate = jnp.dot(x, wg_s[...], preferred_element_type=F32)
            up = jnp.dot(x, wu_s[...], preferred_element_type=F32)
            hmid = (_silu(gate) * up).astype(BF16)
            _pack_rows(jnp.dot(hmid, wd_s[...], preferred_element_type=F32), ybuf.at[s])
            y_copy(g, s).start()

            @pl.when(g + EXPERT_RING < used)
            def _():
                x_copy(g + EXPERT_RING, s).start()
            return carry

        lax.fori_loop(0, nb, body, 0)

    @pl.when(e == n_exp - 1)
    def _():
        for s in range(EXPERT_RING):
            @pl.when(s < used)
            def _():
                y_copy(0, s).wait()


def _experts(pad_start, n_blk, xs, w_gate, w_up, w_down, layer):
    n_exp, d, de = w_gate.shape[-3:]
    sub = d // (2 * LANES)
    blk = SLOT_ROWS * sub
    assert xs.shape[0] % blk == 0 and xs.shape[1] == LANES
    return pl.pallas_call(
        functools.partial(_experts_kernel, sub=sub, n_exp=n_exp),
        grid_spec=pltpu.PrefetchScalarGridSpec(
            num_scalar_prefetch=2,
            grid=(n_exp,),
            in_specs=[pl.BlockSpec((None, None, d, de), lambda e, ps, nb: (layer, e, 0, 0)),
                      pl.BlockSpec((None, None, d, de), lambda e, ps, nb: (layer, e, 0, 0)),
                      pl.BlockSpec((None, None, de, d), lambda e, ps, nb: (layer, e, 0, 0)),
                      pl.BlockSpec(memory_space=pl.ANY)],
            out_specs=pl.BlockSpec(memory_space=pl.ANY),
            scratch_shapes=[pltpu.VMEM((d, de), BF16), pltpu.VMEM((d, de), BF16), pltpu.VMEM((de, d), BF16),
                            pltpu.VMEM((EXPERT_RING, blk, LANES), U32), pltpu.VMEM((EXPERT_RING, blk, LANES), U32),
                            pltpu.SemaphoreType.DMA((EXPERT_RING,)), pltpu.SemaphoreType.DMA((EXPERT_RING,)),
                            pltpu.SemaphoreType.DMA(())]),
        out_shape=jax.ShapeDtypeStruct(xs.shape, U32),
        compiler_params=_cparams(("arbitrary",)),
    )(pad_start, n_blk, w_gate, w_up, w_down, xs)


def _combine_kernel(dest0_ref, dest1_ref, w_ref, x1_ref, wsg_ref, wsu_ref,
                    wsd_ref, g_ref, b_ref, ys_ref, yp_ref, ysm_ref, buf, sem, *, tc, alpha, sub, n_prompt_tiles):
    i = pl.program_id(0)
    n = pl.num_programs(0)
    slot = i % 2

    def gather_row(dest_ref, s, k, t):
        src = pl.multiple_of(dest_ref[k, t] * sub, sub)
        dst = (k * tc + t) * sub
        if not isinstance(dst, int):
            dst = pl.multiple_of(dst, sub)
        pltpu.make_async_copy(ys_ref.at[pl.ds(src, sub)], buf.at[s, pl.ds(dst, sub)], sem.at[s]).start(priority=k % 2)

    @pl.when(i == 0)
    def _():
        def per_token(t, carry):
            for k in range(TOP_K):
                gather_row(dest0_ref, 0, k, t)
            return carry
        lax.fori_loop(0, tc, per_token, 0)

    for t in range(tc):
        for k in range(TOP_K):
            gather_row(dest1_ref, 1 - slot, k, t)

    x1 = x1_ref[...]
    xb = x1.astype(BF16)
    hs = (_silu(jnp.dot(xb, wsg_ref[...], preferred_element_type=F32))
          * jnp.dot(xb, wsu_ref[...], preferred_element_type=F32)).astype(BF16)
    shared = jnp.dot(hs, wsd_ref[...], preferred_element_type=F32)

    def drain(s):
        pltpu.make_async_copy(ys_ref.at[pl.ds(0, TOP_K * tc * sub)], buf.at[s], sem.at[s]).wait()

    drain(slot)
    w = w_ref[...]
    lo_acc = [jnp.zeros((tc, LANES), F32) for _ in range(sub)]
    hi_acc = [jnp.zeros((tc, LANES), F32) for _ in range(sub)]
    for k in range(TOP_K):
        lo, hi = _unpack_rows(buf.at[slot], k * tc, tc, sub)
        wk = w[:, k:k + 1]
        for j in range(sub):
            lo_acc[j] = lo_acc[j] + wk * lo[j]
            hi_acc[j] = hi_acc[j] + wk * hi[j]
    moe = jnp.concatenate(lo_acc + hi_acc, axis=1) + shared
    y = _layer_norm(alpha * x1 + moe, g_ref[...], b_ref[...])

    @pl.when(i < n_prompt_tiles)
    def _():
        yp_ref[...] = y

    @pl.when(i >= n_prompt_tiles)
    def _():
        ysm_ref[...] = y

    @pl.when(i == n - 1)
    def _():
        drain(1 - slot)


def _combine(dest_t, w_tok, x1, wsg, wsu, wsd, g, b, ys, *, tc, alpha, n_prompt):
    nt, d = x1.shape
    n_tiles = nt // tc
    sub = d // (2 * LANES)
    assert nt % tc == 0 and n_prompt % tc == 0 and 0 < n_prompt < nt and ys.shape[0] >= TOP_K * tc * sub
    npt = n_prompt // tc
    cur = pl.BlockSpec((TOP_K, tc), lambda i: (0, i), memory_space=pltpu.SMEM)
    nxt = pl.BlockSpec((TOP_K, tc), lambda i: (0, jnp.minimum(i + 1, n_tiles - 1)), memory_space=pltpu.SMEM)
    const = lambda i: (0, 0)
    tile = lambda i: (i, 0)
    return pl.pallas_call(
        functools.partial(_combine_kernel, tc=tc, alpha=alpha, sub=sub, n_prompt_tiles=npt),
        grid=(n_tiles,),
        in_specs=[cur, nxt,
                  pl.BlockSpec((tc, TOP_K), tile),
                  pl.BlockSpec((tc, d), tile),
                  pl.BlockSpec(wsg.shape, const), pl.BlockSpec(wsu.shape, const), pl.BlockSpec(wsd.shape, const),
                  pl.BlockSpec(g.shape, const), pl.BlockSpec(b.shape, const),
                  pl.BlockSpec(memory_space=pl.ANY)],
        out_specs=[pl.BlockSpec((tc, d), lambda i: (jnp.minimum(i, npt - 1), 0)),
                   pl.BlockSpec((tc, d), lambda i: (jnp.maximum(i - npt, 0), 0))],
        out_shape=[jax.ShapeDtypeStruct((n_prompt, d), F32), jax.ShapeDtypeStruct((nt - n_prompt, d), F32)],
        scratch_shapes=[pltpu.VMEM((2, TOP_K * tc * sub, LANES), ys.dtype), pltpu.SemaphoreType.DMA((2,))],
        compiler_params=_cparams(("arbitrary",)),
    )(dest_t, dest_t, w_tok, x1, wsg, wsu, wsd, g, b, ys)


def _rope_cs(pos, rope):
    half = rope // 2
    inv = ROPE_BASE ** (-jnp.arange(half, dtype=F32) / half)
    ang = pos.astype(F32)[:, None] * inv[None, :]
    cos, sin = jnp.cos(ang), jnp.sin(ang)
    pad = jnp.zeros((pos.shape[0], LANES - rope), F32)
    return (jnp.concatenate([cos, cos, pad], axis=1), jnp.concatenate([-sin, sin, pad], axis=1))


def _pick_tile(n, pref):
    t = pref
    while n % t:
        t //= 2
    return t


def kernel(x_prompt, x_sample, cache_kv_latent, cache_k_rope, state_conv, page_table, w_in, conv_w, q_norm, w_uq, kv_norm, w_uk, w_uv, g_conv, g_attn, w_o, ln1_g, ln1_b, w_router, router_bias, w_gate, w_up, w_down, ws_gate, ws_up, ws_down, ln2_g, ln2_b):
    depth = w_in.shape[0]
    bsz, seq, d = x_prompt.shape
    db, t_new, _ = x_sample.shape
    kv_lora, n_heads, nope = w_uk.shape[1:]
    v_dim = w_uv.shape[-1]
    rope = cache_k_rope.shape[-1]
    q_lora = q_norm.shape[-1]
    conv_ch = conv_w.shape[-1]
    n_exp = w_router.shape[-1]
    page = cache_kv_latent.shape[2]
    past_len = page_table.shape[1] * page
    alpha = (2.0 * depth) ** 0.25
    scale = float((nope + rope) ** -0.5) * math.log2(math.e)
    in_cols = w_in.shape[-1]
    in_pad = -(-(in_cols - rope + LANES) // LANES) * LANES
    n_p, n_s = bsz * seq, db * t_new
    nt = n_p + n_s

    tm_p = _pick_tile(seq, 512)
    tm_s = _pick_tile(n_s, 512)
    tq = _pick_tile(seq, 256)
    tm_o = _pick_tile(math.gcd(n_p, n_s), 512)
    tt = _pick_tile(math.gcd(n_p, n_s), 256)
    td = _pick_tile(nt, 1024)
    tc = _pick_tile(math.gcd(n_p, n_s), 128)
    pps = _pick_tile(page_table.shape[1], 16)
    cache_rope_t = jnp.swapaxes(cache_k_rope, 2, 3)

    cos_p, sin_p = _rope_cs(jnp.arange(seq, dtype=jnp.int32), rope)
    cos_s, sin_s = _rope_cs(past_len + jnp.arange(t_new, dtype=jnp.int32), rope)
    cos_s, sin_s = jnp.tile(cos_s, (tm_s // t_new, 1)), jnp.tile(sin_s, (tm_s // t_new, 1))

    xp, xs_ = x_prompt.reshape(n_p, d), x_sample.reshape(n_s, d)
    outs = [[] for _ in range(6)]
    for l in range(depth):
        win = jnp.pad(w_in[l], ((0, 0), (0, in_pad - in_cols))).astype(BF16)
        wuq3 = w_uq[l].reshape(q_lora, n_heads, nope + rope)
        qlat = _fold_qlat(jnp.transpose(wuq3[:, :, :nope], (1, 0, 2)), jnp.transpose(w_uk[l], (1, 2, 0)), scale)
        wq_rope = jnp.transpose(wuq3[:, :, nope:], (1, 0, 2)) * scale
        wq = jnp.concatenate([qlat, wq_rope, jnp.zeros((n_heads, q_lora, LANES - rope), F32)], axis=2)
        wq = jnp.transpose(wq, (1, 0, 2)).reshape(q_lora, n_heads * 2 * LANES).astype(BF16)
        eye = jnp.eye(n_heads, dtype=F32)
        wuv_bd = jnp.einsum('chv,hg->hcgv', w_uv[l], eye).reshape(n_heads * kv_lora, n_heads * v_dim).astype(BF16)
        wo = w_o[l].astype(BF16)
        wr_t = w_router[l].T
        wr_hi = wr_t.astype(BF16)
        wr_lo = (wr_t - wr_hi.astype(F32)).astype(BF16)
        row = lambda v: v[l].reshape(1, -1)

        common = (win, wq, row(q_norm), row(kv_norm), conv_w[l], row(g_conv))
        q_p, kc_p, ckv_p, kpe_p, u_p, yc_p = _mixer_in(
            xp, *common, cos_p, sin_p, None, prompt=True, seq=seq, tm=tm_p, n_heads=n_heads, rope=rope, q_dtype=BF16)
        st = state_conv[l].astype(F32)
        zero = jnp.zeros((db, conv_ch), F32)
        prev1 = jnp.stack([st[:, 1]] + [zero] * (t_new - 1), axis=1).reshape(n_s, conv_ch)
        prev2 = jnp.stack([st[:, 0], st[:, 1]] + [zero] * (t_new - 2), axis=1).reshape(n_s, conv_ch)
        q_s, kc_s, ckv_s, kpe_s, u_s, yc_s = _mixer_in(
            xs_, *common, cos_s, sin_s, (prev1, prev2), prompt=False, seq=t_new, tm=tm_s, n_heads=n_heads, rope=rope,
            q_dtype=F32)
        del kc_s

        at_p = _prompt_attention(q_p, kc_p, batch=bsz, seq=seq, tq=tq, n_heads=n_heads)
        at_s = _sample_attention(q_s, ckv_s, kpe_s, cache_kv_latent, cache_rope_t, page_table, l,
                                 n_heads=n_heads, t_new=t_new, cpages=pps)

        x1, x1p = _mixer_out(xp, xs_, yc_p, yc_s, at_p, at_s, wuv_bd, row(g_attn), wo, row(ln1_g), row(ln1_b),
                             tm=tm_o, alpha=alpha)

        idx_t, w_t, rank_t, cnt = _router(x1, wr_hi, wr_lo, router_bias[l].reshape(n_exp, 1), tt=tt)
        counts = cnt[:, 0].astype(jnp.int32)
        padded = (counts + SLOT_ROWS - 1) // SLOT_ROWS * SLOT_ROWS
        pad_end = jnp.cumsum(padded)
        pad_start = (pad_end - padded).astype(jnp.int32)
        n_blk = (padded // SLOT_ROWS).astype(jnp.int32)
        n_blocks = -(-(nt * TOP_K) // SLOT_ROWS) + n_exp

        dest_t = _slot_rows(idx_t, rank_t, pad_start, tt=tt)
        xs_sorted = _dispatch(pad_start, n_blk, counts, dest_t, x1p, n_blocks * SLOT_ROWS, td=td, sub=d // (2 * LANES))
        ys = _experts(pad_start, n_blk, xs_sorted, w_gate, w_up, w_down, l)
        xp, xs_ = _combine(dest_t, w_t.T, x1, ws_gate[l].astype(BF16), ws_up[l].astype(BF16),
                           ws_down[l].astype(BF16), row(ln2_g), row(ln2_b), ys, tc=tc, alpha=alpha, n_prompt=n_p)
        outs[0].append(ckv_p.reshape(bsz, seq, kv_lora))
        outs[1].append(kpe_p.reshape(bsz, seq, rope))
        outs[2].append(u_p.reshape(bsz, seq, conv_ch)[:, seq - (CONV_W - 1):])
        outs[3].append(ckv_s.reshape(db, t_new, kv_lora))
        outs[4].append(kpe_s.reshape(db, t_new, rope))
        outs[5].append(u_s.reshape(db, t_new, conv_ch)[:, t_new - (CONV_W - 1):])
    return (xp.reshape(bsz, seq, d), xs_.reshape(db, t_new, d)) + tuple(jnp.stack(o) for o in outs)
```

```python
import functools
import math

import jax
import jax.numpy as jnp
from jax import lax
from jax.experimental import pallas as pl
from jax.experimental.pallas import tpu as pltpu

F32 = jnp.float32
BF16 = jnp.bfloat16
U32 = jnp.uint32

ROPE_BASE = 10000.0
NORM_EPS = 1e-6
LN_EPS = 1e-5
NEG_INF = -1e30
TOP_K = 8
N_GROUPS = 8
TOPK_GROUPS = 4
ROUTED_SCALE = 2.5
CONV_W = 3

LANES = 128
SLOT_ROWS = 256
VMEM_LIMIT = 52 * 1024 * 1024

_NT = (((1,), (1,)), ((), ()))


def _cparams(sem):
    return pltpu.CompilerParams(dimension_semantics=sem, vmem_limit_bytes=VMEM_LIMIT)


def _rms(x, g):
    return x * lax.rsqrt(jnp.mean(x * x, axis=-1, keepdims=True) + NORM_EPS) * g


def _layer_norm(x, g, b):
    mu = jnp.mean(x, axis=-1, keepdims=True)
    xc = x - mu
    var = jnp.mean(xc * xc, axis=-1, keepdims=True)
    return xc * lax.rsqrt(var + LN_EPS) * g + b


def _silu(x):
    return x / (1.0 + jnp.exp(-x))


def _rope_tile(p, c, s, half):
    lane = lax.broadcasted_iota(jnp.int32, p.shape, 1)
    swapped = jnp.where(lane < half, pltpu.roll(p, LANES - half, axis=1), pltpu.roll(p, half, axis=1))
    return p * c + swapped * s


def _pack_rows(x, ref):
    r = x.shape[0]
    half = x.shape[1] // 2
    sub = half // LANES
    bits = lax.bitcast_convert_type(x.astype(BF16).astype(F32), U32)
    words = bits[:, half:] | (bits[:, :half] >> 16)
    for j in range(sub):
        ref[pl.ds(j, r, stride=sub), :] = words[:, j * LANES:(j + 1) * LANES]


def _unpack_rows(ref, first, r, sub):
    lo, hi = [], []
    for j in range(sub):
        w = ref[pl.ds(first * sub + j, r, stride=sub), :]
        lo.append(lax.bitcast_convert_type(w << 16, F32))
        hi.append(lax.bitcast_convert_type(w & jnp.uint32(0xFFFF0000), F32))
    return lo, hi


def _fold_kernel(a_ref, b_ref, o_ref, *, scale):
    o_ref[...] = jnp.dot(a_ref[...], b_ref[...], precision=lax.Precision.HIGHEST,
                         preferred_element_type=F32) * scale


def _fold_qlat(wuq_nope, wuk_t, scale):
    h, r, n = wuq_nope.shape
    c = wuk_t.shape[-1]
    return pl.pallas_call(
        functools.partial(_fold_kernel, scale=scale),
        grid=(h,),
        in_specs=[pl.BlockSpec((None, r, n), lambda i: (i, 0, 0)),
                  pl.BlockSpec((None, n, c), lambda i: (i, 0, 0))],
        out_specs=pl.BlockSpec((None, r, c), lambda i: (i, 0, 0)),
        out_shape=jax.ShapeDtypeStruct((h, r, c), F32),
        compiler_params=_cparams(("parallel",)),
    )(wuq_nope, wuk_t)


def _mixer_in_kernel(*refs, prompt, tiles_per_seq, tm, n_chunks, t_new, n_heads, conv_ch, q_lora, kv_lora, rope):
    if prompt:
        (x_ref, win_ref, wq_ref, qn_ref, kvn_ref, cw_ref, gc_ref, cos_ref, sin_ref,
         q_ref, kc_ref, ckv_ref, kpe_ref, u_ref, yc_ref, carry_ref) = refs
    else:
        (x_ref, win_ref, wq_ref, qn_ref, kvn_ref, cw_ref, gc_ref, cos_ref, sin_ref, p1_ref, p2_ref,
         q_ref, kc_ref, ckv_ref, kpe_ref, u_ref, yc_ref) = refs
    half = rope // 2
    c1, c2, c3 = conv_ch, 2 * conv_ch, 3 * conv_ch
    c4 = c3 + q_lora
    c5 = c4 + kv_lora

    cm = tm // n_chunks
    row = lax.broadcasted_iota(jnp.int32, (cm, 1), 0)
    u_prev = None
    for c in range(n_chunks):
        rows = slice(c * cm, (c + 1) * cm)
        h = jnp.dot(x_ref[rows, :].astype(BF16), win_ref[...], preferred_element_type=F32)
        b_gate, c_gate, x_conv = h[:, :c1], h[:, c1:c2], h[:, c2:c3]
        q_a, c_kv, kp = h[:, c3:c4], h[:, c4:c5], h[:, c5:c5 + LANES]

        u = c_gate * x_conv
        u_ref[rows, :] = u
        if prompt:
            if c == 0:
                first = (pl.program_id(0) % tiles_per_seq) == 0
                prev1 = jnp.where(first, 0.0, carry_ref[7:8, :])
                prev2 = jnp.where(first, 0.0, carry_ref[6:7, :])
            else:
                prev1 = u_prev[cm - 1:cm, :]
                prev2 = u_prev[cm - 2:cm - 1, :]
            t_in = row
            p1 = prev1
            p2 = jnp.where(row == 0, prev2, prev1)
        else:
            t_in = row & (t_new - 1)
            p1 = p1_ref[rows, :]
            p2 = p2_ref[rows, :]
        um1 = jnp.where(t_in == 0, p1, pltpu.roll(u, 1, axis=0))
        um2 = jnp.where(t_in < 2, p2, pltpu.roll(u, 2, axis=0))
        conv_y = cw_ref[0:1, :] * um2 + cw_ref[1:2, :] * um1 + cw_ref[2:3, :] * u
        yc_ref[rows, :] = _rms(b_gate * conv_y, gc_ref[...]).astype(yc_ref.dtype)
        u_prev = u

        cos = cos_ref[rows, :]
        sin = sin_ref[rows, :]
        qn = _rms(q_a, qn_ref[...]).astype(BF16)
        q = jnp.dot(qn, wq_ref[...], preferred_element_type=F32)
        for hd in range(n_heads):
            o = hd * 2 * LANES
            q_ref[rows, o:o + LANES] = q[:, o:o + LANES].astype(q_ref.dtype)
            q_ref[rows, o + LANES:o + 2 * LANES] = _rope_tile(q[:, o + LANES:o + 2 * LANES], cos, sin,
                                                              half).astype(q_ref.dtype)

        ckv = _rms(c_kv, kvn_ref[...])
        ckv_ref[rows, :] = ckv
        kpr = _rope_tile(kp, cos, sin, half)
        kpe_ref[rows, :] = kpr[:, :rope]
        kc_ref[rows, :LANES] = ckv.astype(BF16)
        kc_ref[rows, LANES:] = kpr.astype(BF16)
    if prompt:
        carry_ref[...] = u_prev[cm - 8:, :]


def _mixer_in(x2, win, wq, qn, kvn, cw, gc, cos, sin, prev, *, prompt, seq, tm, n_heads, rope, q_dtype):
    n, d = x2.shape
    conv_ch = cw.shape[-1]
    q_lora = qn.shape[-1]
    kv_lora = kvn.shape[-1]
    assert n % tm == 0 and kv_lora == LANES and tm % 8 == 0
    tiles_per_seq = seq // tm if prompt else 1
    if prompt:
        assert seq % tm == 0
    else:
        assert seq & (seq - 1) == 0 and tm % seq == 0
    const = lambda i: (0, 0)
    tile = lambda i: (i, 0)
    in_specs = [
        pl.BlockSpec((tm, d), tile),
        pl.BlockSpec(win.shape, const),
        pl.BlockSpec(wq.shape, const),
        pl.BlockSpec(qn.shape, const),
        pl.BlockSpec(kvn.shape, const),
        pl.BlockSpec(cw.shape, const),
        pl.BlockSpec(gc.shape, const),
    ]
    args = [x2, win, wq, qn, kvn, cw, gc, cos, sin]
    if prompt:
        in_specs += [pl.BlockSpec((tm, LANES), lambda i: (i % tiles_per_seq, 0))] * 2
        scratch = [pltpu.VMEM((8, conv_ch), F32)]
    else:
        in_specs += [pl.BlockSpec((tm, LANES), const)] * 2
        in_specs += [pl.BlockSpec((tm, conv_ch), tile)] * 2
        args += list(prev)
        scratch = []
    qw = wq.shape[-1]
    out_shape = [
        jax.ShapeDtypeStruct((n, qw), q_dtype),
        jax.ShapeDtypeStruct((n, 2 * LANES), BF16),
        jax.ShapeDtypeStruct((n, kv_lora), F32),
        jax.ShapeDtypeStruct((n, rope), F32),
        jax.ShapeDtypeStruct((n, conv_ch), F32),
        jax.ShapeDtypeStruct((n, conv_ch), BF16),
    ]
    out_specs = [
        pl.BlockSpec((tm, qw), tile),
        pl.BlockSpec((tm, 2 * LANES), tile),
        pl.BlockSpec((tm, kv_lora), tile),
        pl.BlockSpec((tm, rope), tile),
        pl.BlockSpec((tm, conv_ch), tile),
        pl.BlockSpec((tm, conv_ch), tile),
    ]
    return pl.pallas_call(
        functools.partial(_mixer_in_kernel, prompt=prompt, tiles_per_seq=tiles_per_seq, tm=tm,
                          n_chunks=2 if tm % 32 == 0 else 1, t_new=seq,
                          n_heads=n_heads, conv_ch=conv_ch, q_lora=q_lora, kv_lora=kv_lora, rope=rope),
        grid=(n // tm,),
        in_specs=in_specs,
        out_specs=out_specs,
        out_shape=out_shape,
        scratch_shapes=scratch,
        compiler_params=_cparams(("arbitrary",)),
    )(*args)


def _softmax_update(s, v, m_ref, l_ref, acc_ref):
    m_prev = m_ref[...]
    m_new = jnp.maximum(m_prev, jnp.max(s, axis=1, keepdims=True))
    alpha = jnp.exp2(m_prev - m_new)
    p = jnp.exp2(s - m_new)
    l_ref[...] = alpha * l_ref[...] + jnp.sum(p, axis=1, keepdims=True)
    acc_ref[...] = alpha * acc_ref[...] + jnp.dot(p.astype(BF16), v, preferred_element_type=F32)
    m_ref[...] = m_new


def _prompt_attn_kernel(q_ref, k_ref, o_ref, qs_ref, m_ref, l_ref, acc_ref, *, tq, n_heads, chunk_heads):
    i = pl.program_id(1)
    cr = chunk_heads * tq
    n_chunks = n_heads // chunk_heads
    for hd in range(n_heads):
        qs_ref[hd * tq:(hd + 1) * tq, :] = q_ref[:, hd * 2 * LANES:(hd + 1) * 2 * LANES]
    m_ref[...] = jnp.full(m_ref.shape, NEG_INF, F32)
    l_ref[...] = jnp.zeros(l_ref.shape, F32)
    acc_ref[...] = jnp.zeros(acc_ref.shape, F32)

    def step(j, width, masked):
        k = k_ref[pl.ds(pl.multiple_of(j * tq, tq), width), :]
        v = k[:, :LANES]
        for c in range(n_chunks):
            rows = slice(c * cr, (c + 1) * cr)
            s = lax.dot_general(qs_ref[rows, :], k, _NT, preferred_element_type=F32)
            if masked:
                t = lax.broadcasted_iota(jnp.int32, s.shape, 0) & (tq - 1)
                col = lax.broadcasted_iota(jnp.int32, s.shape, 1)
                s = jnp.where(col <= t, s, NEG_INF)
            parts = [s[:, w * LANES:(w + 1) * LANES] for w in range(width // LANES)]
            mc = parts[0]
            for part in parts[1:]:
                mc = jnp.maximum(mc, part)
            m_prev = m_ref[rows, :]
            m_new = jnp.maximum(m_prev, jnp.max(mc, axis=1, keepdims=True))
            alpha = jnp.exp2(m_prev - m_new)
            ps = [jnp.exp2(part - m_new) for part in parts]
            psum = ps[0]
            for pp in ps[1:]:
                psum = psum + pp
            l_ref[rows, :] = alpha * l_ref[rows, :] + psum
            p = jnp.concatenate(ps, axis=1).astype(BF16)
            acc_ref[rows, :] = alpha * acc_ref[rows, :] + jnp.dot(p, v, preferred_element_type=F32)
            m_ref[rows, :] = m_new

    def body(jj, carry):
        step(2 * jj, 2 * tq, False)
        return carry

    lax.fori_loop(0, i // 2, body, 0)

    @pl.when(i % 2 == 1)
    def _():
        step(i - 1, tq, False)

    step(i, tq, True)
    out = acc_ref[...] / jnp.sum(l_ref[...], axis=1, keepdims=True)
    for hd in range(n_heads):
        o_ref[:, hd * LANES:(hd + 1) * LANES] = out[hd * tq:(hd + 1) * tq, :].astype(o_ref.dtype)


def _prompt_attention(q, kc, *, batch, seq, tq, n_heads):
    n = q.shape[0]
    nq = seq // tq
    rows = n_heads * tq
    chunk_heads = 2 if n_heads % 2 == 0 else 1
    assert tq & (tq - 1) == 0 and seq % tq == 0 and tq % LANES == 0
    return pl.pallas_call(
        functools.partial(_prompt_attn_kernel, tq=tq, n_heads=n_heads, chunk_heads=chunk_heads),
        grid=(batch, nq),
        in_specs=[pl.BlockSpec((tq, q.shape[1]), lambda b, i: (b * nq + i, 0)),
                  pl.BlockSpec((seq, kc.shape[1]), lambda b, i: (b, 0))],
        out_specs=pl.BlockSpec((tq, n_heads * LANES), lambda b, i: (b * nq + i, 0)),
        out_shape=jax.ShapeDtypeStruct((n, n_heads * LANES), BF16),
        scratch_shapes=[pltpu.VMEM((rows, 2 * LANES), BF16),
                        pltpu.VMEM((rows, LANES), F32),
                        pltpu.VMEM((rows, LANES), F32),
                        pltpu.VMEM((rows, LANES), F32)],
        compiler_params=_cparams(("parallel", "arbitrary")),
    )(q, kc)


def _sample_attn_kernel(pt_ref, q_ref, ckvn_ref, kpen_ref, kv_hbm, rp_hbm, o_ref, kvbuf, rpbuf, qs_ref, m_ref, l_ref,
                        acc_ref, sem, *, layer, n_pages, cpages, n_heads, t_new, rope, page):
    b = pl.program_id(0)
    nb = pl.num_programs(0)
    slot = b % 2

    def fetch(seq, s):
        def body(p, carry):
            pg = pt_ref[seq, p]
            pltpu.make_async_copy(kv_hbm.at[layer, pg], kvbuf.at[s, p], sem.at[0, s]).start()
            pltpu.make_async_copy(rp_hbm.at[layer, pg], rpbuf.at[s, p], sem.at[1, s]).start()
            return carry
        lax.fori_loop(0, n_pages, body, 0)

    @pl.when(b == 0)
    def _():
        fetch(0, 0)

    @pl.when(b + 1 < nb)
    def _():
        fetch(b + 1, 1 - slot)

    for hd in range(n_heads):
        qs_ref[hd * t_new:(hd + 1) * t_new, :] = q_ref[:, hd * 2 * LANES:(hd + 1) * 2 * LANES]
    m_ref[...] = jnp.full(m_ref.shape, NEG_INF, F32)
    l_ref[...] = jnp.zeros(l_ref.shape, F32)
    acc_ref[...] = jnp.zeros(acc_ref.shape, F32)
    qs = qs_ref[...]
    ql = qs[:, :LANES].astype(BF16)
    qp = qs[:, LANES:LANES + rope].astype(BF16)

    pltpu.make_async_copy(kv_hbm.at[layer, pl.ds(0, n_pages)], kvbuf.at[slot], sem.at[0, slot]).wait()
    pltpu.make_async_copy(rp_hbm.at[layer, pl.ds(0, n_pages)], rpbuf.at[slot], sem.at[1, slot]).wait()
    for c in range(n_pages // cpages):
        kv = kvbuf[slot, c * cpages:(c + 1) * cpages].reshape(cpages * page, LANES).astype(BF16)
        rp = jnp.concatenate([rpbuf[slot, c * cpages + p] for p in range(cpages)], axis=1).astype(BF16)
        s = (lax.dot_general(ql, kv, _NT, preferred_element_type=F32)
             + jnp.dot(qp, rp, preferred_element_type=F32))
        _softmax_update(s, kv, m_ref, l_ref, acc_ref)

    kn = jnp.concatenate([ckvn_ref[...], jnp.zeros((page - t_new, LANES), F32)], axis=0).astype(BF16)
    rn = jnp.concatenate([kpen_ref[...], jnp.zeros((page - t_new, rope), F32)], axis=0).astype(BF16)
    s2 = (lax.dot_general(ql, kn, _NT, preferred_element_type=F32)
          + lax.dot_general(qp, rn, _NT, preferred_element_type=F32))
    t = lax.broadcasted_iota(jnp.int32, s2.shape, 0) & (t_new - 1)
    col = lax.broadcasted_iota(jnp.int32, s2.shape, 1)
    s2 = jnp.where(col <= t, s2, NEG_INF)
    _softmax_update(s2, kn, m_ref, l_ref, acc_ref)
    out = acc_ref[...] / l_ref[...]
    for hd in range(n_heads):
        o_ref[:, hd * LANES:(hd + 1) * LANES] = out[hd * t_new:(hd + 1) * t_new, :]


def _sample_attention(q, ckv_new, kpe_new, cache_kv, cache_rope_t, page_table, layer, *, n_heads, t_new, cpages):
    n = q.shape[0]
    db = n // t_new
    n_pages = page_table.shape[1]
    page = cache_kv.shape[2]
    rope = cache_rope_t.shape[2]
    assert n_pages % cpages == 0 and t_new == 8 and cache_kv.shape[-1] == LANES and cache_rope_t.shape[3] == page
    rows = n_heads * t_new
    per_seq = lambda b, pt: (b, 0)
    return pl.pallas_call(
        functools.partial(_sample_attn_kernel, layer=layer, n_pages=n_pages, cpages=cpages, n_heads=n_heads,
                          t_new=t_new, rope=rope, page=page),
        grid_spec=pltpu.PrefetchScalarGridSpec(
            num_scalar_prefetch=1,
            grid=(db,),
            in_specs=[pl.BlockSpec((t_new, q.shape[1]), per_seq),
                      pl.BlockSpec((t_new, LANES), per_seq),
                      pl.BlockSpec((t_new, rope), per_seq),
                      pl.BlockSpec(memory_space=pl.ANY),
                      pl.BlockSpec(memory_space=pl.ANY)],
            out_specs=pl.BlockSpec((t_new, n_heads * LANES), per_seq),
            scratch_shapes=[pltpu.VMEM((2, n_pages, page, LANES), cache_kv.dtype),
                            pltpu.VMEM((2, n_pages, rope, page), cache_rope_t.dtype),
                            pltpu.VMEM((rows, 2 * LANES), F32),
                            pltpu.VMEM((rows, 1), F32),
                            pltpu.VMEM((rows, 1), F32),
                            pltpu.VMEM((rows, LANES), F32),
                            pltpu.SemaphoreType.DMA((2, 2))]),
        out_shape=jax.ShapeDtypeStruct((n, n_heads * LANES), F32),
        compiler_params=_cparams(("arbitrary",)),
    )(page_table, q, ckv_new, kpe_new, cache_kv, cache_rope_t)


def _mixer_out_kernel(xp_ref, xs_ref, ycp_ref, ycs_ref, atp_ref, ats_ref, wuv_ref, ga_ref, wo_ref, g_ref, b_ref,
                      x1_ref, x1p_ref, *, n_prompt_tiles, conv_ch, alpha):
    is_p = pl.program_id(0) < n_prompt_tiles
    tm = x1_ref.shape[0]
    sub = x1p_ref.shape[0] // tm
    n_chunks = 2 if tm % 32 == 0 else 1
    cm = tm // n_chunks
    for c in range(n_chunks):
        rows = slice(c * cm, (c + 1) * cm)
        x = jnp.where(is_p, xp_ref[rows, :], xs_ref[rows, :])
        yc = jnp.where(is_p, ycp_ref[rows, :], ycs_ref[rows, :])
        at = jnp.where(is_p, atp_ref[rows, :], ats_ref[rows, :].astype(BF16))
        o = jnp.dot(at, wuv_ref[...], preferred_element_type=F32)
        ya = _rms(o, ga_ref[...]).astype(BF16)
        mix = (jnp.dot(yc, wo_ref[:conv_ch, :], preferred_element_type=F32)
               + jnp.dot(ya, wo_ref[conv_ch:, :], preferred_element_type=F32))
        x1 = _layer_norm(alpha * x + mix, g_ref[...], b_ref[...])
        x1_ref[rows, :] = x1
        _pack_rows(x1, x1p_ref.at[pl.ds(c * cm * sub, cm * sub)])


def _mixer_out(xp, xs, ycp, ycs, atp, ats, wuv_bd, ga, wo, g, b, *, tm, alpha):
    np_, d = xp.shape
    ns = xs.shape[0]
    assert np_ % tm == 0 and ns % tm == 0
    npt, nst = np_ // tm, ns // tm
    conv_ch = ycp.shape[1]
    p_map = lambda i: (jnp.minimum(i, npt - 1), 0)
    s_map = lambda i: (jnp.maximum(i - npt, 0), 0)
    const = lambda i: (0, 0)
    tile = lambda i: (i, 0)
    nt = np_ + ns
    sub = d // (2 * LANES)
    return pl.pallas_call(
        functools.partial(_mixer_out_kernel, n_prompt_tiles=npt, conv_ch=conv_ch, alpha=alpha),
        grid=(npt + nst,),
        in_specs=[pl.BlockSpec((tm, d), p_map), pl.BlockSpec((tm, d), s_map),
                  pl.BlockSpec((tm, conv_ch), p_map), pl.BlockSpec((tm, conv_ch), s_map),
                  pl.BlockSpec((tm, atp.shape[1]), p_map), pl.BlockSpec((tm, ats.shape[1]), s_map),
                  pl.BlockSpec(wuv_bd.shape, const), pl.BlockSpec(ga.shape, const),
                  pl.BlockSpec(wo.shape, const), pl.BlockSpec(g.shape, const), pl.BlockSpec(b.shape, const)],
        out_specs=[pl.BlockSpec((tm, d), tile), pl.BlockSpec((tm * sub, LANES), tile)],
        out_shape=[jax.ShapeDtypeStruct((nt, d), F32), jax.ShapeDtypeStruct((nt * sub, LANES), U32)],
        compiler_params=_cparams(("parallel",)),
    )(xp, xs, ycp, ycs, atp, ats, wuv_bd, ga, wo, g, b)


def _router_kernel(x_ref, wh_ref, wl_ref, b_ref, tri_ref, idx_ref, w_ref, rank_ref, cnt_ref, base_ref,
                   *, tt, n_exp):
    @pl.when(pl.program_id(0) == 0)
    def _():
        base_ref[...] = jnp.zeros(base_ref.shape, F32)

    x = x_ref[...]
    xh = x.astype(BF16)
    xl = (x - xh.astype(F32)).astype(BF16)
    wh = wh_ref[...]
    z = (lax.dot_general(wh, xh, _NT, preferred_element_type=F32)
         + lax.dot_general(wh, xl, _NT, preferred_element_type=F32)
         + lax.dot_general(wl_ref[...], xh, _NT, preferred_element_type=F32))
    s = 1.0 / (1.0 + jnp.exp(-z))
    sb = s + b_ref[...]

    gsz = n_exp // N_GROUPS
    git = lax.broadcasted_iota(jnp.int32, (gsz, tt), 0).astype(F32)
    blocks, gscore = [], []
    for g in range(N_GROUPS):
        blk = sb[g * gsz:(g + 1) * gsz, :]
        m1 = jnp.max(blk, axis=0, keepdims=True)
        f1 = jnp.min(jnp.where(blk == m1, git, float(gsz)), axis=0, keepdims=True)
        m2 = jnp.max(jnp.where(git == f1, -jnp.inf, blk), axis=0, keepdims=True)
        blocks.append(blk)
        gscore.append(m1 + m2)
    masked = []
    for g in range(N_GROUPS):
        ahead = jnp.zeros((1, tt), F32)
        for g2 in range(N_GROUPS):
            if g2 == g:
                continue
            beats = (gscore[g2] >= gscore[g]) if g2 < g else (gscore[g2] > gscore[g])
            ahead = ahead + jnp.where(beats, 1.0, 0.0)
        masked.append(jnp.where(ahead < float(TOPK_GROUPS), blocks[g], -jnp.inf))
    vals = jnp.concatenate(masked, axis=0)

    rowi = lax.broadcasted_iota(jnp.int32, (n_exp, tt), 0).astype(F32)
    vals0 = vals
    picks, wks = [], []
    for k in range(TOP_K):
        m = jnp.max(vals, axis=0, keepdims=True)
        ik = jnp.min(jnp.where(vals == m, rowi, float(n_exp)), axis=0, keepdims=True)
        hit = rowi == ik
        wks.append(jnp.sum(jnp.where(hit, s, 0.0), axis=0, keepdims=True))
        vals = jnp.where(hit, -jnp.inf, vals)
        picks.append(ik)
    chosen = jnp.where(vals != vals0, 1.0, 0.0)
    wsum = wks[0]
    for k in range(1, TOP_K):
        wsum = wsum + wks[k]

    incl = jnp.dot(chosen.astype(BF16), tri_ref[...], preferred_element_type=F32)
    rnk = base_ref[...] + (incl - chosen)
    for k in range(TOP_K):
        idx_ref[k:k + 1, :] = picks[k].astype(jnp.int32)
        w_ref[k:k + 1, :] = wks[k] / wsum * ROUTED_SCALE
        rk = jnp.sum(jnp.where(rowi == picks[k], rnk, 0.0), axis=0, keepdims=True)
        rank_ref[k:k + 1, :] = rk.astype(jnp.int32)
    base = base_ref[...] + jnp.sum(chosen, axis=1, keepdims=True)
    base_ref[...] = base
    cnt_ref[...] = jnp.broadcast_to(base, cnt_ref.shape)


def _router(x1, wr_hi, wr_lo, bias, *, tt):
    nt, d = x1.shape
    n_exp = wr_hi.shape[0]
    assert nt % tt == 0
    tri = jnp.triu(jnp.ones((tt, tt), BF16))
    const = lambda i: (0, 0)
    col = lambda i: (0, i)
    return pl.pallas_call(
        functools.partial(_router_kernel, tt=tt, n_exp=n_exp),
        grid=(nt // tt,),
        in_specs=[pl.BlockSpec((tt, d), lambda i: (i, 0)),
                  pl.BlockSpec(wr_hi.shape, const), pl.BlockSpec(wr_lo.shape, const),
                  pl.BlockSpec(bias.shape, const), pl.BlockSpec(tri.shape, const)],
        out_specs=[pl.BlockSpec((TOP_K, tt), col), pl.BlockSpec((TOP_K, tt), col),
                   pl.BlockSpec((TOP_K, tt), col), pl.BlockSpec((n_exp, LANES), const)],
        out_shape=[jax.ShapeDtypeStruct((TOP_K, nt), jnp.int32), jax.ShapeDtypeStruct((TOP_K, nt), F32),
                   jax.ShapeDtypeStruct((TOP_K, nt), jnp.int32), jax.ShapeDtypeStruct((n_exp, LANES), F32)],
        scratch_shapes=[pltpu.VMEM((n_exp, 1), F32)],
        compiler_params=_cparams(("arbitrary",)),
    )(x1, wr_hi, wr_lo, bias, tri)


def _slot_rows_kernel(idx_ref, rank_ref, ps_ref, o_ref, *, n_exp):
    tt = idx_ref.shape[1]
    rowi = lax.broadcasted_iota(jnp.int32, (n_exp, tt), 0)
    ps = ps_ref[...]
    for k in range(TOP_K):
        start = jnp.sum(jnp.where(rowi == idx_ref[k:k + 1, :], ps, 0.0), axis=0, keepdims=True)
        o_ref[k:k + 1, :] = start.astype(jnp.int32) + rank_ref[k:k + 1, :]


def _slot_rows(idx_t, rank_t, pad_start, *, tt):
    nt = idx_t.shape[1]
    n_exp = pad_start.shape[0]
    assert nt % tt == 0 and n_exp * SLOT_ROWS + nt * TOP_K < 2 ** 24
    col = lambda i: (0, i)
    return pl.pallas_call(
        functools.partial(_slot_rows_kernel, n_exp=n_exp),
        grid=(nt // tt,),
        in_specs=[pl.BlockSpec((TOP_K, tt), col), pl.BlockSpec((TOP_K, tt), col),
                  pl.BlockSpec((n_exp, 1), lambda i: (0, 0))],
        out_specs=pl.BlockSpec((TOP_K, tt), col),
        out_shape=jax.ShapeDtypeStruct((TOP_K, nt), jnp.int32),
        compiler_params=_cparams(("parallel",)),
    )(idx_t, rank_t, pad_start.astype(F32).reshape(n_exp, 1))


def _dispatch_kernel(ps_ref, nb_ref, cnt_ref, dest_ref, x_ref, o_ref, zbuf, sem, zsem, *, td, sub, n_exp):
    i = pl.program_id(0)
    blk = SLOT_ROWS * sub

    @pl.when(i == 0)
    def _():
        zbuf[...] = jnp.zeros(zbuf.shape, zbuf.dtype)
        n_blocks = o_ref.shape[0] // blk
        used = ps_ref[n_exp - 1] // SLOT_ROWS + nb_ref[n_exp - 1]

        def zero_block(b):
            return pltpu.make_async_copy(zbuf, o_ref.at[pl.ds(pl.multiple_of(b * blk, blk), blk)], zsem)

        def partial(e):
            return (cnt_ref[e] & (SLOT_ROWS - 1)) != 0

        def fill(e, carry):
            @pl.when(partial(e))
            def _():
                zero_block(ps_ref[e] // SLOT_ROWS + nb_ref[e] - 1).start()
            return carry

        def fill_done(e, carry):
            @pl.when(partial(e))
            def _():
                zero_block(0).wait()
            return carry

        def tail(b, carry):
            zero_block(b).start()
            return carry

        def tail_done(b, carry):
            zero_block(0).wait()
            return carry

        lax.fori_loop(0, n_exp, fill, 0)
        lax.fori_loop(used, n_blocks, tail, 0)
        lax.fori_loop(0, n_exp, fill_done, 0)
        lax.fori_loop(used, n_blocks, tail_done, 0)

    def per_token(t, carry):
        src = x_ref.at[pl.ds(pl.multiple_of(t * sub, sub), sub)]
        for k in range(TOP_K):
            dst = pl.multiple_of(dest_ref[k, t] * sub, sub)
            pltpu.make_async_copy(src, o_ref.at[pl.ds(dst, sub)], sem).start(priority=k % 2)
        return carry

    lax.fori_loop(0, td, per_token, 0)
    for _ in range(TOP_K):
        pltpu.make_async_copy(x_ref, o_ref.at[pl.ds(0, td * sub)], sem).wait()


def _dispatch(pad_start, n_blk, counts, dest_t, x1p, n_slots, *, td, sub):
    nt = x1p.shape[0] // sub
    assert nt % td == 0 and n_slots >= td and n_slots % SLOT_ROWS == 0
    return pl.pallas_call(
        functools.partial(_dispatch_kernel, td=td, sub=sub, n_exp=counts.shape[0]),
        grid_spec=pltpu.PrefetchScalarGridSpec(
            num_scalar_prefetch=3,
            grid=(nt // td,),
            in_specs=[pl.BlockSpec((TOP_K, td), lambda i, *_: (0, i), memory_space=pltpu.SMEM),
                      pl.BlockSpec((td * sub, LANES), lambda i, *_: (i, 0))],
            out_specs=pl.BlockSpec(memory_space=pl.ANY),
            scratch_shapes=[pltpu.VMEM((SLOT_ROWS * sub, LANES), x1p.dtype), pltpu.SemaphoreType.DMA(()),
                            pltpu.SemaphoreType.DMA(())]),
        out_shape=jax.ShapeDtypeStruct((n_slots * sub, LANES), x1p.dtype),
        compiler_params=_cparams(("arbitrary",)),
    )(pad_start, n_blk, counts, dest_t, x1p)


EXPERT_RING = 8


def _experts_kernel(ps_ref, nb_ref, wg_ref, wu_ref, wd_ref, xs_hbm, ys_hbm, wg_s, wu_s, wd_s, xbuf, ybuf,
                    xsem, ysem, zsem, *, sub, n_exp):
    e = pl.program_id(0)
    blk = SLOT_ROWS * sub
    n_blocks = xs_hbm.shape[0] // blk
    used = ps_ref[n_exp - 1] // SLOT_ROWS + nb_ref[n_exp - 1]

    def rows(g):
        return pl.ds(pl.multiple_of(g * blk, blk), blk)

    def x_copy(g, s):
        return pltpu.make_async_copy(xs_hbm.at[rows(g)], xbuf.at[s], xsem.at[s])

    def y_copy(g, s):
        return pltpu.make_async_copy(ybuf.at[s], ys_hbm.at[rows(g)], ysem.at[s])

    @pl.when(e == 0)
    def _():
        zs = EXPERT_RING - 1
        ybuf[zs] = jnp.zeros(ybuf.shape[1:], ybuf.dtype)

        def fill(b, carry):
            pltpu.make_async_copy(ybuf.at[zs], ys_hbm.at[rows(b)], zsem).start()
            return carry

        def fill_done(b, carry):
            pltpu.make_async_copy(ybuf.at[zs], ys_hbm.at[rows(0)], zsem).wait()
            return carry

        lax.fori_loop(used, n_blocks, fill, 0)
        lax.fori_loop(used, n_blocks, fill_done, 0)
        for s in range(EXPERT_RING):
            @pl.when(s < used)
            def _():
                x_copy(s, s).start()

    nb = nb_ref[e]
    first = ps_ref[e] // SLOT_ROWS

    @pl.when(nb > 0)
    def _():
        wg_s[...] = wg_ref[...].astype(BF16)
        wu_s[...] = wu_ref[...].astype(BF16)
        wd_s[...] = wd_ref[...].astype(BF16)

        def body(j, carry):
            g = first + j
            s = g & (EXPERT_RING - 1)
            x_copy(g, s).wait()

            @pl.when(g >= EXPERT_RING)
            def _():
                y_copy(g, s).wait()

            lo, hi = _unpack_rows(xbuf.at[s], 0, SLOT_ROWS, sub)
            x = jnp.concatenate(lo + hi, axis=1).astype(BF16)
            gate = jnp.dot(x, wg_s[...], preferred_element_type=F32)
            up = jnp.dot(x, wu_s[...], preferred_element_type=F32)
            hmid = (_silu(gate) * up).astype(BF16)
            _pack_rows(jnp.dot(hmid, wd_s[...], preferred_element_type=F32), ybuf.at[s])
            y_copy(g, s).start()

            @pl.when(g + EXPERT_RING < used)
            def _():
                x_copy(g + EXPERT_RING, s).start()
            return carry

        lax.fori_loop(0, nb, body, 0)

    @pl.when(e == n_exp - 1)
    def _():
        for s in range(EXPERT_RING):
            @pl.when(s < used)
            def _():
                y_copy(0, s).wait()


def _experts(pad_start, n_blk, xs, w_gate, w_up, w_down, layer):
    n_exp, d, de = w_gate.shape[-3:]
    sub = d // (2 * LANES)
    blk = SLOT_ROWS * sub
    assert xs.shape[0] % blk == 0 and xs.shape[1] == LANES
    return pl.pallas_call(
        functools.partial(_experts_kernel, sub=sub, n_exp=n_exp),
        grid_spec=pltpu.PrefetchScalarGridSpec(
            num_scalar_prefetch=2,
            grid=(n_exp,),
            in_specs=[pl.BlockSpec((None, None, d, de), lambda e, ps, nb: (layer, e, 0, 0)),
                      pl.BlockSpec((None, None, d, de), lambda e, ps, nb: (layer, e, 0, 0)),
                      pl.BlockSpec((None, None, de, d), lambda e, ps, nb: (layer, e, 0, 0)),
                      pl.BlockSpec(memory_space=pl.ANY)],
            out_specs=pl.BlockSpec(memory_space=pl.ANY),
            scratch_shapes=[pltpu.VMEM((d, de), BF16), pltpu.VMEM((d, de), BF16), pltpu.VMEM((de, d), BF16),
                            pltpu.VMEM((EXPERT_RING, blk, LANES), U32), pltpu.VMEM((EXPERT_RING, blk, LANES), U32),
                            pltpu.SemaphoreType.DMA((EXPERT_RING,)), pltpu.SemaphoreType.DMA((EXPERT_RING,)),
                            pltpu.SemaphoreType.DMA(())]),
        out_shape=jax.ShapeDtypeStruct(xs.shape, U32),
        compiler_params=_cparams(("arbitrary",)),
    )(pad_start, n_blk, w_gate, w_up, w_down, xs)


def _combine_kernel(dest0_ref, dest1_ref, w_ref, x1_ref, wsg_ref, wsu_ref,
                    wsd_ref, g_ref, b_ref, ys_ref, yp_ref, ysm_ref, buf, sem, *, tc, alpha, sub, n_prompt_tiles):
    i = pl.program_id(0)
    n = pl.num_programs(0)
    slot = i % 2

    def gather_row(dest_ref, s, k, t):
        src = pl.multiple_of(dest_ref[k, t] * sub, sub)
        dst = (k * tc + t) * sub
        if not isinstance(dst, int):
            dst = pl.multiple_of(dst, sub)
        pltpu.make_async_copy(ys_ref.at[pl.ds(src, sub)], buf.at[s, pl.ds(dst, sub)], sem.at[s]).start(priority=k % 2)

    @pl.when(i == 0)
    def _():
        def per_token(t, carry):
            for k in range(TOP_K):
                gather_row(dest0_ref, 0, k, t)
            return carry
        lax.fori_loop(0, tc, per_token, 0)

    for t in range(tc):
        for k in range(TOP_K):
            gather_row(dest1_ref, 1 - slot, k, t)

    x1 = x1_ref[...]
    xb = x1.astype(BF16)
    hs = (_silu(jnp.dot(xb, wsg_ref[...], preferred_element_type=F32))
          * jnp.dot(xb, wsu_ref[...], preferred_element_type=F32)).astype(BF16)
    shared = jnp.dot(hs, wsd_ref[...], preferred_element_type=F32)

    def drain(s):
        pltpu.make_async_copy(ys_ref.at[pl.ds(0, TOP_K * tc * sub)], buf.at[s], sem.at[s]).wait()

    drain(slot)
    w = w_ref[...]
    lo_acc = [jnp.zeros((tc, LANES), F32) for _ in range(sub)]
    hi_acc = [jnp.zeros((tc, LANES), F32) for _ in range(sub)]
    for k in range(TOP_K):
        lo, hi = _unpack_rows(buf.at[slot], k * tc, tc, sub)
        wk = w[:, k:k + 1]
        for j in range(sub):
            lo_acc[j] = lo_acc[j] + wk * lo[j]
            hi_acc[j] = hi_acc[j] + wk * hi[j]
    moe = jnp.concatenate(lo_acc + hi_acc, axis=1) + shared
    y = _layer_norm(alpha * x1 + moe, g_ref[...], b_ref[...])

    @pl.when(i < n_prompt_tiles)
    def _():
        yp_ref[...] = y

    @pl.when(i >= n_prompt_tiles)
    def _():
        ysm_ref[...] = y

    @pl.when(i == n - 1)
    def _():
        drain(1 - slot)


def _combine(dest_t, w_tok, x1, wsg, wsu, wsd, g, b, ys, *, tc, alpha, n_prompt):
    nt, d = x1.shape
    n_tiles = nt // tc
    sub = d // (2 * LANES)
    assert nt % tc == 0 and n_prompt % tc == 0 and 0 < n_prompt < nt and ys.shape[0] >= TOP_K * tc * sub
    npt = n_prompt // tc
    cur = pl.BlockSpec((TOP_K, tc), lambda i: (0, i), memory_space=pltpu.SMEM)
    nxt = pl.BlockSpec((TOP_K, tc), lambda i: (0, jnp.minimum(i + 1, n_tiles - 1)), memory_space=pltpu.SMEM)
    const = lambda i: (0, 0)
    tile = lambda i: (i, 0)
    return pl.pallas_call(
        functools.partial(_combine_kernel, tc=tc, alpha=alpha, sub=sub, n_prompt_tiles=npt),
        grid=(n_tiles,),
        in_specs=[cur, nxt,
                  pl.BlockSpec((tc, TOP_K), tile),
                  pl.BlockSpec((tc, d), tile),
                  pl.BlockSpec(wsg.shape, const), pl.BlockSpec(wsu.shape, const), pl.BlockSpec(wsd.shape, const),
                  pl.BlockSpec(g.shape, const), pl.BlockSpec(b.shape, const),
                  pl.BlockSpec(memory_space=pl.ANY)],
        out_specs=[pl.BlockSpec((tc, d), lambda i: (jnp.minimum(i, npt - 1), 0)),
                   pl.BlockSpec((tc, d), lambda i: (jnp.maximum(i - npt, 0), 0))],
        out_shape=[jax.ShapeDtypeStruct((n_prompt, d), F32), jax.ShapeDtypeStruct((nt - n_prompt, d), F32)],
        scratch_shapes=[pltpu.VMEM((2, TOP_K * tc * sub, LANES), ys.dtype), pltpu.SemaphoreType.DMA((2,))],
        compiler_params=_cparams(("arbitrary",)),
    )(dest_t, dest_t, w_tok, x1, wsg, wsu, wsd, g, b, ys)


def _rope_cs(pos, rope):
    half = rope // 2
    inv = ROPE_BASE ** (-jnp.arange(half, dtype=F32) / half)
    ang = pos.astype(F32)[:, None] * inv[None, :]
    cos, sin = jnp.cos(ang), jnp.sin(ang)
    pad = jnp.zeros((pos.shape[0], LANES - rope), F32)
    return (jnp.concatenate([cos, cos, pad], axis=1), jnp.concatenate([-sin, sin, pad], axis=1))


def _pick_tile(n, pref):
    t = pref
    while n % t:
        t //= 2
    return t


def kernel(x_prompt, x_sample, cache_kv_latent, cache_k_rope, state_conv, page_table, w_in, conv_w, q_norm, w_uq, kv_norm, w_uk, w_uv, g_conv, g_attn, w_o, ln1_g, ln1_b, w_router, router_bias, w_gate, w_up, w_down, ws_gate, ws_up, ws_down, ln2_g, ln2_b):
    depth = w_in.shape[0]
    bsz, seq, d = x_prompt.shape
    db, t_new, _ = x_sample.shape
    kv_lora, n_heads, nope = w_uk.shape[1:]
    v_dim = w_uv.shape[-1]
    rope = cache_k_rope.shape[-1]
    q_lora = q_norm.shape[-1]
    conv_ch = conv_w.shape[-1]
    n_exp = w_router.shape[-1]
    page = cache_kv_latent.shape[2]
    past_len = page_table.shape[1] * page
    alpha = (2.0 * depth) ** 0.25
    scale = float((nope + rope) ** -0.5) * math.log2(math.e)
    in_cols = w_in.shape[-1]
    in_pad = -(-(in_cols - rope + LANES) // LANES) * LANES
    n_p, n_s = bsz * seq, db * t_new
    nt = n_p + n_s

    tm_p = _pick_tile(seq, 512)
    tm_s = _pick_tile(n_s, 512)
    tq = _pick_tile(seq, 256)
    tm_o = _pick_tile(math.gcd(n_p, n_s), 512)
    tt = _pick_tile(math.gcd(n_p, n_s), 512)
    td = _pick_tile(nt, 1024)
    tc = _pick_tile(math.gcd(n_p, n_s), 256)
    pps = _pick_tile(page_table.shape[1], 16)
    cache_rope_t = jnp.swapaxes(cache_k_rope, 2, 3)

    cos_p, sin_p = _rope_cs(jnp.arange(seq, dtype=jnp.int32), rope)
    cos_s, sin_s = _rope_cs(past_len + jnp.arange(t_new, dtype=jnp.int32), rope)
    cos_s, sin_s = jnp.tile(cos_s, (tm_s // t_new, 1)), jnp.tile(sin_s, (tm_s // t_new, 1))

    xp, xs_ = x_prompt.reshape(n_p, d), x_sample.reshape(n_s, d)
    outs = [[] for _ in range(6)]
    for l in range(depth):
        win = jnp.pad(w_in[l], ((0, 0), (0, in_pad - in_cols))).astype(BF16)
        wuq3 = w_uq[l].reshape(q_lora, n_heads, nope + rope)
        qlat = _fold_qlat(jnp.transpose(wuq3[:, :, :nope], (1, 0, 2)), jnp.transpose(w_uk[l], (1, 2, 0)), scale)
        wq_rope = jnp.transpose(wuq3[:, :, nope:], (1, 0, 2)) * scale
        wq = jnp.concatenate([qlat, wq_rope, jnp.zeros((n_heads, q_lora, LANES - rope), F32)], axis=2)
        wq = jnp.transpose(wq, (1, 0, 2)).reshape(q_lora, n_heads * 2 * LANES).astype(BF16)
        eye = jnp.eye(n_heads, dtype=F32)
        wuv_bd = jnp.einsum('chv,hg->hcgv', w_uv[l], eye).reshape(n_heads * kv_lora, n_heads * v_dim).astype(BF16)
        wo = w_o[l].astype(BF16)
        wr_t = w_router[l].T
        wr_hi = wr_t.astype(BF16)
        wr_lo = (wr_t - wr_hi.astype(F32)).astype(BF16)
        row = lambda v: v[l].reshape(1, -1)

        common = (win, wq, row(q_norm), row(kv_norm), conv_w[l], row(g_conv))
        q_p, kc_p, ckv_p, kpe_p, u_p, yc_p = _mixer_in(
            xp, *common, cos_p, sin_p, None, prompt=True, seq=seq, tm=tm_p, n_heads=n_heads, rope=rope, q_dtype=BF16)
        st = state_conv[l].astype(F32)
        zero = jnp.zeros((db, conv_ch), F32)
        prev1 = jnp.stack([st[:, 1]] + [zero] * (t_new - 1), axis=1).reshape(n_s, conv_ch)
        prev2 = jnp.stack([st[:, 0], st[:, 1]] + [zero] * (t_new - 2), axis=1).reshape(n_s, conv_ch)
        q_s, kc_s, ckv_s, kpe_s, u_s, yc_s = _mixer_in(
            xs_, *common, cos_s, sin_s, (prev1, prev2), prompt=False, seq=t_new, tm=tm_s, n_heads=n_heads, rope=rope,
            q_dtype=F32)
        del kc_s

        at_p = _prompt_attention(q_p, kc_p, batch=bsz, seq=seq, tq=tq, n_heads=n_heads)
        at_s = _sample_attention(q_s, ckv_s, kpe_s, cache_kv_latent, cache_rope_t, page_table, l,
                                 n_heads=n_heads, t_new=t_new, cpages=pps)

        x1, x1p = _mixer_out(xp, xs_, yc_p, yc_s, at_p, at_s, wuv_bd, row(g_attn), wo, row(ln1_g), row(ln1_b),
                             tm=tm_o, alpha=alpha)

        idx_t, w_t, rank_t, cnt = _router(x1, wr_hi, wr_lo, router_bias[l].reshape(n_exp, 1), tt=tt)
        counts = cnt[:, 0].astype(jnp.int32)
        padded = (counts + SLOT_ROWS - 1) // SLOT_ROWS * SLOT_ROWS
        pad_end = jnp.cumsum(padded)
        pad_start = (pad_end - padded).astype(jnp.int32)
        n_blk = (padded // SLOT_ROWS).astype(jnp.int32)
        n_blocks = -(-(nt * TOP_K) // SLOT_ROWS) + n_exp

        dest_t = _slot_rows(idx_t, rank_t, pad_start, tt=td)
        xs_sorted = _dispatch(pad_start, n_blk, counts, dest_t, x1p, n_blocks * SLOT_ROWS, td=td, sub=d // (2 * LANES))
        ys = _experts(pad_start, n_blk, xs_sorted, w_gate, w_up, w_down, l)
        xp, xs_ = _combine(dest_t, w_t.T, x1, ws_gate[l].astype(BF16), ws_up[l].astype(BF16),
                           ws_down[l].astype(BF16), row(ln2_g), row(ln2_b), ys, tc=tc, alpha=alpha, n_prompt=n_p)
        outs[0].append(ckv_p.reshape(bsz, seq, kv_lora))
        outs[1].append(kpe_p.reshape(bsz, seq, rope))
        outs[2].append(u_p.reshape(bsz, seq, conv_ch)[:, seq - (CONV_W - 1):])
        outs[3].append(ckv_s.reshape(db, t_new, kv_lora))
        outs[4].append(kpe_s.reshape(db, t_new, rope))
        outs[5].append(u_s.reshape(db, t_new, conv_ch)[:, t_new - (CONV_W - 1):])
    return (xp.reshape(bsz, seq, d), xs_.reshape(db, t_new, d)) + tuple(jnp.stack(o) for o in outs)
```

```python
import functools
import math

import jax
import jax.numpy as jnp
from jax import lax
from jax.experimental import pallas as pl
from jax.experimental.pallas import tpu as pltpu

F32 = jnp.float32
BF16 = jnp.bfloat16
U32 = jnp.uint32

ROPE_BASE = 10000.0
NORM_EPS = 1e-6
LN_EPS = 1e-5
NEG_INF = -1e30
TOP_K = 8
N_GROUPS = 8
TOPK_GROUPS = 4
ROUTED_SCALE = 2.5
CONV_W = 3

LANES = 128
SLOT_ROWS = 256
VMEM_LIMIT = 52 * 1024 * 1024

_NT = (((1,), (1,)), ((), ()))


def _cparams(sem):
    return pltpu.CompilerParams(dimension_semantics=sem, vmem_limit_bytes=VMEM_LIMIT)


def _rms(x, g):
    return x * lax.rsqrt(jnp.mean(x * x, axis=-1, keepdims=True) + NORM_EPS) * g


def _layer_norm(x, g, b):
    mu = jnp.mean(x, axis=-1, keepdims=True)
    xc = x - mu
    var = jnp.mean(xc * xc, axis=-1, keepdims=True)
    return xc * lax.rsqrt(var + LN_EPS) * g + b


def _silu(x):
    return x / (1.0 + jnp.exp(-x))


def _rope_tile(p, c, s, half):
    lane = lax.broadcasted_iota(jnp.int32, p.shape, 1)
    swapped = jnp.where(lane < half, pltpu.roll(p, LANES - half, axis=1), pltpu.roll(p, half, axis=1))
    return p * c + swapped * s


def _pack_rows(x, ref):
    r = x.shape[0]
    half = x.shape[1] // 2
    sub = half // LANES
    bits = lax.bitcast_convert_type(x.astype(BF16).astype(F32), U32)
    words = bits[:, half:] | (bits[:, :half] >> 16)
    for j in range(sub):
        ref[pl.ds(j, r, stride=sub), :] = words[:, j * LANES:(j + 1) * LANES]


def _unpack_rows(ref, first, r, sub):
    lo, hi = [], []
    for j in range(sub):
        w = ref[pl.ds(first * sub + j, r, stride=sub), :]
        lo.append(lax.bitcast_convert_type(w << 16, F32))
        hi.append(lax.bitcast_convert_type(w & jnp.uint32(0xFFFF0000), F32))
    return lo, hi


def _fold_kernel(a_ref, b_ref, o_ref, *, scale):
    o_ref[...] = jnp.dot(a_ref[...], b_ref[...], precision=lax.Precision.HIGHEST,
                         preferred_element_type=F32) * scale


def _fold_qlat(wuq_nope, wuk_t, scale):
    h, r, n = wuq_nope.shape
    c = wuk_t.shape[-1]
    return pl.pallas_call(
        functools.partial(_fold_kernel, scale=scale),
        grid=(h,),
        in_specs=[pl.BlockSpec((None, r, n), lambda i: (i, 0, 0)),
                  pl.BlockSpec((None, n, c), lambda i: (i, 0, 0))],
        out_specs=pl.BlockSpec((None, r, c), lambda i: (i, 0, 0)),
        out_shape=jax.ShapeDtypeStruct((h, r, c), F32),
        compiler_params=_cparams(("parallel",)),
    )(wuq_nope, wuk_t)


def _mixer_in_kernel(*refs, prompt, tiles_per_seq, tm, n_chunks, t_new, n_heads, conv_ch, q_lora, kv_lora, rope):
    if prompt:
        (x_ref, win_ref, wq_ref, qn_ref, kvn_ref, cw_ref, gc_ref, cos_ref, sin_ref,
         q_ref, kc_ref, ckv_ref, kpe_ref, u_ref, yc_ref, carry_ref) = refs
    else:
        (x_ref, win_ref, wq_ref, qn_ref, kvn_ref, cw_ref, gc_ref, cos_ref, sin_ref, p1_ref, p2_ref,
         q_ref, kc_ref, ckv_ref, kpe_ref, u_ref, yc_ref) = refs
    half = rope // 2
    c1, c2, c3 = conv_ch, 2 * conv_ch, 3 * conv_ch
    c4 = c3 + q_lora
    c5 = c4 + kv_lora

    cm = tm // n_chunks
    row = lax.broadcasted_iota(jnp.int32, (cm, 1), 0)
    u_prev = None
    for c in range(n_chunks):
        rows = slice(c * cm, (c + 1) * cm)
        h = jnp.dot(x_ref[rows, :].astype(BF16), win_ref[...], preferred_element_type=F32)
        b_gate, c_gate, x_conv = h[:, :c1], h[:, c1:c2], h[:, c2:c3]
        q_a, c_kv, kp = h[:, c3:c4], h[:, c4:c5], h[:, c5:c5 + LANES]

        u = c_gate * x_conv
        u_ref[rows, :] = u
        if prompt:
            if c == 0:
                first = (pl.program_id(0) % tiles_per_seq) == 0
                prev1 = jnp.where(first, 0.0, carry_ref[7:8, :])
                prev2 = jnp.where(first, 0.0, carry_ref[6:7, :])
            else:
                prev1 = u_prev[cm - 1:cm, :]
                prev2 = u_prev[cm - 2:cm - 1, :]
            t_in = row
            p1 = prev1
            p2 = jnp.where(row == 0, prev2, prev1)
        else:
            t_in = row & (t_new - 1)
            p1 = p1_ref[rows, :]
            p2 = p2_ref[rows, :]
        um1 = jnp.where(t_in == 0, p1, pltpu.roll(u, 1, axis=0))
        um2 = jnp.where(t_in < 2, p2, pltpu.roll(u, 2, axis=0))
        conv_y = cw_ref[0:1, :] * um2 + cw_ref[1:2, :] * um1 + cw_ref[2:3, :] * u
        yc_ref[rows, :] = _rms(b_gate * conv_y, gc_ref[...]).astype(yc_ref.dtype)
        u_prev = u

        cos = cos_ref[rows, :]
        sin = sin_ref[rows, :]
        qn = _rms(q_a, qn_ref[...]).astype(BF16)
        q = jnp.dot(qn, wq_ref[...], preferred_element_type=F32)
        for hd in range(n_heads):
            o = hd * 2 * LANES
            q_ref[rows, o:o + LANES] = q[:, o:o + LANES].astype(q_ref.dtype)
            q_ref[rows, o + LANES:o + 2 * LANES] = _rope_tile(q[:, o + LANES:o + 2 * LANES], cos, sin,
                                                              half).astype(q_ref.dtype)

        ckv = _rms(c_kv, kvn_ref[...])
        ckv_ref[rows, :] = ckv
        kpr = _rope_tile(kp, cos, sin, half)
        kpe_ref[rows, :] = kpr[:, :rope]
        kc_ref[rows, :LANES] = ckv.astype(BF16)
        kc_ref[rows, LANES:] = kpr.astype(BF16)
    if prompt:
        carry_ref[...] = u_prev[cm - 8:, :]


def _mixer_in(x2, win, wq, qn, kvn, cw, gc, cos, sin, prev, *, prompt, seq, tm, n_heads, rope, q_dtype):
    n, d = x2.shape
    conv_ch = cw.shape[-1]
    q_lora = qn.shape[-1]
    kv_lora = kvn.shape[-1]
    assert n % tm == 0 and kv_lora == LANES and tm % 8 == 0
    tiles_per_seq = seq // tm if prompt else 1
    if prompt:
        assert seq % tm == 0
    else:
        assert seq & (seq - 1) == 0 and tm % seq == 0
    const = lambda i: (0, 0)
    tile = lambda i: (i, 0)
    in_specs = [
        pl.BlockSpec((tm, d), tile),
        pl.BlockSpec(win.shape, const),
        pl.BlockSpec(wq.shape, const),
        pl.BlockSpec(qn.shape, const),
        pl.BlockSpec(kvn.shape, const),
        pl.BlockSpec(cw.shape, const),
        pl.BlockSpec(gc.shape, const),
    ]
    args = [x2, win, wq, qn, kvn, cw, gc, cos, sin]
    if prompt:
        in_specs += [pl.BlockSpec((tm, LANES), lambda i: (i % tiles_per_seq, 0))] * 2
        scratch = [pltpu.VMEM((8, conv_ch), F32)]
    else:
        in_specs += [pl.BlockSpec((tm, LANES), const)] * 2
        in_specs += [pl.BlockSpec((tm, conv_ch), tile)] * 2
        args += list(prev)
        scratch = []
    qw = wq.shape[-1]
    out_shape = [
        jax.ShapeDtypeStruct((n, qw), q_dtype),
        jax.ShapeDtypeStruct((n, 2 * LANES), BF16),
        jax.ShapeDtypeStruct((n, kv_lora), F32),
        jax.ShapeDtypeStruct((n, rope), F32),
        jax.ShapeDtypeStruct((n, conv_ch), F32),
        jax.ShapeDtypeStruct((n, conv_ch), BF16),
    ]
    out_specs = [
        pl.BlockSpec((tm, qw), tile),
        pl.BlockSpec((tm, 2 * LANES), tile),
        pl.BlockSpec((tm, kv_lora), tile),
        pl.BlockSpec((tm, rope), tile),
        pl.BlockSpec((tm, conv_ch), tile),
        pl.BlockSpec((tm, conv_ch), tile),
    ]
    return pl.pallas_call(
        functools.partial(_mixer_in_kernel, prompt=prompt, tiles_per_seq=tiles_per_seq, tm=tm,
                          n_chunks=2 if tm % 32 == 0 else 1, t_new=seq,
                          n_heads=n_heads, conv_ch=conv_ch, q_lora=q_lora, kv_lora=kv_lora, rope=rope),
        grid=(n // tm,),
        in_specs=in_specs,
        out_specs=out_specs,
        out_shape=out_shape,
        scratch_shapes=scratch,
        compiler_params=_cparams(("arbitrary",)),
    )(*args)


MAX_KEY_BLOCKS = 4

def _softmax_update(s, v, m_ref, l_ref, acc_ref):
    m_prev = m_ref[...]
    m_new = jnp.maximum(m_prev, jnp.max(s, axis=1, keepdims=True))
    alpha = jnp.exp2(m_prev - m_new)
    p = jnp.exp2(s - m_new)
    l_ref[...] = alpha * l_ref[...] + jnp.sum(p, axis=1, keepdims=True)
    acc_ref[...] = alpha * acc_ref[...] + jnp.dot(p.astype(BF16), v, preferred_element_type=F32)
    m_ref[...] = m_new


def _prompt_attn_kernel(q_ref, k_ref, o_ref, qs_ref, m_ref, l_ref, acc_ref, *, tq, n_heads, chunk_heads):
    i = pl.program_id(1)
    cr = chunk_heads * tq
    n_chunks = n_heads // chunk_heads
    for hd in range(n_heads):
        qs_ref[hd * tq:(hd + 1) * tq, :] = q_ref[:, hd * 2 * LANES:(hd + 1) * 2 * LANES]
    m_ref[...] = jnp.full(m_ref.shape, NEG_INF, F32)
    l_ref[...] = jnp.zeros(l_ref.shape, F32)
    acc_ref[...] = jnp.zeros(acc_ref.shape, F32)

    def step(j, width, masked):
        k = k_ref[pl.ds(pl.multiple_of(j * tq, tq), width), :]
        v = k[:, :LANES]
        for c in range(n_chunks):
            rows = slice(c * cr, (c + 1) * cr)
            s = lax.dot_general(qs_ref[rows, :], k, _NT, preferred_element_type=F32)
            if masked:
                t = lax.broadcasted_iota(jnp.int32, s.shape, 0) & (tq - 1)
                col = lax.broadcasted_iota(jnp.int32, s.shape, 1)
                s = jnp.where(col <= t, s, NEG_INF)
            parts = [s[:, w * LANES:(w + 1) * LANES] for w in range(width // LANES)]
            mc = parts[0]
            for part in parts[1:]:
                mc = jnp.maximum(mc, part)
            m_prev = m_ref[rows, :]
            m_new = jnp.maximum(m_prev, jnp.max(mc, axis=1, keepdims=True))
            alpha = jnp.exp2(m_prev - m_new)
            ps = [jnp.exp2(part - m_new) for part in parts]
            psum = ps[0]
            for pp in ps[1:]:
                psum = psum + pp
            l_ref[rows, :] = alpha * l_ref[rows, :] + psum
            p = jnp.concatenate(ps, axis=1).astype(BF16)
            acc_ref[rows, :] = alpha * acc_ref[rows, :] + jnp.dot(p, v, preferred_element_type=F32)
            m_ref[rows, :] = m_new

    nq = k_ref.shape[0] // tq
    wmax = 1
    while wmax * 2 <= min(MAX_KEY_BLOCKS, nq - 1):
        wmax *= 2

    def body(jj, carry):
        step(wmax * jj, wmax * tq, False)
        return carry

    lax.fori_loop(0, i // wmax, body, 0)
    w = wmax // 2
    while w >= 1:
        @pl.when((i // w) % 2 == 1)
        def _(w=w):
            step((i // (2 * w)) * (2 * w), w * tq, False)
        w //= 2

    step(i, tq, True)
    out = acc_ref[...] / jnp.sum(l_ref[...], axis=1, keepdims=True)
    for hd in range(n_heads):
        o_ref[:, hd * LANES:(hd + 1) * LANES] = out[hd * tq:(hd + 1) * tq, :].astype(o_ref.dtype)


def _prompt_attention(q, kc, *, batch, seq, tq, n_heads):
    n = q.shape[0]
    nq = seq // tq
    rows = n_heads * tq
    chunk_heads = 2 if n_heads % 2 == 0 else 1
    assert tq & (tq - 1) == 0 and seq % tq == 0 and tq % LANES == 0
    return pl.pallas_call(
        functools.partial(_prompt_attn_kernel, tq=tq, n_heads=n_heads, chunk_heads=chunk_heads),
        grid=(batch, nq),
        in_specs=[pl.BlockSpec((tq, q.shape[1]), lambda b, i: (b * nq + i, 0)),
                  pl.BlockSpec((seq, kc.shape[1]), lambda b, i: (b, 0))],
        out_specs=pl.BlockSpec((tq, n_heads * LANES), lambda b, i: (b * nq + i, 0)),
        out_shape=jax.ShapeDtypeStruct((n, n_heads * LANES), BF16),
        scratch_shapes=[pltpu.VMEM((rows, 2 * LANES), BF16),
                        pltpu.VMEM((rows, LANES), F32),
                        pltpu.VMEM((rows, LANES), F32),
                        pltpu.VMEM((rows, LANES), F32)],
        compiler_params=_cparams(("parallel", "arbitrary")),
    )(q, kc)


def _sample_attn_kernel(pt_ref, q_ref, ckvn_ref, kpen_ref, kv_hbm, rp_hbm, o_ref, kvbuf, rpbuf, qs_ref, m_ref, l_ref,
                        acc_ref, sem, *, layer, n_pages, cpages, n_heads, t_new, rope, page):
    b = pl.program_id(0)
    nb = pl.num_programs(0)
    slot = b % 2

    def fetch(seq, s):
        def body(p, carry):
            pg = pt_ref[seq, p]
            pltpu.make_async_copy(kv_hbm.at[layer, pg], kvbuf.at[s, p], sem.at[0, s]).start()
            pltpu.make_async_copy(rp_hbm.at[layer, pg], rpbuf.at[s, p], sem.at[1, s]).start()
            return carry
        lax.fori_loop(0, n_pages, body, 0)

    @pl.when(b == 0)
    def _():
        fetch(0, 0)

    @pl.when(b + 1 < nb)
    def _():
        fetch(b + 1, 1 - slot)

    for hd in range(n_heads):
        qs_ref[hd * t_new:(hd + 1) * t_new, :] = q_ref[:, hd * 2 * LANES:(hd + 1) * 2 * LANES]
    m_ref[...] = jnp.full(m_ref.shape, NEG_INF, F32)
    l_ref[...] = jnp.zeros(l_ref.shape, F32)
    acc_ref[...] = jnp.zeros(acc_ref.shape, F32)
    qs = qs_ref[...]
    ql = qs[:, :LANES].astype(BF16)
    qp = qs[:, LANES:LANES + rope].astype(BF16)

    pltpu.make_async_copy(kv_hbm.at[layer, pl.ds(0, n_pages)], kvbuf.at[slot], sem.at[0, slot]).wait()
    pltpu.make_async_copy(rp_hbm.at[layer, pl.ds(0, n_pages)], rpbuf.at[slot], sem.at[1, slot]).wait()
    for c in range(n_pages // cpages):
        kv = kvbuf[slot, c * cpages:(c + 1) * cpages].reshape(cpages * page, LANES).astype(BF16)
        rp = jnp.concatenate([rpbuf[slot, c * cpages + p] for p in range(cpages)], axis=1).astype(BF16)
        s = (lax.dot_general(ql, kv, _NT, preferred_element_type=F32)
             + jnp.dot(qp, rp, preferred_element_type=F32))
        _softmax_update(s, kv, m_ref, l_ref, acc_ref)

    kn = jnp.concatenate([ckvn_ref[...], jnp.zeros((page - t_new, LANES), F32)], axis=0).astype(BF16)
    rn = jnp.concatenate([kpen_ref[...], jnp.zeros((page - t_new, rope), F32)], axis=0).astype(BF16)
    s2 = (lax.dot_general(ql, kn, _NT, preferred_element_type=F32)
          + lax.dot_general(qp, rn, _NT, preferred_element_type=F32))
    t = lax.broadcasted_iota(jnp.int32, s2.shape, 0) & (t_new - 1)
    col = lax.broadcasted_iota(jnp.int32, s2.shape, 1)
    s2 = jnp.where(col <= t, s2, NEG_INF)
    _softmax_update(s2, kn, m_ref, l_ref, acc_ref)
    out = acc_ref[...] / l_ref[...]
    for hd in range(n_heads):
        o_ref[:, hd * LANES:(hd + 1) * LANES] = out[hd * t_new:(hd + 1) * t_new, :]


def _sample_attention(q, ckv_new, kpe_new, cache_kv, cache_rope_t, page_table, layer, *, n_heads, t_new, cpages):
    n = q.shape[0]
    db = n // t_new
    n_pages = page_table.shape[1]
    page = cache_kv.shape[2]
    rope = cache_rope_t.shape[2]
    assert n_pages % cpages == 0 and t_new == 8 and cache_kv.shape[-1] == LANES and cache_rope_t.shape[3] == page
    rows = n_heads * t_new
    per_seq = lambda b, pt: (b, 0)
    return pl.pallas_call(
        functools.partial(_sample_attn_kernel, layer=layer, n_pages=n_pages, cpages=cpages, n_heads=n_heads,
                          t_new=t_new, rope=rope, page=page),
        grid_spec=pltpu.PrefetchScalarGridSpec(
            num_scalar_prefetch=1,
            grid=(db,),
            in_specs=[pl.BlockSpec((t_new, q.shape[1]), per_seq),
                      pl.BlockSpec((t_new, LANES), per_seq),
                      pl.BlockSpec((t_new, rope), per_seq),
                      pl.BlockSpec(memory_space=pl.ANY),
                      pl.BlockSpec(memory_space=pl.ANY)],
            out_specs=pl.BlockSpec((t_new, n_heads * LANES), per_seq),
            scratch_shapes=[pltpu.VMEM((2, n_pages, page, LANES), cache_kv.dtype),
                            pltpu.VMEM((2, n_pages, rope, page), cache_rope_t.dtype),
                            pltpu.VMEM((rows, 2 * LANES), F32),
                            pltpu.VMEM((rows, 1), F32),
                            pltpu.VMEM((rows, 1), F32),
                            pltpu.VMEM((rows, LANES), F32),
                            pltpu.SemaphoreType.DMA((2, 2))]),
        out_shape=jax.ShapeDtypeStruct((n, n_heads * LANES), F32),
        compiler_params=_cparams(("arbitrary",)),
    )(page_table, q, ckv_new, kpe_new, cache_kv, cache_rope_t)


def _mixer_out_kernel(xp_ref, xs_ref, ycp_ref, ycs_ref, atp_ref, ats_ref, wuv_ref, ga_ref, wo_ref, g_ref, b_ref,
                      x1_ref, x1p_ref, *, n_prompt_tiles, conv_ch, alpha):
    is_p = pl.program_id(0) < n_prompt_tiles
    tm = x1_ref.shape[0]
    sub = x1p_ref.shape[0] // tm
    n_chunks = 2 if tm % 32 == 0 else 1
    cm = tm // n_chunks
    for c in range(n_chunks):
        rows = slice(c * cm, (c + 1) * cm)
        x = jnp.where(is_p, xp_ref[rows, :], xs_ref[rows, :])
        yc = jnp.where(is_p, ycp_ref[rows, :], ycs_ref[rows, :])
        at = jnp.where(is_p, atp_ref[rows, :], ats_ref[rows, :].astype(BF16))
        o = jnp.dot(at, wuv_ref[...], preferred_element_type=F32)
        ya = _rms(o, ga_ref[...]).astype(BF16)
        mix = (jnp.dot(yc, wo_ref[:conv_ch, :], preferred_element_type=F32)
               + jnp.dot(ya, wo_ref[conv_ch:, :], preferred_element_type=F32))
        x1 = _layer_norm(alpha * x + mix, g_ref[...], b_ref[...])
        x1_ref[rows, :] = x1
        _pack_rows(x1, x1p_ref.at[pl.ds(c * cm * sub, cm * sub)])


def _mixer_out(xp, xs, ycp, ycs, atp, ats, wuv_bd, ga, wo, g, b, *, tm, alpha):
    np_, d = xp.shape
    ns = xs.shape[0]
    assert np_ % tm == 0 and ns % tm == 0
    npt, nst = np_ // tm, ns // tm
    conv_ch = ycp.shape[1]
    p_map = lambda i: (jnp.minimum(i, npt - 1), 0)
    s_map = lambda i: (jnp.maximum(i - npt, 0), 0)
    const = lambda i: (0, 0)
    tile = lambda i: (i, 0)
    nt = np_ + ns
    sub = d // (2 * LANES)
    return pl.pallas_call(
        functools.partial(_mixer_out_kernel, n_prompt_tiles=npt, conv_ch=conv_ch, alpha=alpha),
        grid=(npt + nst,),
        in_specs=[pl.BlockSpec((tm, d), p_map), pl.BlockSpec((tm, d), s_map),
                  pl.BlockSpec((tm, conv_ch), p_map), pl.BlockSpec((tm, conv_ch), s_map),
                  pl.BlockSpec((tm, atp.shape[1]), p_map), pl.BlockSpec((tm, ats.shape[1]), s_map),
                  pl.BlockSpec(wuv_bd.shape, const), pl.BlockSpec(ga.shape, const),
                  pl.BlockSpec(wo.shape, const), pl.BlockSpec(g.shape, const), pl.BlockSpec(b.shape, const)],
        out_specs=[pl.BlockSpec((tm, d), tile), pl.BlockSpec((tm * sub, LANES), tile)],
        out_shape=[jax.ShapeDtypeStruct((nt, d), F32), jax.ShapeDtypeStruct((nt * sub, LANES), U32)],
        compiler_params=_cparams(("parallel",)),
    )(xp, xs, ycp, ycs, atp, ats, wuv_bd, ga, wo, g, b)


def _router_kernel(x_ref, wh_ref, wl_ref, b_ref, tri_ref, idx_ref, w_ref, rank_ref, cnt_ref, base_ref,
                   *, tt, n_exp):
    @pl.when(pl.program_id(0) == 0)
    def _():
        base_ref[...] = jnp.zeros(base_ref.shape, F32)

    x = x_ref[...]
    xh = x.astype(BF16)
    xl = (x - xh.astype(F32)).astype(BF16)
    wh = wh_ref[...]
    z = (lax.dot_general(wh, xh, _NT, preferred_element_type=F32)
         + lax.dot_general(wh, xl, _NT, preferred_element_type=F32)
         + lax.dot_general(wl_ref[...], xh, _NT, preferred_element_type=F32))
    s = 1.0 / (1.0 + jnp.exp(-z))
    sb = s + b_ref[...]

    gsz = n_exp // N_GROUPS
    git = lax.broadcasted_iota(jnp.int32, (gsz, tt), 0).astype(F32)
    blocks, gscore = [], []
    for g in range(N_GROUPS):
        blk = sb[g * gsz:(g + 1) * gsz, :]
        m1 = jnp.max(blk, axis=0, keepdims=True)
        f1 = jnp.min(jnp.where(blk == m1, git, float(gsz)), axis=0, keepdims=True)
        m2 = jnp.max(jnp.where(git == f1, -jnp.inf, blk), axis=0, keepdims=True)
        blocks.append(blk)
        gscore.append(m1 + m2)
    masked = []
    for g in range(N_GROUPS):
        ahead = jnp.zeros((1, tt), F32)
        for g2 in range(N_GROUPS):
            if g2 == g:
                continue
            beats = (gscore[g2] >= gscore[g]) if g2 < g else (gscore[g2] > gscore[g])
            ahead = ahead + jnp.where(beats, 1.0, 0.0)
        masked.append(jnp.where(ahead < float(TOPK_GROUPS), blocks[g], -jnp.inf))
    vals = jnp.concatenate(masked, axis=0)

    rowi = lax.broadcasted_iota(jnp.int32, (n_exp, tt), 0).astype(F32)
    vals0 = vals
    picks, wks = [], []
    for k in range(TOP_K):
        m = jnp.max(vals, axis=0, keepdims=True)
        ik = jnp.min(jnp.where(vals == m, rowi, float(n_exp)), axis=0, keepdims=True)
        hit = rowi == ik
        wks.append(jnp.sum(jnp.where(hit, s, 0.0), axis=0, keepdims=True))
        vals = jnp.where(hit, -jnp.inf, vals)
        picks.append(ik)
    chosen = jnp.where(vals != vals0, 1.0, 0.0)
    wsum = wks[0]
    for k in range(1, TOP_K):
        wsum = wsum + wks[k]

    incl = jnp.dot(chosen.astype(BF16), tri_ref[...], preferred_element_type=F32)
    rnk = base_ref[...] + (incl - chosen)
    for k in range(TOP_K):
        idx_ref[k:k + 1, :] = picks[k].astype(jnp.int32)
        w_ref[k:k + 1, :] = wks[k] / wsum * ROUTED_SCALE
        rk = jnp.sum(jnp.where(rowi == picks[k], rnk, 0.0), axis=0, keepdims=True)
        rank_ref[k:k + 1, :] = rk.astype(jnp.int32)
    base = base_ref[...] + jnp.sum(chosen, axis=1, keepdims=True)
    base_ref[...] = base
    cnt_ref[...] = jnp.broadcast_to(base, cnt_ref.shape)


def _router(x1, wr_hi, wr_lo, bias, *, tt):
    nt, d = x1.shape
    n_exp = wr_hi.shape[0]
    assert nt % tt == 0
    tri = jnp.triu(jnp.ones((tt, tt), BF16))
    const = lambda i: (0, 0)
    col = lambda i: (0, i)
    return pl.pallas_call(
        functools.partial(_router_kernel, tt=tt, n_exp=n_exp),
        grid=(nt // tt,),
        in_specs=[pl.BlockSpec((tt, d), lambda i: (i, 0)),
                  pl.BlockSpec(wr_hi.shape, const), pl.BlockSpec(wr_lo.shape, const),
                  pl.BlockSpec(bias.shape, const), pl.BlockSpec(tri.shape, const)],
        out_specs=[pl.BlockSpec((TOP_K, tt), col), pl.BlockSpec((TOP_K, tt), col),
                   pl.BlockSpec((TOP_K, tt), col), pl.BlockSpec((n_exp, LANES), const)],
        out_shape=[jax.ShapeDtypeStruct((TOP_K, nt), jnp.int32), jax.ShapeDtypeStruct((TOP_K, nt), F32),
                   jax.ShapeDtypeStruct((TOP_K, nt), jnp.int32), jax.ShapeDtypeStruct((n_exp, LANES), F32)],
        scratch_shapes=[pltpu.VMEM((n_exp, 1), F32)],
        compiler_params=_cparams(("arbitrary",)),
    )(x1, wr_hi, wr_lo, bias, tri)


def _slot_rows_kernel(idx_ref, rank_ref, ps_ref, o_ref, *, n_exp):
    tt = idx_ref.shape[1]
    rowi = lax.broadcasted_iota(jnp.int32, (n_exp, tt), 0)
    ps = ps_ref[...]
    for k in range(TOP_K):
        start = jnp.sum(jnp.where(rowi == idx_ref[k:k + 1, :], ps, 0.0), axis=0, keepdims=True)
        o_ref[k:k + 1, :] = start.astype(jnp.int32) + rank_ref[k:k + 1, :]


def _slot_rows(idx_t, rank_t, pad_start, *, tt):
    nt = idx_t.shape[1]
    n_exp = pad_start.shape[0]
    assert nt % tt == 0 and n_exp * SLOT_ROWS + nt * TOP_K < 2 ** 24
    col = lambda i: (0, i)
    return pl.pallas_call(
        functools.partial(_slot_rows_kernel, n_exp=n_exp),
        grid=(nt // tt,),
        in_specs=[pl.BlockSpec((TOP_K, tt), col), pl.BlockSpec((TOP_K, tt), col),
                  pl.BlockSpec((n_exp, 1), lambda i: (0, 0))],
        out_specs=pl.BlockSpec((TOP_K, tt), col),
        out_shape=jax.ShapeDtypeStruct((TOP_K, nt), jnp.int32),
        compiler_params=_cparams(("parallel",)),
    )(idx_t, rank_t, pad_start.astype(F32).reshape(n_exp, 1))


def _dispatch_kernel(ps_ref, nb_ref, cnt_ref, dest_ref, x_ref, o_ref, zbuf, sem, zsem, *, td, sub, n_exp):
    i = pl.program_id(0)
    blk = SLOT_ROWS * sub

    @pl.when(i == 0)
    def _():
        zbuf[...] = jnp.zeros(zbuf.shape, zbuf.dtype)
        n_blocks = o_ref.shape[0] // blk
        used = ps_ref[n_exp - 1] // SLOT_ROWS + nb_ref[n_exp - 1]

        def zero_block(b):
            return pltpu.make_async_copy(zbuf, o_ref.at[pl.ds(pl.multiple_of(b * blk, blk), blk)], zsem)

        def partial(e):
            return (cnt_ref[e] & (SLOT_ROWS - 1)) != 0

        def fill(e, carry):
            @pl.when(partial(e))
            def _():
                zero_block(ps_ref[e] // SLOT_ROWS + nb_ref[e] - 1).start()
            return carry

        def fill_done(e, carry):
            @pl.when(partial(e))
            def _():
                zero_block(0).wait()
            return carry

        def tail(b, carry):
            zero_block(b).start()
            return carry

        def tail_done(b, carry):
            zero_block(0).wait()
            return carry

        lax.fori_loop(0, n_exp, fill, 0)
        lax.fori_loop(used, n_blocks, tail, 0)
        lax.fori_loop(0, n_exp, fill_done, 0)
        lax.fori_loop(used, n_blocks, tail_done, 0)

    def per_token(t, carry):
        src = x_ref.at[pl.ds(pl.multiple_of(t * sub, sub), sub)]
        for k in range(TOP_K):
            dst = pl.multiple_of(dest_ref[k, t] * sub, sub)
            pltpu.make_async_copy(src, o_ref.at[pl.ds(dst, sub)], sem).start(priority=k % 2)
        return carry

    lax.fori_loop(0, td, per_token, 0)
    for _ in range(TOP_K):
        pltpu.make_async_copy(x_ref, o_ref.at[pl.ds(0, td * sub)], sem).wait()


def _dispatch(pad_start, n_blk, counts, dest_t, x1p, n_slots, *, td, sub):
    nt = x1p.shape[0] // sub
    assert nt % td == 0 and n_slots >= td and n_slots % SLOT_ROWS == 0
    return pl.pallas_call(
        functools.partial(_dispatch_kernel, td=td, sub=sub, n_exp=counts.shape[0]),
        grid_spec=pltpu.PrefetchScalarGridSpec(
            num_scalar_prefetch=3,
            grid=(nt // td,),
            in_specs=[pl.BlockSpec((TOP_K, td), lambda i, *_: (0, i), memory_space=pltpu.SMEM),
                      pl.BlockSpec((td * sub, LANES), lambda i, *_: (i, 0))],
            out_specs=pl.BlockSpec(memory_space=pl.ANY),
            scratch_shapes=[pltpu.VMEM((SLOT_ROWS * sub, LANES), x1p.dtype), pltpu.SemaphoreType.DMA(()),
                            pltpu.SemaphoreType.DMA(())]),
        out_shape=jax.ShapeDtypeStruct((n_slots * sub, LANES), x1p.dtype),
        compiler_params=_cparams(("arbitrary",)),
    )(pad_start, n_blk, counts, dest_t, x1p)


EXPERT_RING = 8


def _experts_kernel(ps_ref, nb_ref, wg_ref, wu_ref, wd_ref, xs_hbm, ys_hbm, wg_s, wu_s, wd_s, xbuf, ybuf,
                    xsem, ysem, zsem, *, sub, n_exp):
    e = pl.program_id(0)
    blk = SLOT_ROWS * sub
    n_blocks = xs_hbm.shape[0] // blk
    used = ps_ref[n_exp - 1] // SLOT_ROWS + nb_ref[n_exp - 1]

    def rows(g):
        return pl.ds(pl.multiple_of(g * blk, blk), blk)

    def ring(buf, s, n=1):
        return buf.at[pl.ds(pl.multiple_of(s * blk, blk), n * blk)]

    def x_copy(g, s):
        return pltpu.make_async_copy(xs_hbm.at[rows(g)], ring(xbuf, s), xsem.at[s])

    def y_copy(g, s):
        return pltpu.make_async_copy(ring(ybuf, s), ys_hbm.at[rows(g)], ysem.at[s])

    @pl.when(e == 0)
    def _():
        zbuf = ring(ybuf, EXPERT_RING - 1)
        zbuf[...] = jnp.zeros(zbuf.shape, zbuf.dtype)

        def fill(b, carry):
            pltpu.make_async_copy(zbuf, ys_hbm.at[rows(b)], zsem).start()
            return carry

        def fill_done(b, carry):
            pltpu.make_async_copy(zbuf, ys_hbm.at[rows(0)], zsem).wait()
            return carry

        lax.fori_loop(used, n_blocks, fill, 0)
        lax.fori_loop(used, n_blocks, fill_done, 0)
        for s in range(EXPERT_RING):
            @pl.when(s < used)
            def _():
                x_copy(s, s).start()

    nb = nb_ref[e]
    first = ps_ref[e] // SLOT_ROWS

    @pl.when(nb > 0)
    def _():
        wg_s[...] = wg_ref[...].astype(BF16)
        wu_s[...] = wu_ref[...].astype(BF16)
        wd_s[...] = wd_ref[...].astype(BF16)

        def run(g, n):
            s = g & (EXPERT_RING - 1)
            for d in range(n):
                x_copy(g + d, s + d).wait()

            @pl.when(g >= EXPERT_RING)
            def _():
                for d in range(n):
                    y_copy(g + d, s + d).wait()

            lo, hi = _unpack_rows(ring(xbuf, s, n), 0, n * SLOT_ROWS, sub)
            x = jnp.concatenate(lo + hi, axis=1).astype(BF16)
            gate = jnp.dot(x, wg_s[...], preferred_element_type=F32)
            up = jnp.dot(x, wu_s[...], preferred_element_type=F32)
            hmid = (_silu(gate) * up).astype(BF16)
            _pack_rows(jnp.dot(hmid, wd_s[...], preferred_element_type=F32), ring(ybuf, s, n))
            for d in range(n):
                y_copy(g + d, s + d).start()
            for d in range(n):
                @pl.when(g + d + EXPERT_RING < used)
                def _():
                    x_copy(g + d + EXPERT_RING, s + d).start()

        lead = first & 1

        @pl.when(lead == 1)
        def _():
            run(first, 1)

        def pair(p, carry):
            run(first + lead + 2 * p, 2)
            return carry

        lax.fori_loop(0, (nb - lead) // 2, pair, 0)

        @pl.when(((nb - lead) & 1) == 1)
        def _():
            run(first + nb - 1, 1)

    @pl.when(e == n_exp - 1)
    def _():
        for s in range(EXPERT_RING):
            @pl.when(s < used)
            def _():
                y_copy(0, s).wait()


def _experts(pad_start, n_blk, xs, w_gate, w_up, w_down, layer):
    n_exp, d, de = w_gate.shape[-3:]
    sub = d // (2 * LANES)
    blk = SLOT_ROWS * sub
    assert xs.shape[0] % blk == 0 and xs.shape[1] == LANES
    return pl.pallas_call(
        functools.partial(_experts_kernel, sub=sub, n_exp=n_exp),
        grid_spec=pltpu.PrefetchScalarGridSpec(
            num_scalar_prefetch=2,
            grid=(n_exp,),
            in_specs=[pl.BlockSpec((None, None, d, de), lambda e, ps, nb: (layer, e, 0, 0)),
                      pl.BlockSpec((None, None, d, de), lambda e, ps, nb: (layer, e, 0, 0)),
                      pl.BlockSpec((None, None, de, d), lambda e, ps, nb: (layer, e, 0, 0)),
                      pl.BlockSpec(memory_space=pl.ANY)],
            out_specs=pl.BlockSpec(memory_space=pl.ANY),
            scratch_shapes=[pltpu.VMEM((d, de), BF16), pltpu.VMEM((d, de), BF16), pltpu.VMEM((de, d), BF16),
                            pltpu.VMEM((EXPERT_RING * blk, LANES), U32), pltpu.VMEM((EXPERT_RING * blk, LANES), U32),
                            pltpu.SemaphoreType.DMA((EXPERT_RING,)), pltpu.SemaphoreType.DMA((EXPERT_RING,)),
                            pltpu.SemaphoreType.DMA(())]),
        out_shape=jax.ShapeDtypeStruct(xs.shape, U32),
        compiler_params=_cparams(("arbitrary",)),
    )(pad_start, n_blk, w_gate, w_up, w_down, xs)


def _combine_kernel(dest0_ref, dest1_ref, w_ref, x1_ref, wsg_ref, wsu_ref,
                    wsd_ref, g_ref, b_ref, ys_ref, yp_ref, ysm_ref, buf, sem, *, tc, alpha, sub, n_prompt_tiles):
    i = pl.program_id(0)
    n = pl.num_programs(0)
    slot = i % 2

    def gather_row(dest_ref, s, k, t):
        src = pl.multiple_of(dest_ref[k, t] * sub, sub)
        dst = (k * tc + t) * sub
        if not isinstance(dst, int):
            dst = pl.multiple_of(dst, sub)
        pltpu.make_async_copy(ys_ref.at[pl.ds(src, sub)], buf.at[s, pl.ds(dst, sub)], sem.at[s]).start(priority=k % 2)

    @pl.when(i == 0)
    def _():
        def per_token(t, carry):
            for k in range(TOP_K):
                gather_row(dest0_ref, 0, k, t)
            return carry
        lax.fori_loop(0, tc, per_token, 0)

    for t in range(tc):
        for k in range(TOP_K):
            gather_row(dest1_ref, 1 - slot, k, t)

    x1 = x1_ref[...]
    xb = x1.astype(BF16)
    hs = (_silu(jnp.dot(xb, wsg_ref[...], preferred_element_type=F32))
          * jnp.dot(xb, wsu_ref[...], preferred_element_type=F32)).astype(BF16)
    shared = jnp.dot(hs, wsd_ref[...], preferred_element_type=F32)

    def drain(s):
        pltpu.make_async_copy(ys_ref.at[pl.ds(0, TOP_K * tc * sub)], buf.at[s], sem.at[s]).wait()

    drain(slot)
    w = w_ref[...]
    lo_acc = [jnp.zeros((tc, LANES), F32) for _ in range(sub)]
    hi_acc = [jnp.zeros((tc, LANES), F32) for _ in range(sub)]
    for k in range(TOP_K):
        lo, hi = _unpack_rows(buf.at[slot], k * tc, tc, sub)
        wk = w[:, k:k + 1]
        for j in range(sub):
            lo_acc[j] = lo_acc[j] + wk * lo[j]
            hi_acc[j] = hi_acc[j] + wk * hi[j]
    moe = jnp.concatenate(lo_acc + hi_acc, axis=1) + shared
    y = _layer_norm(alpha * x1 + moe, g_ref[...], b_ref[...])

    @pl.when(i < n_prompt_tiles)
    def _():
        yp_ref[...] = y

    @pl.when(i >= n_prompt_tiles)
    def _():
        ysm_ref[...] = y

    @pl.when(i == n - 1)
    def _():
        drain(1 - slot)


def _combine(dest_t, w_tok, x1, wsg, wsu, wsd, g, b, ys, *, tc, alpha, n_prompt):
    nt, d = x1.shape
    n_tiles = nt // tc
    sub = d // (2 * LANES)
    assert nt % tc == 0 and n_prompt % tc == 0 and 0 < n_prompt < nt and ys.shape[0] >= TOP_K * tc * sub
    npt = n_prompt // tc
    cur = pl.BlockSpec((TOP_K, tc), lambda i: (0, i), memory_space=pltpu.SMEM)
    nxt = pl.BlockSpec((TOP_K, tc), lambda i: (0, jnp.minimum(i + 1, n_tiles - 1)), memory_space=pltpu.SMEM)
    const = lambda i: (0, 0)
    tile = lambda i: (i, 0)
    return pl.pallas_call(
        functools.partial(_combine_kernel, tc=tc, alpha=alpha, sub=sub, n_prompt_tiles=npt),
        grid=(n_tiles,),
        in_specs=[cur, nxt,
                  pl.BlockSpec((tc, TOP_K), tile),
                  pl.BlockSpec((tc, d), tile),
                  pl.BlockSpec(wsg.shape, const), pl.BlockSpec(wsu.shape, const), pl.BlockSpec(wsd.shape, const),
                  pl.BlockSpec(g.shape, const), pl.BlockSpec(b.shape, const),
                  pl.BlockSpec(memory_space=pl.ANY)],
        out_specs=[pl.BlockSpec((tc, d), lambda i: (jnp.minimum(i, npt - 1), 0)),
                   pl.BlockSpec((tc, d), lambda i: (jnp.maximum(i - npt, 0), 0))],
        out_shape=[jax.ShapeDtypeStruct((n_prompt, d), F32), jax.ShapeDtypeStruct((nt - n_prompt, d), F32)],
        scratch_shapes=[pltpu.VMEM((2, TOP_K * tc * sub, LANES), ys.dtype), pltpu.SemaphoreType.DMA((2,))],
        compiler_params=_cparams(("arbitrary",)),
    )(dest_t, dest_t, w_tok, x1, wsg, wsu, wsd, g, b, ys)


def _rope_cs(pos, rope):
    half = rope // 2
    inv = ROPE_BASE ** (-jnp.arange(half, dtype=F32) / half)
    ang = pos.astype(F32)[:, None] * inv[None, :]
    cos, sin = jnp.cos(ang), jnp.sin(ang)
    pad = jnp.zeros((pos.shape[0], LANES - rope), F32)
    return (jnp.concatenate([cos, cos, pad], axis=1), jnp.concatenate([-sin, sin, pad], axis=1))


def _pick_tile(n, pref):
    t = pref
    while n % t:
        t //= 2
    return t


def kernel(x_prompt, x_sample, cache_kv_latent, cache_k_rope, state_conv, page_table, w_in, conv_w, q_norm, w_uq, kv_norm, w_uk, w_uv, g_conv, g_attn, w_o, ln1_g, ln1_b, w_router, router_bias, w_gate, w_up, w_down, ws_gate, ws_up, ws_down, ln2_g, ln2_b):
    depth = w_in.shape[0]
    bsz, seq, d = x_prompt.shape
    db, t_new, _ = x_sample.shape
    kv_lora, n_heads, nope = w_uk.shape[1:]
    v_dim = w_uv.shape[-1]
    rope = cache_k_rope.shape[-1]
    q_lora = q_norm.shape[-1]
    conv_ch = conv_w.shape[-1]
    n_exp = w_router.shape[-1]
    page = cache_kv_latent.shape[2]
    past_len = page_table.shape[1] * page
    alpha = (2.0 * depth) ** 0.25
    scale = float((nope + rope) ** -0.5) * math.log2(math.e)
    in_cols = w_in.shape[-1]
    in_pad = -(-(in_cols - rope + LANES) // LANES) * LANES
    n_p, n_s = bsz * seq, db * t_new
    nt = n_p + n_s

    tm_p = _pick_tile(seq, 512)
    tm_s = _pick_tile(n_s, 512)
    tq = _pick_tile(seq, 256)
    tm_o = _pick_tile(math.gcd(n_p, n_s), 512)
    tt = _pick_tile(math.gcd(n_p, n_s), 512)
    td = _pick_tile(nt, 1024)
    tc = _pick_tile(math.gcd(n_p, n_s), 256)
    pps = _pick_tile(page_table.shape[1], 16)
    cache_rope_t = jnp.swapaxes(cache_k_rope, 2, 3)

    cos_p, sin_p = _rope_cs(jnp.arange(seq, dtype=jnp.int32), rope)
    cos_s, sin_s = _rope_cs(past_len + jnp.arange(t_new, dtype=jnp.int32), rope)
    cos_s, sin_s = jnp.tile(cos_s, (tm_s // t_new, 1)), jnp.tile(sin_s, (tm_s // t_new, 1))

    xp, xs_ = x_prompt.reshape(n_p, d), x_sample.reshape(n_s, d)
    outs = [[] for _ in range(6)]
    for l in range(depth):
        win = jnp.pad(w_in[l], ((0, 0), (0, in_pad - in_cols))).astype(BF16)
        wuq3 = w_uq[l].reshape(q_lora, n_heads, nope + rope)
        qlat = _fold_qlat(jnp.transpose(wuq3[:, :, :nope], (1, 0, 2)), jnp.transpose(w_uk[l], (1, 2, 0)), scale)
        wq_rope = jnp.transpose(wuq3[:, :, nope:], (1, 0, 2)) * scale
        wq = jnp.concatenate([qlat, wq_rope, jnp.zeros((n_heads, q_lora, LANES - rope), F32)], axis=2)
        wq = jnp.transpose(wq, (1, 0, 2)).reshape(q_lora, n_heads * 2 * LANES).astype(BF16)
        eye = jnp.eye(n_heads, dtype=F32)
        wuv_bd = jnp.einsum('chv,hg->hcgv', w_uv[l], eye).reshape(n_heads * kv_lora, n_heads * v_dim).astype(BF16)
        wo = w_o[l].astype(BF16)
        wr_t = w_router[l].T
        wr_hi = wr_t.astype(BF16)
        wr_lo = (wr_t - wr_hi.astype(F32)).astype(BF16)
        row = lambda v: v[l].reshape(1, -1)

        common = (win, wq, row(q_norm), row(kv_norm), conv_w[l], row(g_conv))
        q_p, kc_p, ckv_p, kpe_p, u_p, yc_p = _mixer_in(
            xp, *common, cos_p, sin_p, None, prompt=True, seq=seq, tm=tm_p, n_heads=n_heads, rope=rope, q_dtype=BF16)
        st = state_conv[l].astype(F32)
        zero = jnp.zeros((db, conv_ch), F32)
        prev1 = jnp.stack([st[:, 1]] + [zero] * (t_new - 1), axis=1).reshape(n_s, conv_ch)
        prev2 = jnp.stack([st[:, 0], st[:, 1]] + [zero] * (t_new - 2), axis=1).reshape(n_s, conv_ch)
        q_s, kc_s, ckv_s, kpe_s, u_s, yc_s = _mixer_in(
            xs_, *common, cos_s, sin_s, (prev1, prev2), prompt=False, seq=t_new, tm=tm_s, n_heads=n_heads, rope=rope,
            q_dtype=F32)
        del kc_s

        at_p = _prompt_attention(q_p, kc_p, batch=bsz, seq=seq, tq=tq, n_heads=n_heads)
        at_s = _sample_attention(q_s, ckv_s, kpe_s, cache_kv_latent, cache_rope_t, page_table, l,
                                 n_heads=n_heads, t_new=t_new, cpages=pps)

        x1, x1p = _mixer_out(xp, xs_, yc_p, yc_s, at_p, at_s, wuv_bd, row(g_attn), wo, row(ln1_g), row(ln1_b),
                             tm=tm_o, alpha=alpha)

        idx_t, w_t, rank_t, cnt = _router(x1, wr_hi, wr_lo, router_bias[l].reshape(n_exp, 1), tt=tt)
        counts = cnt[:, 0].astype(jnp.int32)
        padded = (counts + SLOT_ROWS - 1) // SLOT_ROWS * SLOT_ROWS
        pad_end = jnp.cumsum(padded)
        pad_start = (pad_end - padded).astype(jnp.int32)
        n_blk = (padded // SLOT_ROWS).astype(jnp.int32)
        n_blocks = -(-(nt * TOP_K) // SLOT_ROWS) + n_exp

        dest_t = _slot_rows(idx_t, rank_t, pad_start, tt=td)
        xs_sorted = _dispatch(pad_start, n_blk, counts, dest_t, x1p, n_blocks * SLOT_ROWS, td=td, sub=d // (2 * LANES))
        ys = _experts(pad_start, n_blk, xs_sorted, w_gate, w_up, w_down, l)
        xp, xs_ = _combine(dest_t, w_t.T, x1, ws_gate[l].astype(BF16), ws_up[l].astype(BF16),
                           ws_down[l].astype(BF16), row(ln2_g), row(ln2_b), ys, tc=tc, alpha=alpha, n_prompt=n_p)
        outs[0].append(ckv_p.reshape(bsz, seq, kv_lora))
        outs[1].append(kpe_p.reshape(bsz, seq, rope))
        outs[2].append(u_p.reshape(bsz, seq, conv_ch)[:, seq - (CONV_W - 1):])
        outs[3].append(ckv_s.reshape(db, t_new, kv_lora))
        outs[4].append(kpe_s.reshape(db, t_new, rope))
        outs[5].append(u_s.reshape(db, t_new, conv_ch)[:, t_new - (CONV_W - 1):])
    return (xp.reshape(bsz, seq, d), xs_.reshape(db, t_new, d)) + tuple(jnp.stack(o) for o in outs)
```

```python
import functools
import math

import jax
import jax.numpy as jnp
from jax import lax
from jax.experimental import pallas as pl
from jax.experimental.pallas import tpu as pltpu

F32 = jnp.float32
BF16 = jnp.bfloat16
U32 = jnp.uint32

ROPE_BASE = 10000.0
NORM_EPS = 1e-6
LN_EPS = 1e-5
NEG_INF = -1e30
TOP_K = 8
N_GROUPS = 8
TOPK_GROUPS = 4
ROUTED_SCALE = 2.5
CONV_W = 3

LANES = 128
SLOT_ROWS = 256
VMEM_LIMIT = 52 * 1024 * 1024

_NT = (((1,), (1,)), ((), ()))


def _cparams(sem):
    return pltpu.CompilerParams(dimension_semantics=sem, vmem_limit_bytes=VMEM_LIMIT)


def _rms(x, g):
    return x * lax.rsqrt(jnp.mean(x * x, axis=-1, keepdims=True) + NORM_EPS) * g


def _layer_norm(x, g, b):
    mu = jnp.mean(x, axis=-1, keepdims=True)
    xc = x - mu
    var = jnp.mean(xc * xc, axis=-1, keepdims=True)
    return xc * lax.rsqrt(var + LN_EPS) * g + b


def _silu(x):
    return x / (1.0 + jnp.exp(-x))


def _rope_tile(p, c, s, half):
    lane = lax.broadcasted_iota(jnp.int32, p.shape, 1)
    swapped = jnp.where(lane < half, pltpu.roll(p, LANES - half, axis=1), pltpu.roll(p, half, axis=1))
    return p * c + swapped * s


def _pack_rows(x, ref):
    r = x.shape[0]
    half = x.shape[1] // 2
    sub = half // LANES
    bits = lax.bitcast_convert_type(x.astype(BF16).astype(F32), U32)
    words = bits[:, half:] | (bits[:, :half] >> 16)
    for j in range(sub):
        ref[pl.ds(j, r, stride=sub), :] = words[:, j * LANES:(j + 1) * LANES]


def _unpack_rows(ref, first, r, sub):
    lo, hi = [], []
    for j in range(sub):
        w = ref[pl.ds(first * sub + j, r, stride=sub), :]
        lo.append(lax.bitcast_convert_type(w << 16, F32))
        hi.append(lax.bitcast_convert_type(w & jnp.uint32(0xFFFF0000), F32))
    return lo, hi


def _fold_kernel(a_ref, b_ref, o_ref, *, scale):
    o_ref[...] = jnp.dot(a_ref[...], b_ref[...], precision=lax.Precision.HIGHEST,
                         preferred_element_type=F32) * scale


def _fold_qlat(wuq_nope, wuk_t, scale):
    h, r, n = wuq_nope.shape
    c = wuk_t.shape[-1]
    return pl.pallas_call(
        functools.partial(_fold_kernel, scale=scale),
        grid=(h,),
        in_specs=[pl.BlockSpec((None, r, n), lambda i: (i, 0, 0)),
                  pl.BlockSpec((None, n, c), lambda i: (i, 0, 0))],
        out_specs=pl.BlockSpec((None, r, c), lambda i: (i, 0, 0)),
        out_shape=jax.ShapeDtypeStruct((h, r, c), F32),
        compiler_params=_cparams(("parallel",)),
    )(wuq_nope, wuk_t)


def _mixer_in_kernel(*refs, prompt, tiles_per_seq, tm, n_chunks, t_new, n_heads, conv_ch, q_lora, kv_lora, rope):
    if prompt:
        (x_ref, win_ref, wq_ref, qn_ref, kvn_ref, cw_ref, gc_ref, cos_ref, sin_ref,
         q_ref, kc_ref, ckv_ref, kpe_ref, u_ref, yc_ref, carry_ref) = refs
    else:
        (x_ref, win_ref, wq_ref, qn_ref, kvn_ref, cw_ref, gc_ref, cos_ref, sin_ref, p1_ref, p2_ref,
         q_ref, kc_ref, ckv_ref, kpe_ref, u_ref, yc_ref) = refs
    half = rope // 2
    c1, c2, c3 = conv_ch, 2 * conv_ch, 3 * conv_ch
    c4 = c3 + q_lora
    c5 = c4 + kv_lora

    cm = tm // n_chunks
    row = lax.broadcasted_iota(jnp.int32, (cm, 1), 0)
    u_prev = None
    for c in range(n_chunks):
        rows = slice(c * cm, (c + 1) * cm)
        h = jnp.dot(x_ref[rows, :].astype(BF16), win_ref[...], preferred_element_type=F32)
        b_gate, c_gate, x_conv = h[:, :c1], h[:, c1:c2], h[:, c2:c3]
        q_a, c_kv, kp = h[:, c3:c4], h[:, c4:c5], h[:, c5:c5 + LANES]

        u = c_gate * x_conv
        u_ref[rows, :] = u
        if prompt:
            if c == 0:
                first = (pl.program_id(0) % tiles_per_seq) == 0
                prev1 = jnp.where(first, 0.0, carry_ref[7:8, :])
                prev2 = jnp.where(first, 0.0, carry_ref[6:7, :])
            else:
                prev1 = u_prev[cm - 1:cm, :]
                prev2 = u_prev[cm - 2:cm - 1, :]
            t_in = row
            p1 = prev1
            p2 = jnp.where(row == 0, prev2, prev1)
        else:
            t_in = row & (t_new - 1)
            p1 = p1_ref[rows, :]
            p2 = p2_ref[rows, :]
        um1 = jnp.where(t_in == 0, p1, pltpu.roll(u, 1, axis=0))
        um2 = jnp.where(t_in < 2, p2, pltpu.roll(u, 2, axis=0))
        conv_y = cw_ref[0:1, :] * um2 + cw_ref[1:2, :] * um1 + cw_ref[2:3, :] * u
        yc_ref[rows, :] = _rms(b_gate * conv_y, gc_ref[...]).astype(yc_ref.dtype)
        u_prev = u

        cos = cos_ref[rows, :]
        sin = sin_ref[rows, :]
        qn = _rms(q_a, qn_ref[...]).astype(BF16)
        q = jnp.dot(qn, wq_ref[...], preferred_element_type=F32)
        for hd in range(n_heads):
            o = hd * 2 * LANES
            q_ref[rows, o:o + LANES] = q[:, o:o + LANES].astype(q_ref.dtype)
            q_ref[rows, o + LANES:o + 2 * LANES] = _rope_tile(q[:, o + LANES:o + 2 * LANES], cos, sin,
                                                              half).astype(q_ref.dtype)

        ckv = _rms(c_kv, kvn_ref[...])
        ckv_ref[rows, :] = ckv
        kpr = _rope_tile(kp, cos, sin, half)
        kpe_ref[rows, :] = kpr[:, :rope]
        kc_ref[rows, :LANES] = ckv.astype(BF16)
        kc_ref[rows, LANES:] = kpr.astype(BF16)
    if prompt:
        carry_ref[...] = u_prev[cm - 8:, :]


def _mixer_in(x2, win, wq, qn, kvn, cw, gc, cos, sin, prev, *, prompt, seq, tm, n_heads, rope, q_dtype):
    n, d = x2.shape
    conv_ch = cw.shape[-1]
    q_lora = qn.shape[-1]
    kv_lora = kvn.shape[-1]
    assert n % tm == 0 and kv_lora == LANES and tm % 8 == 0
    tiles_per_seq = seq // tm if prompt else 1
    if prompt:
        assert seq % tm == 0
    else:
        assert seq & (seq - 1) == 0 and tm % seq == 0
    const = lambda i: (0, 0)
    tile = lambda i: (i, 0)
    in_specs = [
        pl.BlockSpec((tm, d), tile),
        pl.BlockSpec(win.shape, const),
        pl.BlockSpec(wq.shape, const),
        pl.BlockSpec(qn.shape, const),
        pl.BlockSpec(kvn.shape, const),
        pl.BlockSpec(cw.shape, const),
        pl.BlockSpec(gc.shape, const),
    ]
    args = [x2, win, wq, qn, kvn, cw, gc, cos, sin]
    if prompt:
        in_specs += [pl.BlockSpec((tm, LANES), lambda i: (i % tiles_per_seq, 0))] * 2
        scratch = [pltpu.VMEM((8, conv_ch), F32)]
    else:
        in_specs += [pl.BlockSpec((tm, LANES), const)] * 2
        in_specs += [pl.BlockSpec((tm, conv_ch), tile)] * 2
        args += list(prev)
        scratch = []
    qw = wq.shape[-1]
    out_shape = [
        jax.ShapeDtypeStruct((n, qw), q_dtype),
        jax.ShapeDtypeStruct((n, 2 * LANES), BF16),
        jax.ShapeDtypeStruct((n, kv_lora), F32),
        jax.ShapeDtypeStruct((n, rope), F32),
        jax.ShapeDtypeStruct((n, conv_ch), F32),
        jax.ShapeDtypeStruct((n, conv_ch), BF16),
    ]
    out_specs = [
        pl.BlockSpec((tm, qw), tile),
        pl.BlockSpec((tm, 2 * LANES), tile),
        pl.BlockSpec((tm, kv_lora), tile),
        pl.BlockSpec((tm, rope), tile),
        pl.BlockSpec((tm, conv_ch), tile),
        pl.BlockSpec((tm, conv_ch), tile),
    ]
    return pl.pallas_call(
        functools.partial(_mixer_in_kernel, prompt=prompt, tiles_per_seq=tiles_per_seq, tm=tm,
                          n_chunks=2 if tm % 32 == 0 else 1, t_new=seq,
                          n_heads=n_heads, conv_ch=conv_ch, q_lora=q_lora, kv_lora=kv_lora, rope=rope),
        grid=(n // tm,),
        in_specs=in_specs,
        out_specs=out_specs,
        out_shape=out_shape,
        scratch_shapes=scratch,
        compiler_params=_cparams(("arbitrary",)),
    )(*args)


MAX_KEY_BLOCKS = 4

def _softmax_update(s, v, m_ref, l_ref, acc_ref):
    m_prev = m_ref[...]
    m_new = jnp.maximum(m_prev, jnp.max(s, axis=1, keepdims=True))
    alpha = jnp.exp2(m_prev - m_new)
    p = jnp.exp2(s - m_new)
    l_ref[...] = alpha * l_ref[...] + jnp.sum(p, axis=1, keepdims=True)
    acc_ref[...] = alpha * acc_ref[...] + jnp.dot(p.astype(BF16), v, preferred_element_type=F32)
    m_ref[...] = m_new


def _prompt_attn_kernel(q_ref, k_ref, o_ref, qs_ref, m_ref, l_ref, acc_ref, *, tq, n_heads, chunk_heads):
    i = pl.program_id(1)
    cr = chunk_heads * tq
    n_chunks = n_heads // chunk_heads
    for hd in range(n_heads):
        qs_ref[hd * tq:(hd + 1) * tq, :] = q_ref[:, hd * 2 * LANES:(hd + 1) * 2 * LANES]
    m_ref[...] = jnp.full(m_ref.shape, NEG_INF, F32)
    l_ref[...] = jnp.zeros(l_ref.shape, F32)
    acc_ref[...] = jnp.zeros(acc_ref.shape, F32)

    def step(j, width, masked):
        k = k_ref[pl.ds(pl.multiple_of(j * tq, tq), width), :]
        v = k[:, :LANES]
        for c in range(n_chunks):
            rows = slice(c * cr, (c + 1) * cr)
            s = lax.dot_general(qs_ref[rows, :], k, _NT, preferred_element_type=F32)
            if masked:
                t = lax.broadcasted_iota(jnp.int32, s.shape, 0) & (tq - 1)
                col = lax.broadcasted_iota(jnp.int32, s.shape, 1)
                s = jnp.where(col <= t, s, NEG_INF)
            parts = [s[:, w * LANES:(w + 1) * LANES] for w in range(width // LANES)]
            mc = parts[0]
            for part in parts[1:]:
                mc = jnp.maximum(mc, part)
            m_prev = m_ref[rows, :]
            m_new = jnp.maximum(m_prev, jnp.max(mc, axis=1, keepdims=True))
            alpha = jnp.exp2(m_prev - m_new)
            ps = [jnp.exp2(part - m_new) for part in parts]
            psum = ps[0]
            for pp in ps[1:]:
                psum = psum + pp
            l_ref[rows, :] = alpha * l_ref[rows, :] + psum
            p = jnp.concatenate(ps, axis=1).astype(BF16)
            acc_ref[rows, :] = alpha * acc_ref[rows, :] + jnp.dot(p, v, preferred_element_type=F32)
            m_ref[rows, :] = m_new

    nq = k_ref.shape[0] // tq
    wmax = 1
    while wmax * 2 <= min(MAX_KEY_BLOCKS, nq - 1):
        wmax *= 2

    def body(jj, carry):
        step(wmax * jj, wmax * tq, False)
        return carry

    lax.fori_loop(0, i // wmax, body, 0)
    w = wmax // 2
    while w >= 1:
        @pl.when((i // w) % 2 == 1)
        def _(w=w):
            step((i // (2 * w)) * (2 * w), w * tq, False)
        w //= 2

    step(i, tq, True)
    out = acc_ref[...] / jnp.sum(l_ref[...], axis=1, keepdims=True)
    for hd in range(n_heads):
        o_ref[:, hd * LANES:(hd + 1) * LANES] = out[hd * tq:(hd + 1) * tq, :].astype(o_ref.dtype)


def _prompt_attention(q, kc, *, batch, seq, tq, n_heads):
    n = q.shape[0]
    nq = seq // tq
    rows = n_heads * tq
    chunk_heads = 2 if n_heads % 2 == 0 else 1
    assert tq & (tq - 1) == 0 and seq % tq == 0 and tq % LANES == 0
    return pl.pallas_call(
        functools.partial(_prompt_attn_kernel, tq=tq, n_heads=n_heads, chunk_heads=chunk_heads),
        grid=(batch, nq),
        in_specs=[pl.BlockSpec((tq, q.shape[1]), lambda b, i: (b * nq + i, 0)),
                  pl.BlockSpec((seq, kc.shape[1]), lambda b, i: (b, 0))],
        out_specs=pl.BlockSpec((tq, n_heads * LANES), lambda b, i: (b * nq + i, 0)),
        out_shape=jax.ShapeDtypeStruct((n, n_heads * LANES), BF16),
        scratch_shapes=[pltpu.VMEM((rows, 2 * LANES), BF16),
                        pltpu.VMEM((rows, LANES), F32),
                        pltpu.VMEM((rows, LANES), F32),
                        pltpu.VMEM((rows, LANES), F32)],
        compiler_params=_cparams(("parallel", "arbitrary")),
    )(q, kc)


def _sample_attn_kernel(pt_ref, q_ref, ckvn_ref, kpen_ref, kv_hbm, rp_hbm, o_ref, kvbuf, rpbuf, qs_ref, m_ref, l_ref,
                        acc_ref, sem, *, layer, n_pages, cpages, n_heads, t_new, rope, page):
    b = pl.program_id(0)
    nb = pl.num_programs(0)
    slot = b % 2

    def fetch_page(seq, s, p):
        pg = pt_ref[seq, p]
        pltpu.make_async_copy(kv_hbm.at[layer, pg], kvbuf.at[s, p], sem.at[0, s]).start()
        pltpu.make_async_copy(rp_hbm.at[layer, pg], rpbuf.at[s, p], sem.at[1, s]).start()

    def drain(s):
        pltpu.make_async_copy(kv_hbm.at[layer, pl.ds(0, n_pages)], kvbuf.at[s], sem.at[0, s]).wait()
        pltpu.make_async_copy(rp_hbm.at[layer, pl.ds(0, n_pages)], rpbuf.at[s], sem.at[1, s]).wait()

    @pl.when(b == 0)
    def _():
        def body(p, carry):
            fetch_page(0, 0, p)
            return carry
        lax.fori_loop(0, n_pages, body, 0)

    nxt = jnp.minimum(b + 1, nb - 1)
    for p in range(n_pages):
        fetch_page(nxt, 1 - slot, p)

    for hd in range(n_heads):
        qs_ref[hd * t_new:(hd + 1) * t_new, :] = q_ref[:, hd * 2 * LANES:(hd + 1) * 2 * LANES]
    m_ref[...] = jnp.full(m_ref.shape, NEG_INF, F32)
    l_ref[...] = jnp.zeros(l_ref.shape, F32)
    acc_ref[...] = jnp.zeros(acc_ref.shape, F32)
    qs = qs_ref[...]
    ql = qs[:, :LANES].astype(BF16)
    qp = qs[:, LANES:LANES + rope].astype(BF16)

    drain(slot)
    for c in range(n_pages // cpages):
        kv = kvbuf[slot, c * cpages:(c + 1) * cpages].reshape(cpages * page, LANES).astype(BF16)
        rp = jnp.concatenate([rpbuf[slot, c * cpages + p] for p in range(cpages)], axis=1).astype(BF16)
        s = (lax.dot_general(ql, kv, _NT, preferred_element_type=F32)
             + jnp.dot(qp, rp, preferred_element_type=F32))
        _softmax_update(s, kv, m_ref, l_ref, acc_ref)

    kn = jnp.concatenate([ckvn_ref[...], jnp.zeros((page - t_new, LANES), F32)], axis=0).astype(BF16)
    rn = jnp.concatenate([kpen_ref[...], jnp.zeros((page - t_new, rope), F32)], axis=0).astype(BF16)
    s2 = (lax.dot_general(ql, kn, _NT, preferred_element_type=F32)
          + lax.dot_general(qp, rn, _NT, preferred_element_type=F32))
    t = lax.broadcasted_iota(jnp.int32, s2.shape, 0) & (t_new - 1)
    col = lax.broadcasted_iota(jnp.int32, s2.shape, 1)
    s2 = jnp.where(col <= t, s2, NEG_INF)
    _softmax_update(s2, kn, m_ref, l_ref, acc_ref)
    out = acc_ref[...] / l_ref[...]
    for hd in range(n_heads):
        o_ref[:, hd * LANES:(hd + 1) * LANES] = out[hd * t_new:(hd + 1) * t_new, :]

    @pl.when(b == nb - 1)
    def _():
        drain(1 - slot)


def _sample_attention(q, ckv_new, kpe_new, cache_kv, cache_rope_t, page_table, layer, *, n_heads, t_new, cpages):
    n = q.shape[0]
    db = n // t_new
    n_pages = page_table.shape[1]
    page = cache_kv.shape[2]
    rope = cache_rope_t.shape[2]
    assert n_pages % cpages == 0 and t_new == 8 and cache_kv.shape[-1] == LANES and cache_rope_t.shape[3] == page
    rows = n_heads * t_new
    per_seq = lambda b, pt: (b, 0)
    return pl.pallas_call(
        functools.partial(_sample_attn_kernel, layer=layer, n_pages=n_pages, cpages=cpages, n_heads=n_heads,
                          t_new=t_new, rope=rope, page=page),
        grid_spec=pltpu.PrefetchScalarGridSpec(
            num_scalar_prefetch=1,
            grid=(db,),
            in_specs=[pl.BlockSpec((t_new, q.shape[1]), per_seq),
                      pl.BlockSpec((t_new, LANES), per_seq),
                      pl.BlockSpec((t_new, rope), per_seq),
                      pl.BlockSpec(memory_space=pl.ANY),
                      pl.BlockSpec(memory_space=pl.ANY)],
            out_specs=pl.BlockSpec((t_new, n_heads * LANES), per_seq),
            scratch_shapes=[pltpu.VMEM((2, n_pages, page, LANES), cache_kv.dtype),
                            pltpu.VMEM((2, n_pages, rope, page), cache_rope_t.dtype),
                            pltpu.VMEM((rows, 2 * LANES), F32),
                            pltpu.VMEM((rows, 1), F32),
                            pltpu.VMEM((rows, 1), F32),
                            pltpu.VMEM((rows, LANES), F32),
                            pltpu.SemaphoreType.DMA((2, 2))]),
        out_shape=jax.ShapeDtypeStruct((n, n_heads * LANES), F32),
        compiler_params=_cparams(("arbitrary",)),
    )(page_table, q, ckv_new, kpe_new, cache_kv, cache_rope_t)


def _mixer_out_kernel(xp_ref, xs_ref, ycp_ref, ycs_ref, atp_ref, ats_ref, wuv_ref, ga_ref, wo_ref, g_ref, b_ref,
                      x1_ref, x1p_ref, *, n_prompt_tiles, conv_ch, alpha):
    is_p = pl.program_id(0) < n_prompt_tiles
    tm = x1_ref.shape[0]
    sub = x1p_ref.shape[0] // tm
    n_chunks = 2 if tm % 32 == 0 else 1
    cm = tm // n_chunks
    for c in range(n_chunks):
        rows = slice(c * cm, (c + 1) * cm)
        x = jnp.where(is_p, xp_ref[rows, :], xs_ref[rows, :])
        yc = jnp.where(is_p, ycp_ref[rows, :], ycs_ref[rows, :])
        at = jnp.where(is_p, atp_ref[rows, :], ats_ref[rows, :].astype(BF16))
        o = jnp.dot(at, wuv_ref[...], preferred_element_type=F32)
        ya = _rms(o, ga_ref[...]).astype(BF16)
        mix = (jnp.dot(yc, wo_ref[:conv_ch, :], preferred_element_type=F32)
               + jnp.dot(ya, wo_ref[conv_ch:, :], preferred_element_type=F32))
        x1 = _layer_norm(alpha * x + mix, g_ref[...], b_ref[...])
        x1_ref[rows, :] = x1
        _pack_rows(x1, x1p_ref.at[pl.ds(c * cm * sub, cm * sub)])


def _mixer_out(xp, xs, ycp, ycs, atp, ats, wuv_bd, ga, wo, g, b, *, tm, alpha):
    np_, d = xp.shape
    ns = xs.shape[0]
    assert np_ % tm == 0 and ns % tm == 0
    npt, nst = np_ // tm, ns // tm
    conv_ch = ycp.shape[1]
    p_map = lambda i: (jnp.minimum(i, npt - 1), 0)
    s_map = lambda i: (jnp.maximum(i - npt, 0), 0)
    const = lambda i: (0, 0)
    tile = lambda i: (i, 0)
    nt = np_ + ns
    sub = d // (2 * LANES)
    return pl.pallas_call(
        functools.partial(_mixer_out_kernel, n_prompt_tiles=npt, conv_ch=conv_ch, alpha=alpha),
        grid=(npt + nst,),
        in_specs=[pl.BlockSpec((tm, d), p_map), pl.BlockSpec((tm, d), s_map),
                  pl.BlockSpec((tm, conv_ch), p_map), pl.BlockSpec((tm, conv_ch), s_map),
                  pl.BlockSpec((tm, atp.shape[1]), p_map), pl.BlockSpec((tm, ats.shape[1]), s_map),
                  pl.BlockSpec(wuv_bd.shape, const), pl.BlockSpec(ga.shape, const),
                  pl.BlockSpec(wo.shape, const), pl.BlockSpec(g.shape, const), pl.BlockSpec(b.shape, const)],
        out_specs=[pl.BlockSpec((tm, d), tile), pl.BlockSpec((tm * sub, LANES), tile)],
        out_shape=[jax.ShapeDtypeStruct((nt, d), F32), jax.ShapeDtypeStruct((nt * sub, LANES), U32)],
        compiler_params=_cparams(("parallel",)),
    )(xp, xs, ycp, ycs, atp, ats, wuv_bd, ga, wo, g, b)


def _router_kernel(x_ref, wh_ref, wl_ref, b_ref, tri_ref, idx_ref, w_ref, rank_ref, cnt_ref, base_ref,
                   *, tt, n_exp):
    @pl.when(pl.program_id(0) == 0)
    def _():
        base_ref[...] = jnp.zeros(base_ref.shape, F32)

    x = x_ref[...]
    xh = x.astype(BF16)
    xl = (x - xh.astype(F32)).astype(BF16)
    wh = wh_ref[...]
    z = (lax.dot_general(wh, xh, _NT, preferred_element_type=F32)
         + lax.dot_general(wh, xl, _NT, preferred_element_type=F32)
         + lax.dot_general(wl_ref[...], xh, _NT, preferred_element_type=F32))
    s = 1.0 / (1.0 + jnp.exp(-z))
    sb = s + b_ref[...]

    gsz = n_exp // N_GROUPS
    git = lax.broadcasted_iota(jnp.int32, (gsz, tt), 0).astype(F32)
    blocks, gscore = [], []
    for g in range(N_GROUPS):
        blk = sb[g * gsz:(g + 1) * gsz, :]
        m1 = jnp.max(blk, axis=0, keepdims=True)
        f1 = jnp.min(jnp.where(blk == m1, git, float(gsz)), axis=0, keepdims=True)
        m2 = jnp.max(jnp.where(git == f1, -jnp.inf, blk), axis=0, keepdims=True)
        blocks.append(blk)
        gscore.append(m1 + m2)
    masked = []
    for g in range(N_GROUPS):
        ahead = jnp.zeros((1, tt), F32)
        for g2 in range(N_GROUPS):
            if g2 == g:
                continue
            beats = (gscore[g2] >= gscore[g]) if g2 < g else (gscore[g2] > gscore[g])
            ahead = ahead + jnp.where(beats, 1.0, 0.0)
        masked.append(jnp.where(ahead < float(TOPK_GROUPS), blocks[g], -jnp.inf))
    vals = jnp.concatenate(masked, axis=0)

    rowi = lax.broadcasted_iota(jnp.int32, (n_exp, tt), 0).astype(F32)
    vals0 = vals
    picks, wks = [], []
    for k in range(TOP_K):
        m = jnp.max(vals, axis=0, keepdims=True)
        ik = jnp.min(jnp.where(vals == m, rowi, float(n_exp)), axis=0, keepdims=True)
        hit = rowi == ik
        wks.append(jnp.sum(jnp.where(hit, s, 0.0), axis=0, keepdims=True))
        vals = jnp.where(hit, -jnp.inf, vals)
        picks.append(ik)
    chosen = jnp.where(vals != vals0, 1.0, 0.0)
    wsum = wks[0]
    for k in range(1, TOP_K):
        wsum = wsum + wks[k]

    incl = jnp.dot(chosen.astype(BF16), tri_ref[...], preferred_element_type=F32)
    rnk = base_ref[...] + (incl - chosen)
    for k in range(TOP_K):
        idx_ref[k:k + 1, :] = picks[k].astype(jnp.int32)
        w_ref[k:k + 1, :] = wks[k] / wsum * ROUTED_SCALE
        rk = jnp.sum(jnp.where(rowi == picks[k], rnk, 0.0), axis=0, keepdims=True)
        rank_ref[k:k + 1, :] = rk.astype(jnp.int32)
    base = base_ref[...] + jnp.sum(chosen, axis=1, keepdims=True)
    base_ref[...] = base
    cnt_ref[...] = jnp.broadcast_to(base, cnt_ref.shape)


def _router(x1, wr_hi, wr_lo, bias, *, tt):
    nt, d = x1.shape
    n_exp = wr_hi.shape[0]
    assert nt % tt == 0
    tri = jnp.triu(jnp.ones((tt, tt), BF16))
    const = lambda i: (0, 0)
    col = lambda i: (0, i)
    return pl.pallas_call(
        functools.partial(_router_kernel, tt=tt, n_exp=n_exp),
        grid=(nt // tt,),
        in_specs=[pl.BlockSpec((tt, d), lambda i: (i, 0)),
                  pl.BlockSpec(wr_hi.shape, const), pl.BlockSpec(wr_lo.shape, const),
                  pl.BlockSpec(bias.shape, const), pl.BlockSpec(tri.shape, const)],
        out_specs=[pl.BlockSpec((TOP_K, tt), col), pl.BlockSpec((TOP_K, tt), col),
                   pl.BlockSpec((TOP_K, tt), col), pl.BlockSpec((n_exp, LANES), const)],
        out_shape=[jax.ShapeDtypeStruct((TOP_K, nt), jnp.int32), jax.ShapeDtypeStruct((TOP_K, nt), F32),
                   jax.ShapeDtypeStruct((TOP_K, nt), jnp.int32), jax.ShapeDtypeStruct((n_exp, LANES), F32)],
        scratch_shapes=[pltpu.VMEM((n_exp, 1), F32)],
        compiler_params=_cparams(("arbitrary",)),
    )(x1, wr_hi, wr_lo, bias, tri)


def _slot_rows_kernel(idx_ref, rank_ref, ps_ref, o_ref, *, n_exp):
    tt = idx_ref.shape[1]
    rowi = lax.broadcasted_iota(jnp.int32, (n_exp, tt), 0)
    ps = ps_ref[...]
    for k in range(TOP_K):
        start = jnp.sum(jnp.where(rowi == idx_ref[k:k + 1, :], ps, 0.0), axis=0, keepdims=True)
        o_ref[k:k + 1, :] = start.astype(jnp.int32) + rank_ref[k:k + 1, :]


def _slot_rows(idx_t, rank_t, pad_start, *, tt):
    nt = idx_t.shape[1]
    n_exp = pad_start.shape[0]
    assert nt % tt == 0 and n_exp * SLOT_ROWS + nt * TOP_K < 2 ** 24
    col = lambda i: (0, i)
    return pl.pallas_call(
        functools.partial(_slot_rows_kernel, n_exp=n_exp),
        grid=(nt // tt,),
        in_specs=[pl.BlockSpec((TOP_K, tt), col), pl.BlockSpec((TOP_K, tt), col),
                  pl.BlockSpec((n_exp, 1), lambda i: (0, 0))],
        out_specs=pl.BlockSpec((TOP_K, tt), col),
        out_shape=jax.ShapeDtypeStruct((TOP_K, nt), jnp.int32),
        compiler_params=_cparams(("parallel",)),
    )(idx_t, rank_t, pad_start.astype(F32).reshape(n_exp, 1))


def _dispatch_kernel(ps_ref, nb_ref, cnt_ref, dest_ref, x_ref, o_ref, zbuf, sem, zsem, *, td, sub, n_exp):
    i = pl.program_id(0)
    blk = SLOT_ROWS * sub

    @pl.when(i == 0)
    def _():
        zbuf[...] = jnp.zeros(zbuf.shape, zbuf.dtype)
        n_blocks = o_ref.shape[0] // blk
        used = ps_ref[n_exp - 1] // SLOT_ROWS + nb_ref[n_exp - 1]

        def zero_block(b):
            return pltpu.make_async_copy(zbuf, o_ref.at[pl.ds(pl.multiple_of(b * blk, blk), blk)], zsem)

        def partial(e):
            return (cnt_ref[e] & (SLOT_ROWS - 1)) != 0

        def fill(e, carry):
            @pl.when(partial(e))
            def _():
                zero_block(ps_ref[e] // SLOT_ROWS + nb_ref[e] - 1).start()
            return carry

        def fill_done(e, carry):
            @pl.when(partial(e))
            def _():
                zero_block(0).wait()
            return carry

        def tail(b, carry):
            zero_block(b).start()
            return carry

        def tail_done(b, carry):
            zero_block(0).wait()
            return carry

        lax.fori_loop(0, n_exp, fill, 0)
        lax.fori_loop(used, n_blocks, tail, 0)
        lax.fori_loop(0, n_exp, fill_done, 0)
        lax.fori_loop(used, n_blocks, tail_done, 0)

    def per_token(t, carry):
        src = x_ref.at[pl.ds(pl.multiple_of(t * sub, sub), sub)]
        for k in range(TOP_K):
            dst = pl.multiple_of(dest_ref[k, t] * sub, sub)
            pltpu.make_async_copy(src, o_ref.at[pl.ds(dst, sub)], sem).start(priority=k % 2)
        return carry

    lax.fori_loop(0, td, per_token, 0)
    for _ in range(TOP_K):
        pltpu.make_async_copy(x_ref, o_ref.at[pl.ds(0, td * sub)], sem).wait()


def _dispatch(pad_start, n_blk, counts, dest_t, x1p, n_slots, *, td, sub):
    nt = x1p.shape[0] // sub
    assert nt % td == 0 and n_slots >= td and n_slots % SLOT_ROWS == 0
    return pl.pallas_call(
        functools.partial(_dispatch_kernel, td=td, sub=sub, n_exp=counts.shape[0]),
        grid_spec=pltpu.PrefetchScalarGridSpec(
            num_scalar_prefetch=3,
            grid=(nt // td,),
            in_specs=[pl.BlockSpec((TOP_K, td), lambda i, *_: (0, i), memory_space=pltpu.SMEM),
                      pl.BlockSpec((td * sub, LANES), lambda i, *_: (i, 0))],
            out_specs=pl.BlockSpec(memory_space=pl.ANY),
            scratch_shapes=[pltpu.VMEM((SLOT_ROWS * sub, LANES), x1p.dtype), pltpu.SemaphoreType.DMA(()),
                            pltpu.SemaphoreType.DMA(())]),
        out_shape=jax.ShapeDtypeStruct((n_slots * sub, LANES), x1p.dtype),
        compiler_params=_cparams(("arbitrary",)),
    )(pad_start, n_blk, counts, dest_t, x1p)


EXPERT_RING = 8


def _experts_kernel(ps_ref, nb_ref, wg_ref, wu_ref, wd_ref, xs_hbm, ys_hbm, wg_s, wu_s, wd_s, xbuf, ybuf,
                    xsem, ysem, zsem, *, sub, n_exp):
    e = pl.program_id(0)
    blk = SLOT_ROWS * sub
    n_blocks = xs_hbm.shape[0] // blk
    used = ps_ref[n_exp - 1] // SLOT_ROWS + nb_ref[n_exp - 1]

    def rows(g):
        return pl.ds(pl.multiple_of(g * blk, blk), blk)

    def ring(buf, s, n=1):
        return buf.at[pl.ds(pl.multiple_of(s * blk, blk), n * blk)]

    def x_copy(g, s):
        return pltpu.make_async_copy(xs_hbm.at[rows(g)], ring(xbuf, s), xsem.at[s])

    def y_copy(g, s):
        return pltpu.make_async_copy(ring(ybuf, s), ys_hbm.at[rows(g)], ysem.at[s])

    @pl.when(e == 0)
    def _():
        zbuf = ring(ybuf, EXPERT_RING - 1)
        zbuf[...] = jnp.zeros(zbuf.shape, zbuf.dtype)

        def fill(b, carry):
            pltpu.make_async_copy(zbuf, ys_hbm.at[rows(b)], zsem).start()
            return carry

        def fill_done(b, carry):
            pltpu.make_async_copy(zbuf, ys_hbm.at[rows(0)], zsem).wait()
            return carry

        lax.fori_loop(used, n_blocks, fill, 0)
        lax.fori_loop(used, n_blocks, fill_done, 0)
        for s in range(EXPERT_RING):
            @pl.when(s < used)
            def _():
                x_copy(s, s).start()

    nb = nb_ref[e]
    first = ps_ref[e] // SLOT_ROWS

    @pl.when(nb > 0)
    def _():
        wg_s[...] = wg_ref[...].astype(BF16)
        wu_s[...] = wu_ref[...].astype(BF16)
        wd_s[...] = wd_ref[...].astype(BF16)

        def run(g, n):
            s = g & (EXPERT_RING - 1)
            for d in range(n):
                x_copy(g + d, s + d).wait()

            @pl.when(g >= EXPERT_RING)
            def _():
                for d in range(n):
                    y_copy(g + d, s + d).wait()

            lo, hi = _unpack_rows(ring(xbuf, s, n), 0, n * SLOT_ROWS, sub)
            x = jnp.concatenate(lo + hi, axis=1).astype(BF16)
            gate = jnp.dot(x, wg_s[...], preferred_element_type=F32)
            up = jnp.dot(x, wu_s[...], preferred_element_type=F32)
            hmid = (_silu(gate) * up).astype(BF16)
            _pack_rows(jnp.dot(hmid, wd_s[...], preferred_element_type=F32), ring(ybuf, s, n))
            for d in range(n):
                y_copy(g + d, s + d).start()
            for d in range(n):
                @pl.when(g + d + EXPERT_RING < used)
                def _():
                    x_copy(g + d + EXPERT_RING, s + d).start()

        lead = first & 1

        @pl.when(lead == 1)
        def _():
            run(first, 1)

        def pair(p, carry):
            run(first + lead + 2 * p, 2)
            return carry

        lax.fori_loop(0, (nb - lead) // 2, pair, 0)

        @pl.when(((nb - lead) & 1) == 1)
        def _():
            run(first + nb - 1, 1)

    @pl.when(e == n_exp - 1)
    def _():
        for s in range(EXPERT_RING):
            @pl.when(s < used)
            def _():
                y_copy(0, s).wait()


def _experts(pad_start, n_blk, xs, w_gate, w_up, w_down, layer):
    n_exp, d, de = w_gate.shape[-3:]
    sub = d // (2 * LANES)
    blk = SLOT_ROWS * sub
    assert xs.shape[0] % blk == 0 and xs.shape[1] == LANES
    return pl.pallas_call(
        functools.partial(_experts_kernel, sub=sub, n_exp=n_exp),
        grid_spec=pltpu.PrefetchScalarGridSpec(
            num_scalar_prefetch=2,
            grid=(n_exp,),
            in_specs=[pl.BlockSpec((None, None, d, de), lambda e, ps, nb: (layer, e, 0, 0)),
                      pl.BlockSpec((None, None, d, de), lambda e, ps, nb: (layer, e, 0, 0)),
                      pl.BlockSpec((None, None, de, d), lambda e, ps, nb: (layer, e, 0, 0)),
                      pl.BlockSpec(memory_space=pl.ANY)],
            out_specs=pl.BlockSpec(memory_space=pl.ANY),
            scratch_shapes=[pltpu.VMEM((d, de), BF16), pltpu.VMEM((d, de), BF16), pltpu.VMEM((de, d), BF16),
                            pltpu.VMEM((EXPERT_RING * blk, LANES), U32), pltpu.VMEM((EXPERT_RING * blk, LANES), U32),
                            pltpu.SemaphoreType.DMA((EXPERT_RING,)), pltpu.SemaphoreType.DMA((EXPERT_RING,)),
                            pltpu.SemaphoreType.DMA(())]),
        out_shape=jax.ShapeDtypeStruct(xs.shape, U32),
        compiler_params=_cparams(("arbitrary",)),
    )(pad_start, n_blk, w_gate, w_up, w_down, xs)


def _combine_kernel(dest0_ref, dest1_ref, w_ref, x1_ref, wsg_ref, wsu_ref,
                    wsd_ref, g_ref, b_ref, ys_ref, yp_ref, ysm_ref, buf, sem, *, tc, alpha, sub, n_prompt_tiles):
    i = pl.program_id(0)
    n = pl.num_programs(0)
    slot = i % 2

    def gather_row(dest_ref, s, k, t):
        src = pl.multiple_of(dest_ref[k, t] * sub, sub)
        dst = (k * tc + t) * sub
        if not isinstance(dst, int):
            dst = pl.multiple_of(dst, sub)
        pltpu.make_async_copy(ys_ref.at[pl.ds(src, sub)], buf.at[s, pl.ds(dst, sub)], sem.at[s]).start(priority=k % 2)

    @pl.when(i == 0)
    def _():
        def per_token(t, carry):
            for k in range(TOP_K):
                gather_row(dest0_ref, 0, k, t)
            return carry
        lax.fori_loop(0, tc, per_token, 0)

    for t in range(tc):
        for k in range(TOP_K):
            gather_row(dest1_ref, 1 - slot, k, t)

    x1 = x1_ref[...]
    xb = x1.astype(BF16)
    hs = (_silu(jnp.dot(xb, wsg_ref[...], preferred_element_type=F32))
          * jnp.dot(xb, wsu_ref[...], preferred_element_type=F32)).astype(BF16)
    shared = jnp.dot(hs, wsd_ref[...], preferred_element_type=F32)

    def drain(s):
        pltpu.make_async_copy(ys_ref.at[pl.ds(0, TOP_K * tc * sub)], buf.at[s], sem.at[s]).wait()

    drain(slot)
    w = w_ref[...]
    lo_acc = [jnp.zeros((tc, LANES), F32) for _ in range(sub)]
    hi_acc = [jnp.zeros((tc, LANES), F32) for _ in range(sub)]
    for k in range(TOP_K):
        lo, hi = _unpack_rows(buf.at[slot], k * tc, tc, sub)
        wk = w[:, k:k + 1]
        for j in range(sub):
            lo_acc[j] = lo_acc[j] + wk * lo[j]
            hi_acc[j] = hi_acc[j] + wk * hi[j]
    moe = jnp.concatenate(lo_acc + hi_acc, axis=1) + shared
    y = _layer_norm(alpha * x1 + moe, g_ref[...], b_ref[...])

    @pl.when(i < n_prompt_tiles)
    def _():
        yp_ref[...] = y

    @pl.when(i >= n_prompt_tiles)
    def _():
        ysm_ref[...] = y

    @pl.when(i == n - 1)
    def _():
        drain(1 - slot)


def _combine(dest_t, w_tok, x1, wsg, wsu, wsd, g, b, ys, *, tc, alpha, n_prompt):
    nt, d = x1.shape
    n_tiles = nt // tc
    sub = d // (2 * LANES)
    assert nt % tc == 0 and n_prompt % tc == 0 and 0 < n_prompt < nt and ys.shape[0] >= TOP_K * tc * sub
    npt = n_prompt // tc
    cur = pl.BlockSpec((TOP_K, tc), lambda i: (0, i), memory_space=pltpu.SMEM)
    nxt = pl.BlockSpec((TOP_K, tc), lambda i: (0, jnp.minimum(i + 1, n_tiles - 1)), memory_space=pltpu.SMEM)
    const = lambda i: (0, 0)
    tile = lambda i: (i, 0)
    return pl.pallas_call(
        functools.partial(_combine_kernel, tc=tc, alpha=alpha, sub=sub, n_prompt_tiles=npt),
        grid=(n_tiles,),
        in_specs=[cur, nxt,
                  pl.BlockSpec((tc, TOP_K), tile),
                  pl.BlockSpec((tc, d), tile),
                  pl.BlockSpec(wsg.shape, const), pl.BlockSpec(wsu.shape, const), pl.BlockSpec(wsd.shape, const),
                  pl.BlockSpec(g.shape, const), pl.BlockSpec(b.shape, const),
                  pl.BlockSpec(memory_space=pl.ANY)],
        out_specs=[pl.BlockSpec((tc, d), lambda i: (jnp.minimum(i, npt - 1), 0)),
                   pl.BlockSpec((tc, d), lambda i: (jnp.maximum(i - npt, 0), 0))],
        out_shape=[jax.ShapeDtypeStruct((n_prompt, d), F32), jax.ShapeDtypeStruct((nt - n_prompt, d), F32)],
        scratch_shapes=[pltpu.VMEM((2, TOP_K * tc * sub, LANES), ys.dtype), pltpu.SemaphoreType.DMA((2,))],
        compiler_params=_cparams(("arbitrary",)),
    )(dest_t, dest_t, w_tok, x1, wsg, wsu, wsd, g, b, ys)


def _rope_cs(pos, rope):
    half = rope // 2
    inv = ROPE_BASE ** (-jnp.arange(half, dtype=F32) / half)
    ang = pos.astype(F32)[:, None] * inv[None, :]
    cos, sin = jnp.cos(ang), jnp.sin(ang)
    pad = jnp.zeros((pos.shape[0], LANES - rope), F32)
    return (jnp.concatenate([cos, cos, pad], axis=1), jnp.concatenate([-sin, sin, pad], axis=1))


def _pick_tile(n, pref):
    t = pref
    while n % t:
        t //= 2
    return t


def kernel(x_prompt, x_sample, cache_kv_latent, cache_k_rope, state_conv, page_table, w_in, conv_w, q_norm, w_uq, kv_norm, w_uk, w_uv, g_conv, g_attn, w_o, ln1_g, ln1_b, w_router, router_bias, w_gate, w_up, w_down, ws_gate, ws_up, ws_down, ln2_g, ln2_b):
    depth = w_in.shape[0]
    bsz, seq, d = x_prompt.shape
    db, t_new, _ = x_sample.shape
    kv_lora, n_heads, nope = w_uk.shape[1:]
    v_dim = w_uv.shape[-1]
    rope = cache_k_rope.shape[-1]
    q_lora = q_norm.shape[-1]
    conv_ch = conv_w.shape[-1]
    n_exp = w_router.shape[-1]
    page = cache_kv_latent.shape[2]
    past_len = page_table.shape[1] * page
    alpha = (2.0 * depth) ** 0.25
    scale = float((nope + rope) ** -0.5) * math.log2(math.e)
    in_cols = w_in.shape[-1]
    in_pad = -(-(in_cols - rope + LANES) // LANES) * LANES
    n_p, n_s = bsz * seq, db * t_new
    nt = n_p + n_s

    tm_p = _pick_tile(seq, 512)
    tm_s = _pick_tile(n_s, 512)
    tq = _pick_tile(seq, 256)
    tm_o = _pick_tile(math.gcd(n_p, n_s), 1024)
    tt = _pick_tile(math.gcd(n_p, n_s), 512)
    td = _pick_tile(nt, 1024)
    tc = _pick_tile(math.gcd(n_p, n_s), 256)
    pps = _pick_tile(page_table.shape[1], 64)
    cache_rope_t = jnp.swapaxes(cache_k_rope, 2, 3)

    cos_p, sin_p = _rope_cs(jnp.arange(seq, dtype=jnp.int32), rope)
    cos_s, sin_s = _rope_cs(past_len + jnp.arange(t_new, dtype=jnp.int32), rope)
    cos_s, sin_s = jnp.tile(cos_s, (tm_s // t_new, 1)), jnp.tile(sin_s, (tm_s // t_new, 1))

    xp, xs_ = x_prompt.reshape(n_p, d), x_sample.reshape(n_s, d)
    outs = [[] for _ in range(6)]
    for l in range(depth):
        win = jnp.pad(w_in[l], ((0, 0), (0, in_pad - in_cols))).astype(BF16)
        wuq3 = w_uq[l].reshape(q_lora, n_heads, nope + rope)
        qlat = _fold_qlat(jnp.transpose(wuq3[:, :, :nope], (1, 0, 2)), jnp.transpose(w_uk[l], (1, 2, 0)), scale)
        wq_rope = jnp.transpose(wuq3[:, :, nope:], (1, 0, 2)) * scale
        wq = jnp.concatenate([qlat, wq_rope, jnp.zeros((n_heads, q_lora, LANES - rope), F32)], axis=2)
        wq = jnp.transpose(wq, (1, 0, 2)).reshape(q_lora, n_heads * 2 * LANES).astype(BF16)
        eye = jnp.eye(n_heads, dtype=F32)
        wuv_bd = jnp.einsum('chv,hg->hcgv', w_uv[l], eye).reshape(n_heads * kv_lora, n_heads * v_dim).astype(BF16)
        wo = w_o[l].astype(BF16)
        wr_t = w_router[l].T
        wr_hi = wr_t.astype(BF16)
        wr_lo = (wr_t - wr_hi.astype(F32)).astype(BF16)
        row = lambda v: v[l].reshape(1, -1)

        common = (win, wq, row(q_norm), row(kv_norm), conv_w[l], row(g_conv))
        q_p, kc_p, ckv_p, kpe_p, u_p, yc_p = _mixer_in(
            xp, *common, cos_p, sin_p, None, prompt=True, seq=seq, tm=tm_p, n_heads=n_heads, rope=rope, q_dtype=BF16)
        st = state_conv[l].astype(F32)
        zero = jnp.zeros((db, conv_ch), F32)
        prev1 = jnp.stack([st[:, 1]] + [zero] * (t_new - 1), axis=1).reshape(n_s, conv_ch)
        prev2 = jnp.stack([st[:, 0], st[:, 1]] + [zero] * (t_new - 2), axis=1).reshape(n_s, conv_ch)
        q_s, kc_s, ckv_s, kpe_s, u_s, yc_s = _mixer_in(
            xs_, *common, cos_s, sin_s, (prev1, prev2), prompt=False, seq=t_new, tm=tm_s, n_heads=n_heads, rope=rope,
            q_dtype=F32)
        del kc_s

        at_p = _prompt_attention(q_p, kc_p, batch=bsz, seq=seq, tq=tq, n_heads=n_heads)
        at_s = _sample_attention(q_s, ckv_s, kpe_s, cache_kv_latent, cache_rope_t, page_table, l,
                                 n_heads=n_heads, t_new=t_new, cpages=pps)

        x1, x1p = _mixer_out(xp, xs_, yc_p, yc_s, at_p, at_s, wuv_bd, row(g_attn), wo, row(ln1_g), row(ln1_b),
                             tm=tm_o, alpha=alpha)

        idx_t, w_t, rank_t, cnt = _router(x1, wr_hi, wr_lo, router_bias[l].reshape(n_exp, 1), tt=tt)
        counts = cnt[:, 0].astype(jnp.int32)
        padded = (counts + SLOT_ROWS - 1) // SLOT_ROWS * SLOT_ROWS
        pad_end = jnp.cumsum(padded)
        pad_start = (pad_end - padded).astype(jnp.int32)
        n_blk = (padded // SLOT_ROWS).astype(jnp.int32)
        n_blocks = -(-(nt * TOP_K) // SLOT_ROWS) + n_exp

        dest_t = _slot_rows(idx_t, rank_t, pad_start, tt=td)
        xs_sorted = _dispatch(pad_start, n_blk, counts, dest_t, x1p, n_blocks * SLOT_ROWS, td=td, sub=d // (2 * LANES))
        ys = _experts(pad_start, n_blk, xs_sorted, w_gate, w_up, w_down, l)
        xp, xs_ = _combine(dest_t, w_t.T, x1, ws_gate[l].astype(BF16), ws_up[l].astype(BF16),
                           ws_down[l].astype(BF16), row(ln2_g), row(ln2_b), ys, tc=tc, alpha=alpha, n_prompt=n_p)
        outs[0].append(ckv_p.reshape(bsz, seq, kv_lora))
        outs[1].append(kpe_p.reshape(bsz, seq, rope))
        outs[2].append(u_p.reshape(bsz, seq, conv_ch)[:, seq - (CONV_W - 1):])
        outs[3].append(ckv_s.reshape(db, t_new, kv_lora))
        outs[4].append(kpe_s.reshape(db, t_new, rope))
        outs[5].append(u_s.reshape(db, t_new, conv_ch)[:, t_new - (CONV_W - 1):])
    return (xp.reshape(bsz, seq, d), xs_.reshape(db, t_new, d)) + tuple(jnp.stack(o) for o in outs)
```

```python
import functools
import math

import jax
import jax.numpy as jnp
from jax import lax
from jax.experimental import pallas as pl
from jax.experimental.pallas import tpu as pltpu

F32 = jnp.float32
BF16 = jnp.bfloat16
U32 = jnp.uint32

ROPE_BASE = 10000.0
NORM_EPS = 1e-6
LN_EPS = 1e-5
NEG_INF = -1e30
TOP_K = 8
N_GROUPS = 8
TOPK_GROUPS = 4
ROUTED_SCALE = 2.5
CONV_W = 3

LANES = 128
SLOT_ROWS = 256
VMEM_LIMIT = 52 * 1024 * 1024

_NT = (((1,), (1,)), ((), ()))


def _cparams(sem):
    return pltpu.CompilerParams(dimension_semantics=sem, vmem_limit_bytes=VMEM_LIMIT)


def _rms(x, g):
    return x * lax.rsqrt(jnp.mean(x * x, axis=-1, keepdims=True) + NORM_EPS) * g


def _layer_norm(x, g, b):
    mu = jnp.mean(x, axis=-1, keepdims=True)
    xc = x - mu
    var = jnp.mean(xc * xc, axis=-1, keepdims=True)
    return xc * lax.rsqrt(var + LN_EPS) * g + b


def _silu(x):
    return x / (1.0 + jnp.exp(-x))


def _rope_tile(p, c, s, half):
    lane = lax.broadcasted_iota(jnp.int32, p.shape, 1)
    swapped = jnp.where(lane < half, pltpu.roll(p, LANES - half, axis=1), pltpu.roll(p, half, axis=1))
    return p * c + swapped * s


def _pack_rows(x, ref):
    r = x.shape[0]
    half = x.shape[1] // 2
    sub = half // LANES
    bits = lax.bitcast_convert_type(x.astype(BF16).astype(F32), U32)
    words = bits[:, half:] | (bits[:, :half] >> 16)
    for j in range(sub):
        ref[pl.ds(j, r, stride=sub), :] = words[:, j * LANES:(j + 1) * LANES]


def _unpack_rows(ref, first, r, sub):
    lo, hi = [], []
    for j in range(sub):
        w = ref[pl.ds(first * sub + j, r, stride=sub), :]
        lo.append(lax.bitcast_convert_type(w << 16, F32))
        hi.append(lax.bitcast_convert_type(w & jnp.uint32(0xFFFF0000), F32))
    return lo, hi


def _fold_kernel(a_ref, b_ref, o_ref, *, scale):
    o_ref[...] = jnp.dot(a_ref[...], b_ref[...], precision=lax.Precision.HIGHEST,
                         preferred_element_type=F32) * scale


def _fold_qlat(wuq_nope, wuk_t, scale):
    h, r, n = wuq_nope.shape
    c = wuk_t.shape[-1]
    return pl.pallas_call(
        functools.partial(_fold_kernel, scale=scale),
        grid=(h,),
        in_specs=[pl.BlockSpec((None, r, n), lambda i: (i, 0, 0)),
                  pl.BlockSpec((None, n, c), lambda i: (i, 0, 0))],
        out_specs=pl.BlockSpec((None, r, c), lambda i: (i, 0, 0)),
        out_shape=jax.ShapeDtypeStruct((h, r, c), F32),
        compiler_params=_cparams(("parallel",)),
    )(wuq_nope, wuk_t)


def _mixer_in_kernel(*refs, prompt, tiles_per_seq, tm, n_chunks, t_new, n_heads, conv_ch, q_lora, kv_lora, rope):
    if prompt:
        (x_ref, win_ref, wq_ref, qn_ref, kvn_ref, cw_ref, gc_ref, cos_ref, sin_ref,
         q_ref, kc_ref, ckv_ref, kpe_ref, u_ref, yc_ref, carry_ref) = refs
    else:
        (x_ref, win_ref, wq_ref, qn_ref, kvn_ref, cw_ref, gc_ref, cos_ref, sin_ref, p1_ref, p2_ref,
         q_ref, kc_ref, ckv_ref, kpe_ref, u_ref, yc_ref) = refs
    half = rope // 2
    c1, c2, c3 = conv_ch, 2 * conv_ch, 3 * conv_ch
    c4 = c3 + q_lora
    c5 = c4 + kv_lora

    cm = tm // n_chunks
    row = lax.broadcasted_iota(jnp.int32, (cm, 1), 0)
    u_prev = None
    for c in range(n_chunks):
        rows = slice(c * cm, (c + 1) * cm)
        h = jnp.dot(x_ref[rows, :].astype(BF16), win_ref[...], preferred_element_type=F32)
        b_gate, c_gate, x_conv = h[:, :c1], h[:, c1:c2], h[:, c2:c3]
        q_a, c_kv, kp = h[:, c3:c4], h[:, c4:c5], h[:, c5:c5 + LANES]

        u = c_gate * x_conv
        u_ref[rows, :] = u
        if prompt:
            if c == 0:
                first = (pl.program_id(0) % tiles_per_seq) == 0
                prev1 = jnp.where(first, 0.0, carry_ref[7:8, :])
                prev2 = jnp.where(first, 0.0, carry_ref[6:7, :])
            else:
                prev1 = u_prev[cm - 1:cm, :]
                prev2 = u_prev[cm - 2:cm - 1, :]
            t_in = row
            p1 = prev1
            p2 = jnp.where(row == 0, prev2, prev1)
        else:
            t_in = row & (t_new - 1)
            p1 = p1_ref[rows, :]
            p2 = p2_ref[rows, :]
        um1 = jnp.where(t_in == 0, p1, pltpu.roll(u, 1, axis=0))
        um2 = jnp.where(t_in < 2, p2, pltpu.roll(u, 2, axis=0))
        conv_y = cw_ref[0:1, :] * um2 + cw_ref[1:2, :] * um1 + cw_ref[2:3, :] * u
        yc_ref[rows, :] = _rms(b_gate * conv_y, gc_ref[...]).astype(yc_ref.dtype)
        u_prev = u

        cos = cos_ref[rows, :]
        sin = sin_ref[rows, :]
        qn = _rms(q_a, qn_ref[...]).astype(BF16)
        q = jnp.dot(qn, wq_ref[...], preferred_element_type=F32)
        for hd in range(n_heads):
            o = hd * 2 * LANES
            q_ref[rows, o:o + LANES] = q[:, o:o + LANES].astype(q_ref.dtype)
            q_ref[rows, o + LANES:o + 2 * LANES] = _rope_tile(q[:, o + LANES:o + 2 * LANES], cos, sin,
                                                              half).astype(q_ref.dtype)

        ckv = _rms(c_kv, kvn_ref[...])
        ckv_ref[rows, :] = ckv
        kpr = _rope_tile(kp, cos, sin, half)
        kpe_ref[rows, :] = kpr[:, :rope]
        kc_ref[rows, :LANES] = ckv.astype(BF16)
        kc_ref[rows, LANES:] = kpr.astype(BF16)
    if prompt:
        carry_ref[...] = u_prev[cm - 8:, :]


def _mixer_in(x2, win, wq, qn, kvn, cw, gc, cos, sin, prev, *, prompt, seq, tm, n_heads, rope, q_dtype):
    n, d = x2.shape
    conv_ch = cw.shape[-1]
    q_lora = qn.shape[-1]
    kv_lora = kvn.shape[-1]
    assert n % tm == 0 and kv_lora == LANES and tm % 8 == 0
    tiles_per_seq = seq // tm if prompt else 1
    if prompt:
        assert seq % tm == 0
    else:
        assert seq & (seq - 1) == 0 and tm % seq == 0
    const = lambda i: (0, 0)
    tile = lambda i: (i, 0)
    in_specs = [
        pl.BlockSpec((tm, d), tile),
        pl.BlockSpec(win.shape, const),
        pl.BlockSpec(wq.shape, const),
        pl.BlockSpec(qn.shape, const),
        pl.BlockSpec(kvn.shape, const),
        pl.BlockSpec(cw.shape, const),
        pl.BlockSpec(gc.shape, const),
    ]
    args = [x2, win, wq, qn, kvn, cw, gc, cos, sin]
    if prompt:
        in_specs += [pl.BlockSpec((tm, LANES), lambda i: (i % tiles_per_seq, 0))] * 2
        scratch = [pltpu.VMEM((8, conv_ch), F32)]
    else:
        in_specs += [pl.BlockSpec((tm, LANES), const)] * 2
        in_specs += [pl.BlockSpec((tm, conv_ch), tile)] * 2
        args += list(prev)
        scratch = []
    qw = wq.shape[-1]
    out_shape = [
        jax.ShapeDtypeStruct((n, qw), q_dtype),
        jax.ShapeDtypeStruct((n, 2 * LANES), BF16),
        jax.ShapeDtypeStruct((n, kv_lora), F32),
        jax.ShapeDtypeStruct((n, rope), F32),
        jax.ShapeDtypeStruct((n, conv_ch), F32),
        jax.ShapeDtypeStruct((n, conv_ch), BF16),
    ]
    out_specs = [
        pl.BlockSpec((tm, qw), tile),
        pl.BlockSpec((tm, 2 * LANES), tile),
        pl.BlockSpec((tm, kv_lora), tile),
        pl.BlockSpec((tm, rope), tile),
        pl.BlockSpec((tm, conv_ch), tile),
        pl.BlockSpec((tm, conv_ch), tile),
    ]
    return pl.pallas_call(
        functools.partial(_mixer_in_kernel, prompt=prompt, tiles_per_seq=tiles_per_seq, tm=tm,
                          n_chunks=2 if tm % 32 == 0 else 1, t_new=seq,
                          n_heads=n_heads, conv_ch=conv_ch, q_lora=q_lora, kv_lora=kv_lora, rope=rope),
        grid=(n // tm,),
        in_specs=in_specs,
        out_specs=out_specs,
        out_shape=out_shape,
        scratch_shapes=scratch,
        compiler_params=_cparams(("arbitrary",)),
    )(*args)


MAX_KEY_BLOCKS = 4

def _softmax_update(s, v, m_ref, l_ref, acc_ref):
    m_prev = m_ref[...]
    m_new = jnp.maximum(m_prev, jnp.max(s, axis=1, keepdims=True))
    alpha = jnp.exp2(m_prev - m_new)
    p = jnp.exp2(s - m_new)
    l_ref[...] = alpha * l_ref[...] + jnp.sum(p, axis=1, keepdims=True)
    acc_ref[...] = alpha * acc_ref[...] + jnp.dot(p.astype(BF16), v, preferred_element_type=F32)
    m_ref[...] = m_new


def _prompt_attn_kernel(q_ref, k_ref, o_ref, qs_ref, m_ref, l_ref, acc_ref, *, tq, n_heads, chunk_heads):
    i = pl.program_id(1)
    cr = chunk_heads * tq
    n_chunks = n_heads // chunk_heads
    for hd in range(n_heads):
        qs_ref[hd * tq:(hd + 1) * tq, :] = q_ref[:, hd * 2 * LANES:(hd + 1) * 2 * LANES]
    m_ref[...] = jnp.full(m_ref.shape, NEG_INF, F32)
    l_ref[...] = jnp.zeros(l_ref.shape, F32)
    acc_ref[...] = jnp.zeros(acc_ref.shape, F32)

    def step(j, width, masked):
        k = k_ref[pl.ds(pl.multiple_of(j * tq, tq), width), :]
        v = k[:, :LANES]
        for c in range(n_chunks):
            rows = slice(c * cr, (c + 1) * cr)
            s = lax.dot_general(qs_ref[rows, :], k, _NT, preferred_element_type=F32)
            if masked:
                t = lax.broadcasted_iota(jnp.int32, s.shape, 0) & (tq - 1)
                col = lax.broadcasted_iota(jnp.int32, s.shape, 1)
                s = jnp.where(col <= t + (width - tq), s, NEG_INF)
            parts = [s[:, w * LANES:(w + 1) * LANES] for w in range(width // LANES)]
            mc = parts[0]
            for part in parts[1:]:
                mc = jnp.maximum(mc, part)
            m_prev = m_ref[rows, :]
            m_new = jnp.maximum(m_prev, jnp.max(mc, axis=1, keepdims=True))
            alpha = jnp.exp2(m_prev - m_new)
            ps = [jnp.exp2(part - m_new) for part in parts]
            psum = ps[0]
            for pp in ps[1:]:
                psum = psum + pp
            l_ref[rows, :] = alpha * l_ref[rows, :] + psum
            p = jnp.concatenate(ps, axis=1).astype(BF16)
            acc_ref[rows, :] = alpha * acc_ref[rows, :] + jnp.dot(p, v, preferred_element_type=F32)
            m_ref[rows, :] = m_new

    nq = k_ref.shape[0] // tq
    wmax = 1
    while wmax * 2 <= min(MAX_KEY_BLOCKS, nq):
        wmax *= 2

    def body(jj, carry):
        step(wmax * jj, wmax * tq, False)
        return carry

    lax.fori_loop(0, i // wmax, body, 0)
    for r in range(wmax):
        @pl.when(i % wmax == r)
        def _(r=r):
            step((i // wmax) * wmax, (r + 1) * tq, True)
    out = acc_ref[...] / jnp.sum(l_ref[...], axis=1, keepdims=True)
    for hd in range(n_heads):
        o_ref[:, hd * LANES:(hd + 1) * LANES] = out[hd * tq:(hd + 1) * tq, :].astype(o_ref.dtype)


def _prompt_attention(q, kc, *, batch, seq, tq, n_heads):
    n = q.shape[0]
    nq = seq // tq
    rows = n_heads * tq
    chunk_heads = 2 if n_heads % 2 == 0 else 1
    assert tq & (tq - 1) == 0 and seq % tq == 0 and tq % LANES == 0
    return pl.pallas_call(
        functools.partial(_prompt_attn_kernel, tq=tq, n_heads=n_heads, chunk_heads=chunk_heads),
        grid=(batch, nq),
        in_specs=[pl.BlockSpec((tq, q.shape[1]), lambda b, i: (b * nq + i, 0)),
                  pl.BlockSpec((seq, kc.shape[1]), lambda b, i: (b, 0))],
        out_specs=pl.BlockSpec((tq, n_heads * LANES), lambda b, i: (b * nq + i, 0)),
        out_shape=jax.ShapeDtypeStruct((n, n_heads * LANES), BF16),
        scratch_shapes=[pltpu.VMEM((rows, 2 * LANES), BF16),
                        pltpu.VMEM((rows, LANES), F32),
                        pltpu.VMEM((rows, LANES), F32),
                        pltpu.VMEM((rows, LANES), F32)],
        compiler_params=_cparams(("parallel", "arbitrary")),
    )(q, kc)


def _sample_attn_kernel(pt_ref, q_ref, ckvn_ref, kpen_ref, kv_hbm, rp_hbm, o_ref, kvbuf, rpbuf, qs_ref, m_ref, l_ref,
                        acc_ref, sem, *, layer, n_pages, cpages, n_heads, t_new, rope, page):
    b = pl.program_id(0)
    nb = pl.num_programs(0)
    slot = b % 2

    def fetch_page(seq, s, p):
        pg = pt_ref[seq, p]
        pltpu.make_async_copy(kv_hbm.at[layer, pg], kvbuf.at[s, p], sem.at[0, s]).start()
        pltpu.make_async_copy(rp_hbm.at[layer, pg], rpbuf.at[s, p], sem.at[1, s]).start()

    def drain(s):
        pltpu.make_async_copy(kv_hbm.at[layer, pl.ds(0, n_pages)], kvbuf.at[s], sem.at[0, s]).wait()
        pltpu.make_async_copy(rp_hbm.at[layer, pl.ds(0, n_pages)], rpbuf.at[s], sem.at[1, s]).wait()

    @pl.when(b == 0)
    def _():
        def body(p, carry):
            fetch_page(0, 0, p)
            return carry
        lax.fori_loop(0, n_pages, body, 0)

    nxt = jnp.minimum(b + 1, nb - 1)
    for p in range(n_pages):
        fetch_page(nxt, 1 - slot, p)

    for hd in range(n_heads):
        qs_ref[hd * t_new:(hd + 1) * t_new, :] = q_ref[:, hd * 2 * LANES:(hd + 1) * 2 * LANES]
    m_ref[...] = jnp.full(m_ref.shape, NEG_INF, F32)
    l_ref[...] = jnp.zeros(l_ref.shape, F32)
    acc_ref[...] = jnp.zeros(acc_ref.shape, F32)
    qs = qs_ref[...]
    ql = qs[:, :LANES].astype(BF16)
    qp = qs[:, LANES:LANES + rope].astype(BF16)

    drain(slot)
    for c in range(n_pages // cpages):
        kv = kvbuf[slot, c * cpages:(c + 1) * cpages].reshape(cpages * page, LANES).astype(BF16)
        rp = jnp.concatenate([rpbuf[slot, c * cpages + p] for p in range(cpages)], axis=1).astype(BF16)
        s = (lax.dot_general(ql, kv, _NT, preferred_element_type=F32)
             + jnp.dot(qp, rp, preferred_element_type=F32))
        _softmax_update(s, kv, m_ref, l_ref, acc_ref)

    kn = jnp.concatenate([ckvn_ref[...], jnp.zeros((page - t_new, LANES), F32)], axis=0).astype(BF16)
    rn = jnp.concatenate([kpen_ref[...], jnp.zeros((page - t_new, rope), F32)], axis=0).astype(BF16)
    s2 = (lax.dot_general(ql, kn, _NT, preferred_element_type=F32)
          + lax.dot_general(qp, rn, _NT, preferred_element_type=F32))
    t = lax.broadcasted_iota(jnp.int32, s2.shape, 0) & (t_new - 1)
    col = lax.broadcasted_iota(jnp.int32, s2.shape, 1)
    s2 = jnp.where(col <= t, s2, NEG_INF)
    _softmax_update(s2, kn, m_ref, l_ref, acc_ref)
    out = acc_ref[...] / l_ref[...]
    for hd in range(n_heads):
        o_ref[:, hd * LANES:(hd + 1) * LANES] = out[hd * t_new:(hd + 1) * t_new, :]

    @pl.when(b == nb - 1)
    def _():
        drain(1 - slot)


def _sample_attention(q, ckv_new, kpe_new, cache_kv, cache_rope_t, page_table, layer, *, n_heads, t_new, cpages):
    n = q.shape[0]
    db = n // t_new
    n_pages = page_table.shape[1]
    page = cache_kv.shape[2]
    rope = cache_rope_t.shape[2]
    assert n_pages % cpages == 0 and t_new == 8 and cache_kv.shape[-1] == LANES and cache_rope_t.shape[3] == page
    rows = n_heads * t_new
    per_seq = lambda b, pt: (b, 0)
    return pl.pallas_call(
        functools.partial(_sample_attn_kernel, layer=layer, n_pages=n_pages, cpages=cpages, n_heads=n_heads,
                          t_new=t_new, rope=rope, page=page),
        grid_spec=pltpu.PrefetchScalarGridSpec(
            num_scalar_prefetch=1,
            grid=(db,),
            in_specs=[pl.BlockSpec((t_new, q.shape[1]), per_seq),
                      pl.BlockSpec((t_new, LANES), per_seq),
                      pl.BlockSpec((t_new, rope), per_seq),
                      pl.BlockSpec(memory_space=pl.ANY),
                      pl.BlockSpec(memory_space=pl.ANY)],
            out_specs=pl.BlockSpec((t_new, n_heads * LANES), per_seq),
            scratch_shapes=[pltpu.VMEM((2, n_pages, page, LANES), cache_kv.dtype),
                            pltpu.VMEM((2, n_pages, rope, page), cache_rope_t.dtype),
                            pltpu.VMEM((rows, 2 * LANES), F32),
                            pltpu.VMEM((rows, 1), F32),
                            pltpu.VMEM((rows, 1), F32),
                            pltpu.VMEM((rows, LANES), F32),
                            pltpu.SemaphoreType.DMA((2, 2))]),
        out_shape=jax.ShapeDtypeStruct((n, n_heads * LANES), F32),
        compiler_params=_cparams(("arbitrary",)),
    )(page_table, q, ckv_new, kpe_new, cache_kv, cache_rope_t)


def _mixer_out_kernel(xp_ref, xs_ref, ycp_ref, ycs_ref, atp_ref, ats_ref, wuv_ref, ga_ref, wo_ref, g_ref, b_ref,
                      x1_ref, x1p_ref, *, n_prompt_tiles, conv_ch, alpha):
    is_p = pl.program_id(0) < n_prompt_tiles
    tm = x1_ref.shape[0]
    sub = x1p_ref.shape[0] // tm
    n_chunks = 2 if tm % 32 == 0 else 1
    cm = tm // n_chunks
    for c in range(n_chunks):
        rows = slice(c * cm, (c + 1) * cm)
        x = jnp.where(is_p, xp_ref[rows, :], xs_ref[rows, :])
        yc = jnp.where(is_p, ycp_ref[rows, :], ycs_ref[rows, :])
        at = jnp.where(is_p, atp_ref[rows, :], ats_ref[rows, :].astype(BF16))
        o = jnp.dot(at, wuv_ref[...], preferred_element_type=F32)
        ya = _rms(o, ga_ref[...]).astype(BF16)
        mix = (jnp.dot(yc, wo_ref[:conv_ch, :], preferred_element_type=F32)
               + jnp.dot(ya, wo_ref[conv_ch:, :], preferred_element_type=F32))
        x1 = _layer_norm(alpha * x + mix, g_ref[...], b_ref[...])
        x1_ref[rows, :] = x1
        _pack_rows(x1, x1p_ref.at[pl.ds(c * cm * sub, cm * sub)])


def _mixer_out(xp, xs, ycp, ycs, atp, ats, wuv_bd, ga, wo, g, b, *, tm, alpha):
    np_, d = xp.shape
    ns = xs.shape[0]
    assert np_ % tm == 0 and ns % tm == 0
    npt, nst = np_ // tm, ns // tm
    conv_ch = ycp.shape[1]
    p_map = lambda i: (jnp.minimum(i, npt - 1), 0)
    s_map = lambda i: (jnp.maximum(i - npt, 0), 0)
    const = lambda i: (0, 0)
    tile = lambda i: (i, 0)
    nt = np_ + ns
    sub = d // (2 * LANES)
    return pl.pallas_call(
        functools.partial(_mixer_out_kernel, n_prompt_tiles=npt, conv_ch=conv_ch, alpha=alpha),
        grid=(npt + nst,),
        in_specs=[pl.BlockSpec((tm, d), p_map), pl.BlockSpec((tm, d), s_map),
                  pl.BlockSpec((tm, conv_ch), p_map), pl.BlockSpec((tm, conv_ch), s_map),
                  pl.BlockSpec((tm, atp.shape[1]), p_map), pl.BlockSpec((tm, ats.shape[1]), s_map),
                  pl.BlockSpec(wuv_bd.shape, const), pl.BlockSpec(ga.shape, const),
                  pl.BlockSpec(wo.shape, const), pl.BlockSpec(g.shape, const), pl.BlockSpec(b.shape, const)],
        out_specs=[pl.BlockSpec((tm, d), tile), pl.BlockSpec((tm * sub, LANES), tile)],
        out_shape=[jax.ShapeDtypeStruct((nt, d), F32), jax.ShapeDtypeStruct((nt * sub, LANES), U32)],
        compiler_params=_cparams(("parallel",)),
    )(xp, xs, ycp, ycs, atp, ats, wuv_bd, ga, wo, g, b)


def _router_kernel(x_ref, wh_ref, wl_ref, b_ref, tri_ref, idx_ref, w_ref, rank_ref, cnt_ref, base_ref,
                   *, tt, n_exp):
    @pl.when(pl.program_id(0) == 0)
    def _():
        base_ref[...] = jnp.zeros(base_ref.shape, F32)

    x = x_ref[...]
    xh = x.astype(BF16)
    xl = (x - xh.astype(F32)).astype(BF16)
    wh = wh_ref[...]
    z = (lax.dot_general(wh, xh, _NT, preferred_element_type=F32)
         + lax.dot_general(wh, xl, _NT, preferred_element_type=F32)
         + lax.dot_general(wl_ref[...], xh, _NT, preferred_element_type=F32))
    s = 1.0 / (1.0 + jnp.exp(-z))
    sb = s + b_ref[...]

    gsz = n_exp // N_GROUPS
    git = lax.broadcasted_iota(jnp.int32, (gsz, tt), 0).astype(F32)
    blocks, gscore = [], []
    for g in range(N_GROUPS):
        blk = sb[g * gsz:(g + 1) * gsz, :]
        m1 = jnp.max(blk, axis=0, keepdims=True)
        f1 = jnp.min(jnp.where(blk == m1, git, float(gsz)), axis=0, keepdims=True)
        m2 = jnp.max(jnp.where(git == f1, -jnp.inf, blk), axis=0, keepdims=True)
        blocks.append(blk)
        gscore.append(m1 + m2)
    masked = []
    for g in range(N_GROUPS):
        ahead = jnp.zeros((1, tt), F32)
        for g2 in range(N_GROUPS):
            if g2 == g:
                continue
            beats = (gscore[g2] >= gscore[g]) if g2 < g else (gscore[g2] > gscore[g])
            ahead = ahead + jnp.where(beats, 1.0, 0.0)
        masked.append(jnp.where(ahead < float(TOPK_GROUPS), blocks[g], -jnp.inf))
    vals = jnp.concatenate(masked, axis=0)

    rowi = lax.broadcasted_iota(jnp.int32, (n_exp, tt), 0).astype(F32)
    vals0 = vals
    picks, wks = [], []
    for k in range(TOP_K):
        m = jnp.max(vals, axis=0, keepdims=True)
        ik = jnp.min(jnp.where(vals == m, rowi, float(n_exp)), axis=0, keepdims=True)
        hit = rowi == ik
        wks.append(jnp.sum(jnp.where(hit, s, 0.0), axis=0, keepdims=True))
        vals = jnp.where(hit, -jnp.inf, vals)
        picks.append(ik)
    chosen = jnp.where(vals != vals0, 1.0, 0.0)
    wsum = wks[0]
    for k in range(1, TOP_K):
        wsum = wsum + wks[k]

    incl = jnp.dot(chosen.astype(BF16), tri_ref[...], preferred_element_type=F32)
    rnk = base_ref[...] + (incl - chosen)
    for k in range(TOP_K):
        idx_ref[k:k + 1, :] = picks[k].astype(jnp.int32)
        w_ref[k:k + 1, :] = wks[k] / wsum * ROUTED_SCALE
        rk = jnp.sum(jnp.where(rowi == picks[k], rnk, 0.0), axis=0, keepdims=True)
        rank_ref[k:k + 1, :] = rk.astype(jnp.int32)
    base = base_ref[...] + jnp.sum(chosen, axis=1, keepdims=True)
    base_ref[...] = base
    cnt_ref[...] = jnp.broadcast_to(base, cnt_ref.shape)


def _router(x1, wr_hi, wr_lo, bias, *, tt):
    nt, d = x1.shape
    n_exp = wr_hi.shape[0]
    assert nt % tt == 0
    tri = jnp.triu(jnp.ones((tt, tt), BF16))
    const = lambda i: (0, 0)
    col = lambda i: (0, i)
    return pl.pallas_call(
        functools.partial(_router_kernel, tt=tt, n_exp=n_exp),
        grid=(nt // tt,),
        in_specs=[pl.BlockSpec((tt, d), lambda i: (i, 0)),
                  pl.BlockSpec(wr_hi.shape, const), pl.BlockSpec(wr_lo.shape, const),
                  pl.BlockSpec(bias.shape, const), pl.BlockSpec(tri.shape, const)],
        out_specs=[pl.BlockSpec((TOP_K, tt), col), pl.BlockSpec((TOP_K, tt), col),
                   pl.BlockSpec((TOP_K, tt), col), pl.BlockSpec((n_exp, LANES), const)],
        out_shape=[jax.ShapeDtypeStruct((TOP_K, nt), jnp.int32), jax.ShapeDtypeStruct((TOP_K, nt), F32),
                   jax.ShapeDtypeStruct((TOP_K, nt), jnp.int32), jax.ShapeDtypeStruct((n_exp, LANES), F32)],
        scratch_shapes=[pltpu.VMEM((n_exp, 1), F32)],
        compiler_params=_cparams(("arbitrary",)),
    )(x1, wr_hi, wr_lo, bias, tri)


def _slot_rows_kernel(idx_ref, rank_ref, ps_ref, o_ref, *, n_exp, sub):
    tt = idx_ref.shape[1]
    rowi = lax.broadcasted_iota(jnp.int32, (n_exp, tt), 0)
    ps = ps_ref[...]
    for k in range(TOP_K):
        start = jnp.sum(jnp.where(rowi == idx_ref[k:k + 1, :], ps, 0.0), axis=0, keepdims=True)
        o_ref[k:k + 1, :] = (start.astype(jnp.int32) + rank_ref[k:k + 1, :]) * sub


def _slot_rows(idx_t, rank_t, pad_start, *, tt, sub):
    nt = idx_t.shape[1]
    n_exp = pad_start.shape[0]
    assert nt % tt == 0 and n_exp * SLOT_ROWS + nt * TOP_K < 2 ** 24
    col = lambda i: (0, i)
    return pl.pallas_call(
        functools.partial(_slot_rows_kernel, n_exp=n_exp, sub=sub),
        grid=(nt // tt,),
        in_specs=[pl.BlockSpec((TOP_K, tt), col), pl.BlockSpec((TOP_K, tt), col),
                  pl.BlockSpec((n_exp, 1), lambda i: (0, 0))],
        out_specs=pl.BlockSpec((TOP_K, tt), col),
        out_shape=jax.ShapeDtypeStruct((TOP_K, nt), jnp.int32),
        compiler_params=_cparams(("parallel",)),
    )(idx_t, rank_t, pad_start.astype(F32).reshape(n_exp, 1))


def _dispatch_kernel(ps_ref, nb_ref, cnt_ref, dest_ref, x_ref, o_ref, zbuf, sem, zsem, *, td, sub, n_exp):
    i = pl.program_id(0)
    blk = SLOT_ROWS * sub

    @pl.when(i == 0)
    def _():
        zbuf[...] = jnp.zeros(zbuf.shape, zbuf.dtype)
        n_blocks = o_ref.shape[0] // blk
        used = ps_ref[n_exp - 1] // SLOT_ROWS + nb_ref[n_exp - 1]

        def zero_block(b):
            return pltpu.make_async_copy(zbuf, o_ref.at[pl.ds(pl.multiple_of(b * blk, blk), blk)], zsem)

        def partial(e):
            return (cnt_ref[e] & (SLOT_ROWS - 1)) != 0

        def fill(e, carry):
            @pl.when(partial(e))
            def _():
                zero_block(ps_ref[e] // SLOT_ROWS + nb_ref[e] - 1).start()
            return carry

        def fill_done(e, carry):
            @pl.when(partial(e))
            def _():
                zero_block(0).wait()
            return carry

        def tail(b, carry):
            zero_block(b).start()
            return carry

        def tail_done(b, carry):
            zero_block(0).wait()
            return carry

        lax.fori_loop(0, n_exp, fill, 0)
        lax.fori_loop(used, n_blocks, tail, 0)
        lax.fori_loop(0, n_exp, fill_done, 0)
        lax.fori_loop(used, n_blocks, tail_done, 0)

    def per_token(t, carry):
        src = x_ref.at[pl.ds(pl.multiple_of(t * sub, sub), sub)]
        for k in range(TOP_K):
            dst = pl.multiple_of(dest_ref[k, t], sub)
            pltpu.make_async_copy(src, o_ref.at[pl.ds(dst, sub)], sem).start(priority=k % 2)
        return carry

    lax.fori_loop(0, td, per_token, 0)
    for _ in range(TOP_K):
        pltpu.make_async_copy(x_ref, o_ref.at[pl.ds(0, td * sub)], sem).wait()


def _dispatch(pad_start, n_blk, counts, dest_t, x1p, n_slots, *, td, sub):
    nt = x1p.shape[0] // sub
    assert nt % td == 0 and n_slots >= td and n_slots % SLOT_ROWS == 0
    return pl.pallas_call(
        functools.partial(_dispatch_kernel, td=td, sub=sub, n_exp=counts.shape[0]),
        grid_spec=pltpu.PrefetchScalarGridSpec(
            num_scalar_prefetch=3,
            grid=(nt // td,),
            in_specs=[pl.BlockSpec((TOP_K, td), lambda i, *_: (0, i), memory_space=pltpu.SMEM),
                      pl.BlockSpec((td * sub, LANES), lambda i, *_: (i, 0))],
            out_specs=pl.BlockSpec(memory_space=pl.ANY),
            scratch_shapes=[pltpu.VMEM((SLOT_ROWS * sub, LANES), x1p.dtype), pltpu.SemaphoreType.DMA(()),
                            pltpu.SemaphoreType.DMA(())]),
        out_shape=jax.ShapeDtypeStruct((n_slots * sub, LANES), x1p.dtype),
        compiler_params=_cparams(("arbitrary",)),
    )(pad_start, n_blk, counts, dest_t, x1p)


EXPERT_RING = 8


def _experts_kernel(ps_ref, nb_ref, wg_ref, wu_ref, wd_ref, xs_hbm, ys_hbm, wg_s, wu_s, wd_s, xbuf, ybuf,
                    xsem, ysem, zsem, *, sub, n_exp):
    e = pl.program_id(0)
    blk = SLOT_ROWS * sub
    n_blocks = xs_hbm.shape[0] // blk
    used = ps_ref[n_exp - 1] // SLOT_ROWS + nb_ref[n_exp - 1]

    def rows(g):
        return pl.ds(pl.multiple_of(g * blk, blk), blk)

    def ring(buf, s, n=1):
        return buf.at[pl.ds(pl.multiple_of(s * blk, blk), n * blk)]

    def x_copy(g, s):
        return pltpu.make_async_copy(xs_hbm.at[rows(g)], ring(xbuf, s), xsem.at[s])

    def y_copy(g, s):
        return pltpu.make_async_copy(ring(ybuf, s), ys_hbm.at[rows(g)], ysem.at[s])

    @pl.when(e == 0)
    def _():
        zbuf = ring(ybuf, EXPERT_RING - 1)
        zbuf[...] = jnp.zeros(zbuf.shape, zbuf.dtype)

        def fill(b, carry):
            pltpu.make_async_copy(zbuf, ys_hbm.at[rows(b)], zsem).start()
            return carry

        def fill_done(b, carry):
            pltpu.make_async_copy(zbuf, ys_hbm.at[rows(0)], zsem).wait()
            return carry

        lax.fori_loop(used, n_blocks, fill, 0)
        lax.fori_loop(used, n_blocks, fill_done, 0)
        for s in range(EXPERT_RING):
            @pl.when(s < used)
            def _():
                x_copy(s, s).start()

    nb = nb_ref[e]
    first = ps_ref[e] // SLOT_ROWS

    @pl.when(nb > 0)
    def _():
        wg_s[...] = wg_ref[...].astype(BF16)
        wu_s[...] = wu_ref[...].astype(BF16)
        wd_s[...] = wd_ref[...].astype(BF16)

        def run(g, n):
            s = g & (EXPERT_RING - 1)
            for d in range(n):
                x_copy(g + d, s + d).wait()

            @pl.when(g >= EXPERT_RING)
            def _():
                for d in range(n):
                    y_copy(g + d, s + d).wait()

            lo, hi = _unpack_rows(ring(xbuf, s, n), 0, n * SLOT_ROWS, sub)
            x = jnp.concatenate(lo + hi, axis=1).astype(BF16)
            gate = jnp.dot(x, wg_s[...], preferred_element_type=F32)
            up = jnp.dot(x, wu_s[...], preferred_element_type=F32)
            hmid = (_silu(gate) * up).astype(BF16)
            _pack_rows(jnp.dot(hmid, wd_s[...], preferred_element_type=F32), ring(ybuf, s, n))
            for d in range(n):
                y_copy(g + d, s + d).start()
            for d in range(n):
                @pl.when(g + d + EXPERT_RING < used)
                def _():
                    x_copy(g + d + EXPERT_RING, s + d).start()

        lead = first & 1

        @pl.when(lead == 1)
        def _():
            run(first, 1)

        def pair(p, carry):
            run(first + lead + 2 * p, 2)
            return carry

        lax.fori_loop(0, (nb - lead) // 2, pair, 0)

        @pl.when(((nb - lead) & 1) == 1)
        def _():
            run(first + nb - 1, 1)

    @pl.when(e == n_exp - 1)
    def _():
        for s in range(EXPERT_RING):
            @pl.when(s < used)
            def _():
                y_copy(0, s).wait()


def _experts(pad_start, n_blk, xs, w_gate, w_up, w_down, layer):
    n_exp, d, de = w_gate.shape[-3:]
    sub = d // (2 * LANES)
    blk = SLOT_ROWS * sub
    assert xs.shape[0] % blk == 0 and xs.shape[1] == LANES
    return pl.pallas_call(
        functools.partial(_experts_kernel, sub=sub, n_exp=n_exp),
        grid_spec=pltpu.PrefetchScalarGridSpec(
            num_scalar_prefetch=2,
            grid=(n_exp,),
            in_specs=[pl.BlockSpec((None, None, d, de), lambda e, ps, nb: (layer, e, 0, 0)),
                      pl.BlockSpec((None, None, d, de), lambda e, ps, nb: (layer, e, 0, 0)),
                      pl.BlockSpec((None, None, de, d), lambda e, ps, nb: (layer, e, 0, 0)),
                      pl.BlockSpec(memory_space=pl.ANY)],
            out_specs=pl.BlockSpec(memory_space=pl.ANY),
            scratch_shapes=[pltpu.VMEM((d, de), BF16), pltpu.VMEM((d, de), BF16), pltpu.VMEM((de, d), BF16),
                            pltpu.VMEM((EXPERT_RING * blk, LANES), U32), pltpu.VMEM((EXPERT_RING * blk, LANES), U32),
                            pltpu.SemaphoreType.DMA((EXPERT_RING,)), pltpu.SemaphoreType.DMA((EXPERT_RING,)),
                            pltpu.SemaphoreType.DMA(())]),
        out_shape=jax.ShapeDtypeStruct(xs.shape, U32),
        compiler_params=_cparams(("arbitrary",)),
    )(pad_start, n_blk, w_gate, w_up, w_down, xs)


def _combine_kernel(dest0_ref, dest1_ref, w_ref, x1_ref, wsg_ref, wsu_ref,
                    wsd_ref, g_ref, b_ref, ys_ref, yp_ref, ysm_ref, buf, sem, *, tc, alpha, sub, n_prompt_tiles):
    i = pl.program_id(0)
    n = pl.num_programs(0)

    def gather_row(dest_ref, s, k, t):
        src = pl.multiple_of(dest_ref[k, t], sub)
        dst = (k * tc + t) * sub
        if not isinstance(dst, int):
            dst = pl.multiple_of(dst, sub)
        pltpu.make_async_copy(ys_ref.at[pl.ds(src, sub)], buf.at[s, pl.ds(dst, sub)], sem.at[s]).start(priority=k % 2)

    def drain(s):
        pltpu.make_async_copy(ys_ref.at[pl.ds(0, TOP_K * tc * sub)], buf.at[s], sem.at[s]).wait()

    @pl.when(i == 0)
    def _():
        def per_token(t, carry):
            for k in range(TOP_K):
                gather_row(dest0_ref, 0, k, t)
            return carry
        lax.fori_loop(0, tc, per_token, 0)

    slot = i % 2
    for t in range(tc):
        for k in range(TOP_K):
            gather_row(dest1_ref, 1 - slot, k, t)

    x1 = x1_ref[...]
    xb = x1.astype(BF16)
    hs = (_silu(jnp.dot(xb, wsg_ref[...], preferred_element_type=F32))
          * jnp.dot(xb, wsu_ref[...], preferred_element_type=F32)).astype(BF16)
    shared = jnp.dot(hs, wsd_ref[...], preferred_element_type=F32)

    drain(slot)
    w = w_ref[...]
    lo_acc = [jnp.zeros((tc, LANES), F32) for _ in range(sub)]
    hi_acc = [jnp.zeros((tc, LANES), F32) for _ in range(sub)]
    for k in range(TOP_K):
        lo, hi = _unpack_rows(buf.at[slot], k * tc, tc, sub)
        wk = w[:, k:k + 1]
        for j in range(sub):
            lo_acc[j] = lo_acc[j] + wk * lo[j]
            hi_acc[j] = hi_acc[j] + wk * hi[j]
    moe = jnp.concatenate(lo_acc + hi_acc, axis=1) + shared
    y = _layer_norm(alpha * x1 + moe, g_ref[...], b_ref[...])

    @pl.when(i < n_prompt_tiles)
    def _():
        yp_ref[...] = y

    @pl.when(i >= n_prompt_tiles)
    def _():
        ysm_ref[...] = y

    @pl.when(i == n - 1)
    def _():
        drain(1 - slot)


def _combine(dest_t, w_tok, x1, wsg, wsu, wsd, g, b, ys, *, tc, alpha, n_prompt):
    nt, d = x1.shape
    n_tiles = nt // tc
    sub = d // (2 * LANES)
    assert nt % tc == 0 and n_prompt % tc == 0 and 0 < n_prompt < nt and ys.shape[0] >= TOP_K * tc * sub
    npt = n_prompt // tc
    cur = pl.BlockSpec((TOP_K, tc), lambda i: (0, i), memory_space=pltpu.SMEM)
    nxt = pl.BlockSpec((TOP_K, tc), lambda i: (0, jnp.minimum(i + 1, n_tiles - 1)), memory_space=pltpu.SMEM)
    const = lambda i: (0, 0)
    tile = lambda i: (i, 0)
    return pl.pallas_call(
        functools.partial(_combine_kernel, tc=tc, alpha=alpha, sub=sub, n_prompt_tiles=npt),
        grid=(n_tiles,),
        in_specs=[cur, nxt,
                  pl.BlockSpec((tc, TOP_K), tile),
                  pl.BlockSpec((tc, d), tile),
                  pl.BlockSpec(wsg.shape, const), pl.BlockSpec(wsu.shape, const), pl.BlockSpec(wsd.shape, const),
                  pl.BlockSpec(g.shape, const), pl.BlockSpec(b.shape, const),
                  pl.BlockSpec(memory_space=pl.ANY)],
        out_specs=[pl.BlockSpec((tc, d), lambda i: (jnp.minimum(i, npt - 1), 0)),
                   pl.BlockSpec((tc, d), lambda i: (jnp.maximum(i - npt, 0), 0))],
        out_shape=[jax.ShapeDtypeStruct((n_prompt, d), F32), jax.ShapeDtypeStruct((nt - n_prompt, d), F32)],
        scratch_shapes=[pltpu.VMEM((2, TOP_K * tc * sub, LANES), ys.dtype), pltpu.SemaphoreType.DMA((2,))],
        compiler_params=_cparams(("arbitrary",)),
    )(dest_t, dest_t, w_tok, x1, wsg, wsu, wsd, g, b, ys)


def _rope_cs(pos, rope):
    half = rope // 2
    inv = ROPE_BASE ** (-jnp.arange(half, dtype=F32) / half)
    ang = pos.astype(F32)[:, None] * inv[None, :]
    cos, sin = jnp.cos(ang), jnp.sin(ang)
    pad = jnp.zeros((pos.shape[0], LANES - rope), F32)
    return (jnp.concatenate([cos, cos, pad], axis=1), jnp.concatenate([-sin, sin, pad], axis=1))


def _pick_tile(n, pref):
    t = pref
    while n % t:
        t //= 2
    return t


def kernel(x_prompt, x_sample, cache_kv_latent, cache_k_rope, state_conv, page_table, w_in, conv_w, q_norm, w_uq, kv_norm, w_uk, w_uv, g_conv, g_attn, w_o, ln1_g, ln1_b, w_router, router_bias, w_gate, w_up, w_down, ws_gate, ws_up, ws_down, ln2_g, ln2_b):
    depth = w_in.shape[0]
    bsz, seq, d = x_prompt.shape
    db, t_new, _ = x_sample.shape
    kv_lora, n_heads, nope = w_uk.shape[1:]
    v_dim = w_uv.shape[-1]
    rope = cache_k_rope.shape[-1]
    q_lora = q_norm.shape[-1]
    conv_ch = conv_w.shape[-1]
    n_exp = w_router.shape[-1]
    page = cache_kv_latent.shape[2]
    past_len = page_table.shape[1] * page
    alpha = (2.0 * depth) ** 0.25
    scale = float((nope + rope) ** -0.5) * math.log2(math.e)
    in_cols = w_in.shape[-1]
    in_pad = -(-(in_cols - rope + LANES) // LANES) * LANES
    n_p, n_s = bsz * seq, db * t_new
    nt = n_p + n_s

    tm_p = _pick_tile(seq, 512)
    tm_s = _pick_tile(n_s, 512)
    tq = _pick_tile(seq, 256)
    tm_o = _pick_tile(math.gcd(n_p, n_s), 1024)
    tt = _pick_tile(math.gcd(n_p, n_s), 512)
    td = max(t for t in range(LANES, 4096 + 1, LANES) if nt % t == 0) if nt % LANES == 0 else _pick_tile(nt, 1024)
    tc = _pick_tile(math.gcd(n_p, n_s), 256)
    pps = _pick_tile(page_table.shape[1], 64)
    cache_rope_t = jnp.swapaxes(cache_k_rope, 2, 3)

    cos_p, sin_p = _rope_cs(jnp.arange(seq, dtype=jnp.int32), rope)
    cos_s, sin_s = _rope_cs(past_len + jnp.arange(t_new, dtype=jnp.int32), rope)
    cos_s, sin_s = jnp.tile(cos_s, (tm_s // t_new, 1)), jnp.tile(sin_s, (tm_s // t_new, 1))

    xp, xs_ = x_prompt.reshape(n_p, d), x_sample.reshape(n_s, d)
    outs = [[] for _ in range(6)]
    for l in range(depth):
        win = jnp.pad(w_in[l], ((0, 0), (0, in_pad - in_cols))).astype(BF16)
        wuq3 = w_uq[l].reshape(q_lora, n_heads, nope + rope)
        qlat = _fold_qlat(jnp.transpose(wuq3[:, :, :nope], (1, 0, 2)), jnp.transpose(w_uk[l], (1, 2, 0)), scale)
        wq_rope = jnp.transpose(wuq3[:, :, nope:], (1, 0, 2)) * scale
        wq = jnp.concatenate([qlat, wq_rope, jnp.zeros((n_heads, q_lora, LANES - rope), F32)], axis=2)
        wq = jnp.transpose(wq, (1, 0, 2)).reshape(q_lora, n_heads * 2 * LANES).astype(BF16)
        eye = jnp.eye(n_heads, dtype=F32)
        wuv_bd = jnp.einsum('chv,hg->hcgv', w_uv[l], eye).reshape(n_heads * kv_lora, n_heads * v_dim).astype(BF16)
        wo = w_o[l].astype(BF16)
        wr_t = w_router[l].T
        wr_hi = wr_t.astype(BF16)
        wr_lo = (wr_t - wr_hi.astype(F32)).astype(BF16)
        row = lambda v: v[l].reshape(1, -1)

        common = (win, wq, row(q_norm), row(kv_norm), conv_w[l], row(g_conv))
        q_p, kc_p, ckv_p, kpe_p, u_p, yc_p = _mixer_in(
            xp, *common, cos_p, sin_p, None, prompt=True, seq=seq, tm=tm_p, n_heads=n_heads, rope=rope, q_dtype=BF16)
        st = state_conv[l].astype(F32)
        zero = jnp.zeros((db, conv_ch), F32)
        prev1 = jnp.stack([st[:, 1]] + [zero] * (t_new - 1), axis=1).reshape(n_s, conv_ch)
        prev2 = jnp.stack([st[:, 0], st[:, 1]] + [zero] * (t_new - 2), axis=1).reshape(n_s, conv_ch)
        q_s, kc_s, ckv_s, kpe_s, u_s, yc_s = _mixer_in(
            xs_, *common, cos_s, sin_s, (prev1, prev2), prompt=False, seq=t_new, tm=tm_s, n_heads=n_heads, rope=rope,
            q_dtype=F32)
        del kc_s

        at_p = _prompt_attention(q_p, kc_p, batch=bsz, seq=seq, tq=tq, n_heads=n_heads)
        at_s = _sample_attention(q_s, ckv_s, kpe_s, cache_kv_latent, cache_rope_t, page_table, l,
                                 n_heads=n_heads, t_new=t_new, cpages=pps)

        x1, x1p = _mixer_out(xp, xs_, yc_p, yc_s, at_p, at_s, wuv_bd, row(g_attn), wo, row(ln1_g), row(ln1_b),
                             tm=tm_o, alpha=alpha)

        idx_t, w_t, rank_t, cnt = _router(x1, wr_hi, wr_lo, router_bias[l].reshape(n_exp, 1), tt=tt)
        counts = cnt[:, 0].astype(jnp.int32)
        padded = (counts + SLOT_ROWS - 1) // SLOT_ROWS * SLOT_ROWS
        pad_end = jnp.cumsum(padded)
        pad_start = (pad_end - padded).astype(jnp.int32)
        n_blk = (padded // SLOT_ROWS).astype(jnp.int32)
        n_blocks = -(-(nt * TOP_K) // SLOT_ROWS) + n_exp

        dest_t = _slot_rows(idx_t, rank_t, pad_start, tt=td, sub=d // (2 * LANES))
        xs_sorted = _dispatch(pad_start, n_blk, counts, dest_t, x1p, n_blocks * SLOT_ROWS, td=td, sub=d // (2 * LANES))
        ys = _experts(pad_start, n_blk, xs_sorted, w_gate, w_up, w_down, l)
        xp, xs_ = _combine(dest_t, w_t.T, x1, ws_gate[l].astype(BF16), ws_up[l].astype(BF16),
                           ws_down[l].astype(BF16), row(ln2_g), row(ln2_b), ys, tc=tc, alpha=alpha, n_prompt=n_p)
        outs[0].append(ckv_p.reshape(bsz, seq, kv_lora))
        outs[1].append(kpe_p.reshape(bsz, seq, rope))
        outs[2].append(u_p.reshape(bsz, seq, conv_ch)[:, seq - (CONV_W - 1):])
        outs[3].append(ckv_s.reshape(db, t_new, kv_lora))
        outs[4].append(kpe_s.reshape(db, t_new, rope))
        outs[5].append(u_s.reshape(db, t_new, conv_ch)[:, t_new - (CONV_W - 1):])
    return (xp.reshape(bsz, seq, d), xs_.reshape(db, t_new, d)) + tuple(jnp.stack(o) for o in outs)
```

```python
import functools
import math

import jax
import jax.numpy as jnp
from jax import lax
from jax.experimental import pallas as pl
from jax.experimental.pallas import tpu as pltpu

F32 = jnp.float32
BF16 = jnp.bfloat16
U32 = jnp.uint32

ROPE_BASE = 10000.0
NORM_EPS = 1e-6
LN_EPS = 1e-5
NEG_INF = -1e30
TOP_K = 8
N_GROUPS = 8
TOPK_GROUPS = 4
ROUTED_SCALE = 2.5
CONV_W = 3

LANES = 128
SLOT_ROWS = 256
VMEM_LIMIT = 52 * 1024 * 1024

_NT = (((1,), (1,)), ((), ()))


def _cparams(sem):
    return pltpu.CompilerParams(dimension_semantics=sem, vmem_limit_bytes=VMEM_LIMIT)


def _rms(x, g):
    return x * lax.rsqrt(jnp.mean(x * x, axis=-1, keepdims=True) + NORM_EPS) * g


def _layer_norm(x, g, b):
    mu = jnp.mean(x, axis=-1, keepdims=True)
    xc = x - mu
    var = jnp.mean(xc * xc, axis=-1, keepdims=True)
    return xc * lax.rsqrt(var + LN_EPS) * g + b


def _silu(x):
    return x / (1.0 + jnp.exp(-x))


def _rope_tile(p, c, s, half):
    lane = lax.broadcasted_iota(jnp.int32, p.shape, 1)
    swapped = jnp.where(lane < half, pltpu.roll(p, LANES - half, axis=1), pltpu.roll(p, half, axis=1))
    return p * c + swapped * s


def _pack_rows(x, ref):
    r = x.shape[0]
    half = x.shape[1] // 2
    sub = half // LANES
    bits = lax.bitcast_convert_type(x.astype(BF16).astype(F32), U32)
    words = bits[:, half:] | (bits[:, :half] >> 16)
    for j in range(sub):
        ref[pl.ds(j, r, stride=sub), :] = words[:, j * LANES:(j + 1) * LANES]


def _unpack_rows(ref, first, r, sub):
    lo, hi = [], []
    for j in range(sub):
        w = ref[pl.ds(first * sub + j, r, stride=sub), :]
        lo.append(lax.bitcast_convert_type(w << 16, F32))
        hi.append(lax.bitcast_convert_type(w & jnp.uint32(0xFFFF0000), F32))
    return lo, hi


def _fold_kernel(a_ref, b_ref, o_ref, *, scale):
    o_ref[...] = jnp.dot(a_ref[...], b_ref[...], precision=lax.Precision.HIGHEST,
                         preferred_element_type=F32) * scale


def _fold_qlat(wuq_nope, wuk_t, scale):
    h, r, n = wuq_nope.shape
    c = wuk_t.shape[-1]
    return pl.pallas_call(
        functools.partial(_fold_kernel, scale=scale),
        grid=(h,),
        in_specs=[pl.BlockSpec((None, r, n), lambda i: (i, 0, 0)),
                  pl.BlockSpec((None, n, c), lambda i: (i, 0, 0))],
        out_specs=pl.BlockSpec((None, r, c), lambda i: (i, 0, 0)),
        out_shape=jax.ShapeDtypeStruct((h, r, c), F32),
        compiler_params=_cparams(("parallel",)),
    )(wuq_nope, wuk_t)


def _mixer_in_kernel(*refs, prompt, tiles_per_seq, tm, n_chunks, t_new, n_heads, conv_ch, q_lora, kv_lora, rope):
    if prompt:
        (x_ref, win_ref, wq_ref, qn_ref, kvn_ref, cw_ref, gc_ref, cos_ref, sin_ref,
         q_ref, kc_ref, ckv_ref, kpe_ref, u_ref, yc_ref, carry_ref) = refs
    else:
        (x_ref, win_ref, wq_ref, qn_ref, kvn_ref, cw_ref, gc_ref, cos_ref, sin_ref, p1_ref, p2_ref,
         q_ref, kc_ref, ckv_ref, kpe_ref, u_ref, yc_ref) = refs
    half = rope // 2
    c1, c2, c3 = conv_ch, 2 * conv_ch, 3 * conv_ch
    c4 = c3 + q_lora
    c5 = c4 + kv_lora

    cm = tm // n_chunks
    row = lax.broadcasted_iota(jnp.int32, (cm, 1), 0)
    u_prev = None
    for c in range(n_chunks):
        rows = slice(c * cm, (c + 1) * cm)
        h = jnp.dot(x_ref[rows, :].astype(BF16), win_ref[...], preferred_element_type=F32)
        b_gate, c_gate, x_conv = h[:, :c1], h[:, c1:c2], h[:, c2:c3]
        q_a, c_kv, kp = h[:, c3:c4], h[:, c4:c5], h[:, c5:c5 + LANES]

        u = c_gate * x_conv
        u_ref[rows, :] = u
        if prompt:
            if c == 0:
                first = (pl.program_id(0) % tiles_per_seq) == 0
                prev1 = jnp.where(first, 0.0, carry_ref[7:8, :])
                prev2 = jnp.where(first, 0.0, carry_ref[6:7, :])
            else:
                prev1 = u_prev[cm - 1:cm, :]
                prev2 = u_prev[cm - 2:cm - 1, :]
            t_in = row
            p1 = prev1
            p2 = jnp.where(row == 0, prev2, prev1)
        else:
            t_in = row & (t_new - 1)
            p1 = p1_ref[rows, :]
            p2 = p2_ref[rows, :]
        um1 = jnp.where(t_in == 0, p1, pltpu.roll(u, 1, axis=0))
        um2 = jnp.where(t_in < 2, p2, pltpu.roll(u, 2, axis=0))
        conv_y = cw_ref[0:1, :] * um2 + cw_ref[1:2, :] * um1 + cw_ref[2:3, :] * u
        yc_ref[rows, :] = _rms(b_gate * conv_y, gc_ref[...]).astype(yc_ref.dtype)
        u_prev = u

        cos = cos_ref[rows, :]
        sin = sin_ref[rows, :]
        qn = _rms(q_a, qn_ref[...]).astype(BF16)
        q = jnp.dot(qn, wq_ref[...], preferred_element_type=F32)
        for hd in range(n_heads):
            o = hd * 2 * LANES
            q_ref[rows, o:o + LANES] = q[:, o:o + LANES].astype(q_ref.dtype)
            q_ref[rows, o + LANES:o + 2 * LANES] = _rope_tile(q[:, o + LANES:o + 2 * LANES], cos, sin,
                                                              half).astype(q_ref.dtype)

        ckv = _rms(c_kv, kvn_ref[...])
        ckv_ref[rows, :] = ckv
        kpr = _rope_tile(kp, cos, sin, half)
        kpe_ref[rows, :] = kpr[:, :rope]
        kc_ref[rows, :LANES] = ckv.astype(BF16)
        kc_ref[rows, LANES:] = kpr.astype(BF16)
    if prompt:
        carry_ref[...] = u_prev[cm - 8:, :]


def _mixer_in(x2, win, wq, qn, kvn, cw, gc, cos, sin, prev, *, prompt, seq, tm, n_heads, rope, q_dtype):
    n, d = x2.shape
    conv_ch = cw.shape[-1]
    q_lora = qn.shape[-1]
    kv_lora = kvn.shape[-1]
    assert n % tm == 0 and kv_lora == LANES and tm % 8 == 0
    tiles_per_seq = seq // tm if prompt else 1
    if prompt:
        assert seq % tm == 0
    else:
        assert seq & (seq - 1) == 0 and tm % seq == 0
    const = lambda i: (0, 0)
    tile = lambda i: (i, 0)
    in_specs = [
        pl.BlockSpec((tm, d), tile),
        pl.BlockSpec(win.shape, const),
        pl.BlockSpec(wq.shape, const),
        pl.BlockSpec(qn.shape, const),
        pl.BlockSpec(kvn.shape, const),
        pl.BlockSpec(cw.shape, const),
        pl.BlockSpec(gc.shape, const),
    ]
    args = [x2, win, wq, qn, kvn, cw, gc, cos, sin]
    if prompt:
        in_specs += [pl.BlockSpec((tm, LANES), lambda i: (i % tiles_per_seq, 0))] * 2
        scratch = [pltpu.VMEM((8, conv_ch), F32)]
    else:
        in_specs += [pl.BlockSpec((tm, LANES), const)] * 2
        in_specs += [pl.BlockSpec((tm, conv_ch), tile)] * 2
        args += list(prev)
        scratch = []
    qw = wq.shape[-1]
    out_shape = [
        jax.ShapeDtypeStruct((n, qw), q_dtype),
        jax.ShapeDtypeStruct((n, 2 * LANES), BF16),
        jax.ShapeDtypeStruct((n, kv_lora), F32),
        jax.ShapeDtypeStruct((n, rope), F32),
        jax.ShapeDtypeStruct((n, conv_ch), F32),
        jax.ShapeDtypeStruct((n, conv_ch), BF16),
    ]
    out_specs = [
        pl.BlockSpec((tm, qw), tile),
        pl.BlockSpec((tm, 2 * LANES), tile),
        pl.BlockSpec((tm, kv_lora), tile),
        pl.BlockSpec((tm, rope), tile),
        pl.BlockSpec((tm, conv_ch), tile),
        pl.BlockSpec((tm, conv_ch), tile),
    ]
    return pl.pallas_call(
        functools.partial(_mixer_in_kernel, prompt=prompt, tiles_per_seq=tiles_per_seq, tm=tm,
                          n_chunks=2 if tm % 32 == 0 else 1, t_new=seq,
                          n_heads=n_heads, conv_ch=conv_ch, q_lora=q_lora, kv_lora=kv_lora, rope=rope),
        grid=(n // tm,),
        in_specs=in_specs,
        out_specs=out_specs,
        out_shape=out_shape,
        scratch_shapes=scratch,
        compiler_params=_cparams(("arbitrary",)),
    )(*args)


MAX_KEY_BLOCKS = 4

def _softmax_update(s, v, m_ref, l_ref, acc_ref):
    m_prev = m_ref[...]
    m_new = jnp.maximum(m_prev, jnp.max(s, axis=1, keepdims=True))
    alpha = jnp.exp2(m_prev - m_new)
    p = jnp.exp2(s - m_new)
    l_ref[...] = alpha * l_ref[...] + jnp.sum(p, axis=1, keepdims=True)
    acc_ref[...] = alpha * acc_ref[...] + jnp.dot(p.astype(BF16), v, preferred_element_type=F32)
    m_ref[...] = m_new


def _prompt_attn_kernel(q_ref, k_ref, o_ref, qs_ref, m_ref, l_ref, acc_ref, *, tq, n_heads, chunk_heads):
    i = pl.program_id(1)
    cr = chunk_heads * tq
    n_chunks = n_heads // chunk_heads
    for hd in range(n_heads):
        qs_ref[hd * tq:(hd + 1) * tq, :] = q_ref[:, hd * 2 * LANES:(hd + 1) * 2 * LANES]
    m_ref[...] = jnp.full(m_ref.shape, NEG_INF, F32)
    l_ref[...] = jnp.zeros(l_ref.shape, F32)
    acc_ref[...] = jnp.zeros(acc_ref.shape, F32)

    def step(j, width, masked):
        k = k_ref[pl.ds(pl.multiple_of(j * tq, tq), width), :]
        v = k[:, :LANES]
        for c in range(n_chunks):
            rows = slice(c * cr, (c + 1) * cr)
            s = lax.dot_general(qs_ref[rows, :], k, _NT, preferred_element_type=F32)
            if masked:
                t = lax.broadcasted_iota(jnp.int32, s.shape, 0) & (tq - 1)
                col = lax.broadcasted_iota(jnp.int32, s.shape, 1)
                s = jnp.where(col <= t + (width - tq), s, NEG_INF)
            parts = [s[:, w * LANES:(w + 1) * LANES] for w in range(width // LANES)]
            mc = parts[0]
            for part in parts[1:]:
                mc = jnp.maximum(mc, part)
            m_prev = m_ref[rows, :]
            m_new = jnp.maximum(m_prev, jnp.max(mc, axis=1, keepdims=True))
            alpha = jnp.exp2(m_prev - m_new)
            ps = [jnp.exp2(part - m_new) for part in parts]
            psum = ps[0]
            for pp in ps[1:]:
                psum = psum + pp
            l_ref[rows, :] = alpha * l_ref[rows, :] + psum
            p = jnp.concatenate(ps, axis=1).astype(BF16)
            acc_ref[rows, :] = alpha * acc_ref[rows, :] + jnp.dot(p, v, preferred_element_type=F32)
            m_ref[rows, :] = m_new

    nq = k_ref.shape[0] // tq
    wmax = 1
    while wmax * 2 <= min(MAX_KEY_BLOCKS, nq):
        wmax *= 2

    def body(jj, carry):
        step(wmax * jj, wmax * tq, False)
        return carry

    lax.fori_loop(0, i // wmax, body, 0)
    for r in range(wmax):
        @pl.when(i % wmax == r)
        def _(r=r):
            step((i // wmax) * wmax, (r + 1) * tq, True)
    out = acc_ref[...] / jnp.sum(l_ref[...], axis=1, keepdims=True)
    for hd in range(n_heads):
        o_ref[:, hd * LANES:(hd + 1) * LANES] = out[hd * tq:(hd + 1) * tq, :].astype(o_ref.dtype)


def _prompt_attention(q, kc, *, batch, seq, tq, n_heads):
    n = q.shape[0]
    nq = seq // tq
    rows = n_heads * tq
    chunk_heads = 2 if n_heads % 2 == 0 else 1
    assert tq & (tq - 1) == 0 and seq % tq == 0 and tq % LANES == 0
    return pl.pallas_call(
        functools.partial(_prompt_attn_kernel, tq=tq, n_heads=n_heads, chunk_heads=chunk_heads),
        grid=(batch, nq),
        in_specs=[pl.BlockSpec((tq, q.shape[1]), lambda b, i: (b * nq + i, 0)),
                  pl.BlockSpec((seq, kc.shape[1]), lambda b, i: (b, 0))],
        out_specs=pl.BlockSpec((tq, n_heads * LANES), lambda b, i: (b * nq + i, 0)),
        out_shape=jax.ShapeDtypeStruct((n, n_heads * LANES), BF16),
        scratch_shapes=[pltpu.VMEM((rows, 2 * LANES), BF16),
                        pltpu.VMEM((rows, LANES), F32),
                        pltpu.VMEM((rows, LANES), F32),
                        pltpu.VMEM((rows, LANES), F32)],
        compiler_params=_cparams(("parallel", "arbitrary")),
    )(q, kc)


def _sample_attn_kernel(pt_ref, q_ref, ckvn_ref, kpen_ref, kv_hbm, rp_hbm, o_ref, kvbuf, rpbuf, qs_ref, m_ref, l_ref,
                        acc_ref, sem, *, layer, n_pages, cpages, n_heads, t_new, rope, page):
    b = pl.program_id(0)
    nb = pl.num_programs(0)
    slot = b % 2

    def fetch_page(seq, s, p):
        pg = pt_ref[seq, p]
        pltpu.make_async_copy(kv_hbm.at[layer, pg], kvbuf.at[s, p], sem.at[0, s]).start()
        pltpu.make_async_copy(rp_hbm.at[layer, pg], rpbuf.at[s, p], sem.at[1, s]).start()

    def drain(s):
        pltpu.make_async_copy(kv_hbm.at[layer, pl.ds(0, n_pages)], kvbuf.at[s], sem.at[0, s]).wait()
        pltpu.make_async_copy(rp_hbm.at[layer, pl.ds(0, n_pages)], rpbuf.at[s], sem.at[1, s]).wait()

    @pl.when(b == 0)
    def _():
        def body(p, carry):
            fetch_page(0, 0, p)
            return carry
        lax.fori_loop(0, n_pages, body, 0)

    nxt = jnp.minimum(b + 1, nb - 1)
    for p in range(n_pages):
        fetch_page(nxt, 1 - slot, p)

    for hd in range(n_heads):
        qs_ref[hd * t_new:(hd + 1) * t_new, :] = q_ref[:, hd * 2 * LANES:(hd + 1) * 2 * LANES]
    m_ref[...] = jnp.full(m_ref.shape, NEG_INF, F32)
    l_ref[...] = jnp.zeros(l_ref.shape, F32)
    acc_ref[...] = jnp.zeros(acc_ref.shape, F32)
    qs = qs_ref[...]
    ql = qs[:, :LANES].astype(BF16)
    qp = qs[:, LANES:LANES + rope].astype(BF16)

    drain(slot)
    for c in range(n_pages // cpages):
        kv = kvbuf[slot, c * cpages:(c + 1) * cpages].reshape(cpages * page, LANES).astype(BF16)
        rp = jnp.concatenate([rpbuf[slot, c * cpages + p] for p in range(cpages)], axis=1).astype(BF16)
        s = (lax.dot_general(ql, kv, _NT, preferred_element_type=F32)
             + jnp.dot(qp, rp, preferred_element_type=F32))
        _softmax_update(s, kv, m_ref, l_ref, acc_ref)

    kn = jnp.concatenate([ckvn_ref[...], jnp.zeros((page - t_new, LANES), F32)], axis=0).astype(BF16)
    rn = jnp.concatenate([kpen_ref[...], jnp.zeros((page - t_new, rope), F32)], axis=0).astype(BF16)
    s2 = (lax.dot_general(ql, kn, _NT, preferred_element_type=F32)
          + lax.dot_general(qp, rn, _NT, preferred_element_type=F32))
    t = lax.broadcasted_iota(jnp.int32, s2.shape, 0) & (t_new - 1)
    col = lax.broadcasted_iota(jnp.int32, s2.shape, 1)
    s2 = jnp.where(col <= t, s2, NEG_INF)
    _softmax_update(s2, kn, m_ref, l_ref, acc_ref)
    out = acc_ref[...] / l_ref[...]
    for hd in range(n_heads):
        o_ref[:, hd * LANES:(hd + 1) * LANES] = out[hd * t_new:(hd + 1) * t_new, :]

    @pl.when(b == nb - 1)
    def _():
        drain(1 - slot)


def _sample_attention(q, ckv_new, kpe_new, cache_kv, cache_rope_t, page_table, layer, *, n_heads, t_new, cpages):
    n = q.shape[0]
    db = n // t_new
    n_pages = page_table.shape[1]
    page = cache_kv.shape[2]
    rope = cache_rope_t.shape[2]
    assert n_pages % cpages == 0 and t_new == 8 and cache_kv.shape[-1] == LANES and cache_rope_t.shape[3] == page
    rows = n_heads * t_new
    per_seq = lambda b, pt: (b, 0)
    return pl.pallas_call(
        functools.partial(_sample_attn_kernel, layer=layer, n_pages=n_pages, cpages=cpages, n_heads=n_heads,
                          t_new=t_new, rope=rope, page=page),
        grid_spec=pltpu.PrefetchScalarGridSpec(
            num_scalar_prefetch=1,
            grid=(db,),
            in_specs=[pl.BlockSpec((t_new, q.shape[1]), per_seq),
                      pl.BlockSpec((t_new, LANES), per_seq),
                      pl.BlockSpec((t_new, rope), per_seq),
                      pl.BlockSpec(memory_space=pl.ANY),
                      pl.BlockSpec(memory_space=pl.ANY)],
            out_specs=pl.BlockSpec((t_new, n_heads * LANES), per_seq),
            scratch_shapes=[pltpu.VMEM((2, n_pages, page, LANES), cache_kv.dtype),
                            pltpu.VMEM((2, n_pages, rope, page), cache_rope_t.dtype),
                            pltpu.VMEM((rows, 2 * LANES), F32),
                            pltpu.VMEM((rows, 1), F32),
                            pltpu.VMEM((rows, 1), F32),
                            pltpu.VMEM((rows, LANES), F32),
                            pltpu.SemaphoreType.DMA((2, 2))]),
        out_shape=jax.ShapeDtypeStruct((n, n_heads * LANES), F32),
        compiler_params=_cparams(("arbitrary",)),
    )(page_table, q, ckv_new, kpe_new, cache_kv, cache_rope_t)


def _mixer_out_kernel(xp_ref, xs_ref, ycp_ref, ycs_ref, atp_ref, ats_ref, wuv_ref, ga_ref, wo_ref, g_ref, b_ref,
                      x1_ref, x1p_ref, *, n_prompt_tiles, conv_ch, alpha):
    is_p = pl.program_id(0) < n_prompt_tiles
    tm = x1_ref.shape[0]
    sub = x1p_ref.shape[0] // tm
    n_chunks = 2 if tm % 32 == 0 else 1
    cm = tm // n_chunks
    for c in range(n_chunks):
        rows = slice(c * cm, (c + 1) * cm)
        x = jnp.where(is_p, xp_ref[rows, :], xs_ref[rows, :])
        yc = jnp.where(is_p, ycp_ref[rows, :], ycs_ref[rows, :])
        at = jnp.where(is_p, atp_ref[rows, :], ats_ref[rows, :].astype(BF16))
        o = jnp.dot(at, wuv_ref[...], preferred_element_type=F32)
        ya = _rms(o, ga_ref[...]).astype(BF16)
        mix = (jnp.dot(yc, wo_ref[:conv_ch, :], preferred_element_type=F32)
               + jnp.dot(ya, wo_ref[conv_ch:, :], preferred_element_type=F32))
        x1 = _layer_norm(alpha * x + mix, g_ref[...], b_ref[...])
        x1_ref[rows, :] = x1
        _pack_rows(x1, x1p_ref.at[pl.ds(c * cm * sub, cm * sub)])


def _mixer_out(xp, xs, ycp, ycs, atp, ats, wuv_bd, ga, wo, g, b, *, tm, alpha):
    np_, d = xp.shape
    ns = xs.shape[0]
    assert np_ % tm == 0 and ns % tm == 0
    npt, nst = np_ // tm, ns // tm
    conv_ch = ycp.shape[1]
    p_map = lambda i: (jnp.minimum(i, npt - 1), 0)
    s_map = lambda i: (jnp.maximum(i - npt, 0), 0)
    const = lambda i: (0, 0)
    tile = lambda i: (i, 0)
    nt = np_ + ns
    sub = d // (2 * LANES)
    return pl.pallas_call(
        functools.partial(_mixer_out_kernel, n_prompt_tiles=npt, conv_ch=conv_ch, alpha=alpha),
        grid=(npt + nst,),
        in_specs=[pl.BlockSpec((tm, d), p_map), pl.BlockSpec((tm, d), s_map),
                  pl.BlockSpec((tm, conv_ch), p_map), pl.BlockSpec((tm, conv_ch), s_map),
                  pl.BlockSpec((tm, atp.shape[1]), p_map), pl.BlockSpec((tm, ats.shape[1]), s_map),
                  pl.BlockSpec(wuv_bd.shape, const), pl.BlockSpec(ga.shape, const),
                  pl.BlockSpec(wo.shape, const), pl.BlockSpec(g.shape, const), pl.BlockSpec(b.shape, const)],
        out_specs=[pl.BlockSpec((tm, d), tile), pl.BlockSpec((tm * sub, LANES), tile)],
        out_shape=[jax.ShapeDtypeStruct((nt, d), F32), jax.ShapeDtypeStruct((nt * sub, LANES), U32)],
        compiler_params=_cparams(("parallel",)),
    )(xp, xs, ycp, ycs, atp, ats, wuv_bd, ga, wo, g, b)


def _router_kernel(x_ref, wh_ref, wl_ref, b_ref, tri_ref, idx_ref, w_ref, rank_ref, cnt_ref, base_ref,
                   *, tt, n_exp):
    @pl.when(pl.program_id(0) == 0)
    def _():
        base_ref[...] = jnp.zeros(base_ref.shape, F32)

    x = x_ref[...]
    xh = x.astype(BF16)
    xl = (x - xh.astype(F32)).astype(BF16)
    wh = wh_ref[...]
    z = (lax.dot_general(wh, xh, _NT, preferred_element_type=F32)
         + lax.dot_general(wh, xl, _NT, preferred_element_type=F32)
         + lax.dot_general(wl_ref[...], xh, _NT, preferred_element_type=F32))
    s = 1.0 / (1.0 + jnp.exp(-z))
    sb = s + b_ref[...]

    gsz = n_exp // N_GROUPS
    git = lax.broadcasted_iota(jnp.int32, (gsz, tt), 0).astype(F32)
    blocks, gscore = [], []
    for g in range(N_GROUPS):
        blk = sb[g * gsz:(g + 1) * gsz, :]
        m1 = jnp.max(blk, axis=0, keepdims=True)
        f1 = jnp.min(jnp.where(blk == m1, git, float(gsz)), axis=0, keepdims=True)
        m2 = jnp.max(jnp.where(git == f1, -jnp.inf, blk), axis=0, keepdims=True)
        blocks.append(blk)
        gscore.append(m1 + m2)
    masked = []
    for g in range(N_GROUPS):
        ahead = jnp.zeros((1, tt), F32)
        for g2 in range(N_GROUPS):
            if g2 == g:
                continue
            beats = (gscore[g2] >= gscore[g]) if g2 < g else (gscore[g2] > gscore[g])
            ahead = ahead + jnp.where(beats, 1.0, 0.0)
        masked.append(jnp.where(ahead < float(TOPK_GROUPS), blocks[g], -jnp.inf))
    vals = jnp.concatenate(masked, axis=0)

    rowi = lax.broadcasted_iota(jnp.int32, (n_exp, tt), 0).astype(F32)
    vals0 = vals
    picks, wks = [], []
    for k in range(TOP_K):
        m = jnp.max(vals, axis=0, keepdims=True)
        ik = jnp.min(jnp.where(vals == m, rowi, float(n_exp)), axis=0, keepdims=True)
        hit = rowi == ik
        wks.append(jnp.sum(jnp.where(hit, s, 0.0), axis=0, keepdims=True))
        vals = jnp.where(hit, -jnp.inf, vals)
        picks.append(ik)
    chosen = jnp.where(vals != vals0, 1.0, 0.0)
    wsum = wks[0]
    for k in range(1, TOP_K):
        wsum = wsum + wks[k]

    incl = jnp.dot(chosen.astype(BF16), tri_ref[...], preferred_element_type=F32)
    rnk = base_ref[...] + (incl - chosen)
    for k in range(TOP_K):
        idx_ref[k:k + 1, :] = picks[k].astype(jnp.int32)
        w_ref[k:k + 1, :] = wks[k] / wsum * ROUTED_SCALE
        rk = jnp.sum(jnp.where(rowi == picks[k], rnk, 0.0), axis=0, keepdims=True)
        rank_ref[k:k + 1, :] = rk.astype(jnp.int32)
    base = base_ref[...] + jnp.sum(chosen, axis=1, keepdims=True)
    base_ref[...] = base
    cnt_ref[...] = jnp.broadcast_to(base, cnt_ref.shape)


def _router(x1, wr_hi, wr_lo, bias, *, tt):
    nt, d = x1.shape
    n_exp = wr_hi.shape[0]
    assert nt % tt == 0
    tri = jnp.triu(jnp.ones((tt, tt), BF16))
    const = lambda i: (0, 0)
    col = lambda i: (0, i)
    return pl.pallas_call(
        functools.partial(_router_kernel, tt=tt, n_exp=n_exp),
        grid=(nt // tt,),
        in_specs=[pl.BlockSpec((tt, d), lambda i: (i, 0)),
                  pl.BlockSpec(wr_hi.shape, const), pl.BlockSpec(wr_lo.shape, const),
                  pl.BlockSpec(bias.shape, const), pl.BlockSpec(tri.shape, const)],
        out_specs=[pl.BlockSpec((TOP_K, tt), col), pl.BlockSpec((TOP_K, tt), col),
                   pl.BlockSpec((TOP_K, tt), col), pl.BlockSpec((n_exp, LANES), const)],
        out_shape=[jax.ShapeDtypeStruct((TOP_K, nt), jnp.int32), jax.ShapeDtypeStruct((TOP_K, nt), F32),
                   jax.ShapeDtypeStruct((TOP_K, nt), jnp.int32), jax.ShapeDtypeStruct((n_exp, LANES), F32)],
        scratch_shapes=[pltpu.VMEM((n_exp, 1), F32)],
        compiler_params=_cparams(("arbitrary",)),
    )(x1, wr_hi, wr_lo, bias, tri)


def _slot_rows_kernel(idx_ref, rank_ref, ps_ref, o_ref, *, n_exp, sub):
    tt = idx_ref.shape[1]
    rowi = lax.broadcasted_iota(jnp.int32, (n_exp, tt), 0)
    ps = ps_ref[...]
    for k in range(TOP_K):
        start = jnp.sum(jnp.where(rowi == idx_ref[k:k + 1, :], ps, 0.0), axis=0, keepdims=True)
        o_ref[k:k + 1, :] = (start.astype(jnp.int32) + rank_ref[k:k + 1, :]) * sub


def _slot_rows(idx_t, rank_t, pad_start, *, tt, sub):
    nt = idx_t.shape[1]
    n_exp = pad_start.shape[0]
    assert nt % tt == 0 and n_exp * SLOT_ROWS + nt * TOP_K < 2 ** 24
    col = lambda i: (0, i)
    return pl.pallas_call(
        functools.partial(_slot_rows_kernel, n_exp=n_exp, sub=sub),
        grid=(nt // tt,),
        in_specs=[pl.BlockSpec((TOP_K, tt), col), pl.BlockSpec((TOP_K, tt), col),
                  pl.BlockSpec((n_exp, 1), lambda i: (0, 0))],
        out_specs=pl.BlockSpec((TOP_K, tt), col),
        out_shape=jax.ShapeDtypeStruct((TOP_K, nt), jnp.int32),
        compiler_params=_cparams(("parallel",)),
    )(idx_t, rank_t, pad_start.astype(F32).reshape(n_exp, 1))


def _dispatch_kernel(ps_ref, nb_ref, cnt_ref, dest_ref, x_ref, o_ref, zbuf, sem, zsem, *, td, sub, n_exp):
    i = pl.program_id(0)
    blk = SLOT_ROWS * sub

    @pl.when(i == 0)
    def _():
        zbuf[...] = jnp.zeros(zbuf.shape, zbuf.dtype)
        n_blocks = o_ref.shape[0] // blk
        used = ps_ref[n_exp - 1] // SLOT_ROWS + nb_ref[n_exp - 1]

        def zero_block(b):
            return pltpu.make_async_copy(zbuf, o_ref.at[pl.ds(pl.multiple_of(b * blk, blk), blk)], zsem)

        def partial(e):
            return (cnt_ref[e] & (SLOT_ROWS - 1)) != 0

        def fill(e, carry):
            @pl.when(partial(e))
            def _():
                zero_block(ps_ref[e] // SLOT_ROWS + nb_ref[e] - 1).start()
            return carry

        def fill_done(e, carry):
            @pl.when(partial(e))
            def _():
                zero_block(0).wait()
            return carry

        def tail(b, carry):
            zero_block(b).start()
            return carry

        def tail_done(b, carry):
            zero_block(0).wait()
            return carry

        lax.fori_loop(0, n_exp, fill, 0)
        lax.fori_loop(used, n_blocks, tail, 0)
        lax.fori_loop(0, n_exp, fill_done, 0)
        lax.fori_loop(used, n_blocks, tail_done, 0)

    def per_token(t, carry):
        src = x_ref.at[pl.ds(pl.multiple_of(t * sub, sub), sub)]
        for k in range(TOP_K):
            dst = pl.multiple_of(dest_ref[k, t], sub)
            pltpu.make_async_copy(src, o_ref.at[pl.ds(dst, sub)], sem).start(priority=k % 2)
        return carry

    lax.fori_loop(0, td, per_token, 0)
    for _ in range(TOP_K):
        pltpu.make_async_copy(x_ref, o_ref.at[pl.ds(0, td * sub)], sem).wait()


def _dispatch(pad_start, n_blk, counts, dest_t, x1p, n_slots, *, td, sub):
    nt = x1p.shape[0] // sub
    assert nt % td == 0 and n_slots >= td and n_slots % SLOT_ROWS == 0
    return pl.pallas_call(
        functools.partial(_dispatch_kernel, td=td, sub=sub, n_exp=counts.shape[0]),
        grid_spec=pltpu.PrefetchScalarGridSpec(
            num_scalar_prefetch=3,
            grid=(nt // td,),
            in_specs=[pl.BlockSpec((TOP_K, td), lambda i, *_: (0, i), memory_space=pltpu.SMEM),
                      pl.BlockSpec((td * sub, LANES), lambda i, *_: (i, 0))],
            out_specs=pl.BlockSpec(memory_space=pl.ANY),
            scratch_shapes=[pltpu.VMEM((SLOT_ROWS * sub, LANES), x1p.dtype), pltpu.SemaphoreType.DMA(()),
                            pltpu.SemaphoreType.DMA(())]),
        out_shape=jax.ShapeDtypeStruct((n_slots * sub, LANES), x1p.dtype),
        compiler_params=_cparams(("arbitrary",)),
    )(pad_start, n_blk, counts, dest_t, x1p)


EXPERT_RING = 8


def _experts_kernel(ps_ref, nb_ref, wg_ref, wu_ref, wd_ref, xs_hbm, ys_hbm, wg_s, wu_s, wd_s, xbuf, ybuf,
                    xsem, ysem, zsem, *, sub, n_exp):
    e = pl.program_id(0)
    blk = SLOT_ROWS * sub
    n_blocks = xs_hbm.shape[0] // blk
    used = ps_ref[n_exp - 1] // SLOT_ROWS + nb_ref[n_exp - 1]

    def rows(g):
        return pl.ds(pl.multiple_of(g * blk, blk), blk)

    def ring(buf, s, n=1):
        return buf.at[pl.ds(pl.multiple_of(s * blk, blk), n * blk)]

    def x_copy(g, s):
        return pltpu.make_async_copy(xs_hbm.at[rows(g)], ring(xbuf, s), xsem.at[s])

    def y_copy(g, s):
        return pltpu.make_async_copy(ring(ybuf, s), ys_hbm.at[rows(g)], ysem.at[s])

    @pl.when(e == 0)
    def _():
        zbuf = ring(ybuf, EXPERT_RING - 1)
        zbuf[...] = jnp.zeros(zbuf.shape, zbuf.dtype)

        def fill(b, carry):
            pltpu.make_async_copy(zbuf, ys_hbm.at[rows(b)], zsem).start()
            return carry

        def fill_done(b, carry):
            pltpu.make_async_copy(zbuf, ys_hbm.at[rows(0)], zsem).wait()
            return carry

        lax.fori_loop(used, n_blocks, fill, 0)
        lax.fori_loop(used, n_blocks, fill_done, 0)
        for s in range(EXPERT_RING):
            @pl.when(s < used)
            def _():
                x_copy(s, s).start()

    nb = nb_ref[e]
    first = ps_ref[e] // SLOT_ROWS

    @pl.when(nb > 0)
    def _():
        wg_s[...] = wg_ref[...].astype(BF16)
        wu_s[...] = wu_ref[...].astype(BF16)
        wd_s[...] = wd_ref[...].astype(BF16)

        def run(g, n):
            s = g & (EXPERT_RING - 1)
            for d in range(n):
                x_copy(g + d, s + d).wait()

            @pl.when(g >= EXPERT_RING)
            def _():
                for d in range(n):
                    y_copy(g + d, s + d).wait()

            lo, hi = _unpack_rows(ring(xbuf, s, n), 0, n * SLOT_ROWS, sub)
            x = jnp.concatenate(lo + hi, axis=1).astype(BF16)
            gate = jnp.dot(x, wg_s[...], preferred_element_type=F32)
            up = jnp.dot(x, wu_s[...], preferred_element_type=F32)
            hmid = (_silu(gate) * up).astype(BF16)
            _pack_rows(jnp.dot(hmid, wd_s[...], preferred_element_type=F32), ring(ybuf, s, n))
            for d in range(n):
                y_copy(g + d, s + d).start()
            for d in range(n):
                @pl.when(g + d + EXPERT_RING < used)
                def _():
                    x_copy(g + d + EXPERT_RING, s + d).start()

        lead = first & 1

        @pl.when(lead == 1)
        def _():
            run(first, 1)

        def pair(p, carry):
            run(first + lead + 2 * p, 2)
            return carry

        lax.fori_loop(0, (nb - lead) // 2, pair, 0)

        @pl.when(((nb - lead) & 1) == 1)
        def _():
            run(first + nb - 1, 1)

    @pl.when(e == n_exp - 1)
    def _():
        for s in range(EXPERT_RING):
            @pl.when(s < used)
            def _():
                y_copy(0, s).wait()


def _experts(pad_start, n_blk, xs, w_gate, w_up, w_down, layer):
    n_exp, d, de = w_gate.shape[-3:]
    sub = d // (2 * LANES)
    blk = SLOT_ROWS * sub
    assert xs.shape[0] % blk == 0 and xs.shape[1] == LANES
    return pl.pallas_call(
        functools.partial(_experts_kernel, sub=sub, n_exp=n_exp),
        grid_spec=pltpu.PrefetchScalarGridSpec(
            num_scalar_prefetch=2,
            grid=(n_exp,),
            in_specs=[pl.BlockSpec((None, None, d, de), lambda e, ps, nb: (layer, e, 0, 0)),
                      pl.BlockSpec((None, None, d, de), lambda e, ps, nb: (layer, e, 0, 0)),
                      pl.BlockSpec((None, None, de, d), lambda e, ps, nb: (layer, e, 0, 0)),
                      pl.BlockSpec(memory_space=pl.ANY)],
            out_specs=pl.BlockSpec(memory_space=pl.ANY),
            scratch_shapes=[pltpu.VMEM((d, de), BF16), pltpu.VMEM((d, de), BF16), pltpu.VMEM((de, d), BF16),
                            pltpu.VMEM((EXPERT_RING * blk, LANES), U32), pltpu.VMEM((EXPERT_RING * blk, LANES), U32),
                            pltpu.SemaphoreType.DMA((EXPERT_RING,)), pltpu.SemaphoreType.DMA((EXPERT_RING,)),
                            pltpu.SemaphoreType.DMA(())]),
        out_shape=jax.ShapeDtypeStruct(xs.shape, U32),
        compiler_params=_cparams(("arbitrary",)),
    )(pad_start, n_blk, w_gate, w_up, w_down, xs)


def _combine_kernel(dest0_ref, dest1_ref, w_ref, x1_ref, wsg_ref, wsu_ref,
                    wsd_ref, g_ref, b_ref, ys_ref, yp_ref, ysm_ref, buf, sem, *, tc, alpha, sub, n_prompt_tiles):
    i = pl.program_id(0)
    n = pl.num_programs(0)

    def gather_row(dest_ref, s, k, t):
        src = pl.multiple_of(dest_ref[k, t], sub)
        dst = (k * tc + t) * sub
        if not isinstance(dst, int):
            dst = pl.multiple_of(dst, sub)
        pltpu.make_async_copy(ys_ref.at[pl.ds(src, sub)], buf.at[s, pl.ds(dst, sub)], sem.at[s]).start(priority=k % 2)

    def drain(s):
        pltpu.make_async_copy(ys_ref.at[pl.ds(0, TOP_K * tc * sub)], buf.at[s], sem.at[s]).wait()

    @pl.when(i == 0)
    def _():
        def per_token(t, carry):
            for k in range(TOP_K):
                gather_row(dest0_ref, 0, k, t)
            return carry
        lax.fori_loop(0, tc, per_token, 0)

    slot = i % 2
    for t in range(tc):
        for k in range(TOP_K):
            gather_row(dest1_ref, 1 - slot, k, t)

    x1 = x1_ref[...]
    xb = x1.astype(BF16)
    hs = (_silu(jnp.dot(xb, wsg_ref[...], preferred_element_type=F32))
          * jnp.dot(xb, wsu_ref[...], preferred_element_type=F32)).astype(BF16)
    shared = jnp.dot(hs, wsd_ref[...], preferred_element_type=F32)

    drain(slot)
    w = w_ref[...]
    lo_acc = [jnp.zeros((tc, LANES), F32) for _ in range(sub)]
    hi_acc = [jnp.zeros((tc, LANES), F32) for _ in range(sub)]
    for k in range(TOP_K):
        lo, hi = _unpack_rows(buf.at[slot], k * tc, tc, sub)
        wk = w[:, k:k + 1]
        for j in range(sub):
            lo_acc[j] = lo_acc[j] + wk * lo[j]
            hi_acc[j] = hi_acc[j] + wk * hi[j]
    moe = jnp.concatenate(lo_acc + hi_acc, axis=1) + shared
    y = _layer_norm(alpha * x1 + moe, g_ref[...], b_ref[...])

    @pl.when(i < n_prompt_tiles)
    def _():
        yp_ref[...] = y

    @pl.when(i >= n_prompt_tiles)
    def _():
        ysm_ref[...] = y

    @pl.when(i == n - 1)
    def _():
        drain(1 - slot)


def _combine(dest_t, w_tok, x1, wsg, wsu, wsd, g, b, ys, *, tc, alpha, n_prompt):
    nt, d = x1.shape
    n_tiles = nt // tc
    sub = d // (2 * LANES)
    assert nt % tc == 0 and n_prompt % tc == 0 and 0 < n_prompt < nt and ys.shape[0] >= TOP_K * tc * sub
    npt = n_prompt // tc
    cur = pl.BlockSpec((TOP_K, tc), lambda i: (0, i), memory_space=pltpu.SMEM)
    nxt = pl.BlockSpec((TOP_K, tc), lambda i: (0, jnp.minimum(i + 1, n_tiles - 1)), memory_space=pltpu.SMEM)
    const = lambda i: (0, 0)
    tile = lambda i: (i, 0)
    return pl.pallas_call(
        functools.partial(_combine_kernel, tc=tc, alpha=alpha, sub=sub, n_prompt_tiles=npt),
        grid=(n_tiles,),
        in_specs=[cur, nxt,
                  pl.BlockSpec((tc, TOP_K), tile),
                  pl.BlockSpec((tc, d), tile),
                  pl.BlockSpec(wsg.shape, const), pl.BlockSpec(wsu.shape, const), pl.BlockSpec(wsd.shape, const),
                  pl.BlockSpec(g.shape, const), pl.BlockSpec(b.shape, const),
                  pl.BlockSpec(memory_space=pl.ANY)],
        out_specs=[pl.BlockSpec((tc, d), lambda i: (jnp.minimum(i, npt - 1), 0)),
                   pl.BlockSpec((tc, d), lambda i: (jnp.maximum(i - npt, 0), 0))],
        out_shape=[jax.ShapeDtypeStruct((n_prompt, d), F32), jax.ShapeDtypeStruct((nt - n_prompt, d), F32)],
        scratch_shapes=[pltpu.VMEM((2, TOP_K * tc * sub, LANES), ys.dtype), pltpu.SemaphoreType.DMA((2,))],
        compiler_params=_cparams(("arbitrary",)),
    )(dest_t, dest_t, w_tok, x1, wsg, wsu, wsd, g, b, ys)


def _rope_cs(pos, rope):
    half = rope // 2
    inv = ROPE_BASE ** (-jnp.arange(half, dtype=F32) / half)
    ang = pos.astype(F32)[:, None] * inv[None, :]
    cos, sin = jnp.cos(ang), jnp.sin(ang)
    pad = jnp.zeros((pos.shape[0], LANES - rope), F32)
    return (jnp.concatenate([cos, cos, pad], axis=1), jnp.concatenate([-sin, sin, pad], axis=1))


def _pick_tile(n, pref):
    t = pref
    while n % t:
        t //= 2
    return t


def kernel(x_prompt, x_sample, cache_kv_latent, cache_k_rope, state_conv, page_table, w_in, conv_w, q_norm, w_uq, kv_norm, w_uk, w_uv, g_conv, g_attn, w_o, ln1_g, ln1_b, w_router, router_bias, w_gate, w_up, w_down, ws_gate, ws_up, ws_down, ln2_g, ln2_b):
    depth = w_in.shape[0]
    bsz, seq, d = x_prompt.shape
    db, t_new, _ = x_sample.shape
    kv_lora, n_heads, nope = w_uk.shape[1:]
    v_dim = w_uv.shape[-1]
    rope = cache_k_rope.shape[-1]
    q_lora = q_norm.shape[-1]
    conv_ch = conv_w.shape[-1]
    n_exp = w_router.shape[-1]
    page = cache_kv_latent.shape[2]
    past_len = page_table.shape[1] * page
    alpha = (2.0 * depth) ** 0.25
    scale = float((nope + rope) ** -0.5) * math.log2(math.e)
    in_cols = w_in.shape[-1]
    in_pad = -(-(in_cols - rope + LANES) // LANES) * LANES
    n_p, n_s = bsz * seq, db * t_new
    nt = n_p + n_s

    tm_p = _pick_tile(seq, 512)
    tm_s = _pick_tile(n_s, 512)
    tq = _pick_tile(seq, 256)
    tm_o = _pick_tile(math.gcd(n_p, n_s), 1024)
    tt = _pick_tile(math.gcd(n_p, n_s), 512)
    td = max(t for t in range(LANES, 4096 + 1, LANES) if nt % t == 0) if nt % LANES == 0 else _pick_tile(nt, 1024)
    tc = _pick_tile(math.gcd(n_p, n_s), 512)
    pps = _pick_tile(page_table.shape[1], 64)
    cache_rope_t = jnp.swapaxes(cache_k_rope, 2, 3)

    cos_p, sin_p = _rope_cs(jnp.arange(seq, dtype=jnp.int32), rope)
    cos_s, sin_s = _rope_cs(past_len + jnp.arange(t_new, dtype=jnp.int32), rope)
    cos_s, sin_s = jnp.tile(cos_s, (tm_s // t_new, 1)), jnp.tile(sin_s, (tm_s // t_new, 1))

    xp, xs_ = x_prompt.reshape(n_p, d), x_sample.reshape(n_s, d)
    outs = [[] for _ in range(6)]
    for l in range(depth):
        win = jnp.pad(w_in[l], ((0, 0), (0, in_pad - in_cols))).astype(BF16)
        wuq3 = w_uq[l].reshape(q_lora, n_heads, nope + rope)
        qlat = _fold_qlat(jnp.transpose(wuq3[:, :, :nope], (1, 0, 2)), jnp.transpose(w_uk[l], (1, 2, 0)), scale)
        wq_rope = jnp.transpose(wuq3[:, :, nope:], (1, 0, 2)) * scale
        wq = jnp.concatenate([qlat, wq_rope, jnp.zeros((n_heads, q_lora, LANES - rope), F32)], axis=2)
        wq = jnp.transpose(wq, (1, 0, 2)).reshape(q_lora, n_heads * 2 * LANES).astype(BF16)
        eye = jnp.eye(n_heads, dtype=F32)
        wuv_bd = jnp.einsum('chv,hg->hcgv', w_uv[l], eye).reshape(n_heads * kv_lora, n_heads * v_dim).astype(BF16)
        wo = w_o[l].astype(BF16)
        wr_t = w_router[l].T
        wr_hi = wr_t.astype(BF16)
        wr_lo = (wr_t - wr_hi.astype(F32)).astype(BF16)
        row = lambda v: v[l].reshape(1, -1)

        common = (win, wq, row(q_norm), row(kv_norm), conv_w[l], row(g_conv))
        q_p, kc_p, ckv_p, kpe_p, u_p, yc_p = _mixer_in(
            xp, *common, cos_p, sin_p, None, prompt=True, seq=seq, tm=tm_p, n_heads=n_heads, rope=rope, q_dtype=BF16)
        st = state_conv[l].astype(F32)
        zero = jnp.zeros((db, conv_ch), F32)
        prev1 = jnp.stack([st[:, 1]] + [zero] * (t_new - 1), axis=1).reshape(n_s, conv_ch)
        prev2 = jnp.stack([st[:, 0], st[:, 1]] + [zero] * (t_new - 2), axis=1).reshape(n_s, conv_ch)
        q_s, kc_s, ckv_s, kpe_s, u_s, yc_s = _mixer_in(
            xs_, *common, cos_s, sin_s, (prev1, prev2), prompt=False, seq=t_new, tm=tm_s, n_heads=n_heads, rope=rope,
            q_dtype=F32)
        del kc_s

        at_p = _prompt_attention(q_p, kc_p, batch=bsz, seq=seq, tq=tq, n_heads=n_heads)
        at_s = _sample_attention(q_s, ckv_s, kpe_s, cache_kv_latent, cache_rope_t, page_table, l,
                                 n_heads=n_heads, t_new=t_new, cpages=pps)

        x1, x1p = _mixer_out(xp, xs_, yc_p, yc_s, at_p, at_s, wuv_bd, row(g_attn), wo, row(ln1_g), row(ln1_b),
                             tm=tm_o, alpha=alpha)

        idx_t, w_t, rank_t, cnt = _router(x1, wr_hi, wr_lo, router_bias[l].reshape(n_exp, 1), tt=tt)
        counts = cnt[:, 0].astype(jnp.int32)
        padded = (counts + SLOT_ROWS - 1) // SLOT_ROWS * SLOT_ROWS
        pad_end = jnp.cumsum(padded)
        pad_start = (pad_end - padded).astype(jnp.int32)
        n_blk = (padded // SLOT_ROWS).astype(jnp.int32)
        n_blocks = -(-(nt * TOP_K) // SLOT_ROWS) + n_exp

        dest_t = _slot_rows(idx_t, rank_t, pad_start, tt=td, sub=d // (2 * LANES))
        xs_sorted = _dispatch(pad_start, n_blk, counts, dest_t, x1p, n_blocks * SLOT_ROWS, td=td, sub=d // (2 * LANES))
        ys = _experts(pad_start, n_blk, xs_sorted, w_gate, w_up, w_down, l)
        xp, xs_ = _combine(dest_t, w_t.T, x1, ws_gate[l].astype(BF16), ws_up[l].astype(BF16),
                           ws_down[l].astype(BF16), row(ln2_g), row(ln2_b), ys, tc=tc, alpha=alpha, n_prompt=n_p)
        outs[0].append(ckv_p.reshape(bsz, seq, kv_lora))
        outs[1].append(kpe_p.reshape(bsz, seq, rope))
        outs[2].append(u_p.reshape(bsz, seq, conv_ch)[:, seq - (CONV_W - 1):])
        outs[3].append(ckv_s.reshape(db, t_new, kv_lora))
        outs[4].append(kpe_s.reshape(db, t_new, rope))
        outs[5].append(u_s.reshape(db, t_new, conv_ch)[:, t_new - (CONV_W - 1):])
    return (xp.reshape(bsz, seq, d), xs_.reshape(db, t_new, d)) + tuple(jnp.stack(o) for o in outs)
```
